```python
import jax, jax.numpy as jnp
from jax import lax
import numpy as np

D_MODEL = 1024
BATCH = 16
SEQ = 2048
DEPTH = 4

CHUNK = 64
RWKV_HEADS = 8
RWKV_HEAD_DIM = 64
RWKV_WIDTH = RWKV_HEADS * RWKV_HEAD_DIM
ATT_HEADS = 8
ATT_HEAD_DIM = 64
ATT_WIDTH = ATT_HEADS * ATT_HEAD_DIM
MIX_WIDTH = RWKV_WIDTH + ATT_WIDTH
DECAY_LORA = 64
AAA_LORA = 64
MV_LORA = 32
GATE_LORA = 128
RWKV_IN = 3 * RWKV_WIDTH + DECAY_LORA + AAA_LORA + GATE_LORA
RWKV_SPLITS = (RWKV_WIDTH, 2 * RWKV_WIDTH, 3 * RWKV_WIDTH,
               3 * RWKV_WIDTH + DECAY_LORA, 3 * RWKV_WIDTH + DECAY_LORA + AAA_LORA)
ATT_IN = 3 * ATT_WIDTH
IN_WIDTH = RWKV_IN + ATT_IN
PREV_CHUNKS = 8
BAND = (PREV_CHUNKS + 1) * CHUNK
REL_MAX = 256
REL_BUCKETS = REL_MAX + CHUNK
N_MEM = 256
XA_HEADS = 4
XA_HEAD_DIM = 128
XA_WIDTH = XA_HEADS * XA_HEAD_DIM
N_GROUPS = 4
EXPERTS_PER_GROUP = 8
N_EXPERTS = N_GROUPS * EXPERTS_PER_GROUP
TOP_K = 2
EXPERT_FF = 512
MOE_BLOCK = 128
NORM_EPS = 1e-6
LNX_EPS = 64e-5
NEG_INF = -1e30

kernel_name = 'hybrid_rwkv7_chunkattn_hmoe'


def rms_norm(x, g, eps=NORM_EPS):
    x32 = x.astype(jnp.float32)
    y = x32 * lax.rsqrt(jnp.mean(x32 * x32, axis=-1, keepdims=True) + eps)
    return (y * g.astype(jnp.float32)).astype(x.dtype)


def token_shift(u):
    return jnp.pad(u, ((0, 0), (1, 0), (0, 0)))[:, :-1]


def wkv7_scan(r, decay, k, v, a, b):
    B, S, H, N = r.shape

    def step(state, inp):
        r_t, w_t, k_t, v_t, a_t, b_t = inp
        sa = jnp.einsum('bhvk,bhk->bhv', state, a_t)
        state = (state * w_t[:, :, None, :] + sa[..., None] * b_t[:, :, None, :]
                 + v_t[..., None] * k_t[:, :, None, :])
        return state, jnp.einsum('bhvk,bhk->bhv', state, r_t)

    xs = tuple(jnp.moveaxis(t, 1, 0) for t in (r, decay, k, v, a, b))
    s0 = jnp.zeros((B, H, N, N), jnp.float32)
    _, ys = lax.scan(step, s0, xs)
    return jnp.moveaxis(ys, 0, 1)


def rwkv7_time_mix(u, v_first, mu, w0, w2, a0, a2, g2, k_k, k_a, r_k, lnx_w, lnx_b, vres):
    B, S, _ = u.shape
    H, N = RWKV_HEADS, RWKV_HEAD_DIM
    u = u.astype(jnp.float32)
    u = u + mu * (token_shift(u) - u)
    r, k, v, wl, al, gl = jnp.split(u, RWKV_SPLITS, axis=-1)
    w = w0 + jnp.tanh(wl) @ w2
    w = -jax.nn.softplus(-w) - 0.5
    decay = jnp.exp(-jnp.exp(w))
    a = jax.nn.sigmoid(a0 + al @ a2)
    g = jax.nn.sigmoid(gl) @ g2
    if vres is None:
        v_first = v
    else:
        v0, v1, v2 = vres
        v = v + (v_first - v) * jax.nn.sigmoid(v0 + (v @ v1) @ v2)
    kk = (k * k_k).reshape(B, S, H, N)
    kk = kk * lax.rsqrt(jnp.maximum(jnp.sum(kk * kk, axis=-1, keepdims=True), 1e-24))
    k = k * (1.0 + (a - 1.0) * k_a)
    r_h = r.reshape(B, S, H, N)
    k_h = k.reshape(B, S, H, N)
    v_h = v.reshape(B, S, H, N)
    a_h = a.reshape(B, S, H, N)
    y = wkv7_scan(r_h, decay.reshape(B, S, H, N), k_h, v_h, -kk, kk * a_h)
    mean = jnp.mean(y, axis=-1, keepdims=True)
    var = jnp.mean(jnp.square(y - mean), axis=-1, keepdims=True)
    y = ((y - mean) * lax.rsqrt(var + LNX_EPS)).reshape(B, S, RWKV_WIDTH) * lnx_w + lnx_b
    bonus = jnp.sum(r_h * k_h * r_k, axis=-1, keepdims=True) * v_h
    y = (y + bonus.reshape(B, S, RWKV_WIDTH)) * g
    return y, v_first


def rel_bias_band(table):
    q_off = jnp.arange(CHUNK)[:, None]
    k_off = jnp.arange(BAND)[None, :] - PREV_CHUNKS * CHUNK
    dist = jnp.clip(q_off - k_off, -(CHUNK - 1), REL_MAX) + (CHUNK - 1)
    return table[:, dist].astype(jnp.float32)


def chunk_band_attention(q, k, v, bias):
    B, S, H, Dh = q.shape
    nc = S // CHUNK
    pad = PREV_CHUNKS * CHUNK
    qc = q.reshape(B, nc, CHUNK, H, Dh).transpose(1, 0, 3, 2, 4)
    kp = jnp.pad(k, ((0, 0), (pad, 0), (0, 0), (0, 0))).transpose(0, 2, 1, 3)
    vp = jnp.pad(v, ((0, 0), (pad, 0), (0, 0), (0, 0))).transpose(0, 2, 1, 3)
    band_off = jnp.arange(BAND) - pad
    scale = Dh ** -0.5

    def one_chunk(args):
        c, qb = args
        start = c * CHUNK
        kb = lax.dynamic_slice_in_dim(kp, start, BAND, axis=2)
        vb = lax.dynamic_slice_in_dim(vp, start, BAND, axis=2)
        s = jnp.einsum('bhqd,bhkd->bhqk', qb.astype(jnp.float32), kb.astype(jnp.float32)) * scale + bias
        s = jnp.where((start + band_off) >= 0, s, NEG_INF)
        p = jax.nn.softmax(s, axis=-1)
        return jnp.einsum('bhqk,bhkd->bhqd', p.astype(vb.dtype), vb)

    out = lax.map(one_chunk, (jnp.arange(nc, dtype=jnp.int32), qc))
    return out.transpose(1, 0, 3, 2, 4).reshape(B, S, H * Dh)


def memory_cross_attention(h, mem_n, w_q, w_kv, w_o, q_norm, k_norm):
    B, S, _ = h.shape
    M = mem_n.shape[1]
    q = (h @ w_q).reshape(B, S, XA_HEADS, XA_HEAD_DIM)
    k, v = jnp.split(mem_n @ w_kv, 2, axis=-1)
    k = k.reshape(B, M, XA_HEADS, XA_HEAD_DIM)
    v = v.reshape(B, M, XA_HEADS, XA_HEAD_DIM)
    q = rms_norm(q, q_norm)
    k = rms_norm(k, k_norm)
    s = jnp.einsum('bshd,bmhd->bhsm', q.astype(jnp.float32), k.astype(jnp.float32)) * (XA_HEAD_DIM ** -0.5)
    p = jax.nn.softmax(s, axis=-1)
    o = jnp.einsum('bhsm,bmhd->bshd', p.astype(v.dtype), v).reshape(B, S, XA_WIDTH)
    return o @ w_o


def hierarchical_moe(h, w_group, b_group, w_route, b_route, w_gate, w_up, w_down):
    B, S, D = h.shape
    T = B * S
    xt = h.reshape(T, D)
    group_p = jax.nn.softmax((xt @ w_group).astype(jnp.float32) + b_group, axis=-1)
    g_top = jnp.argmax(group_p, axis=-1).astype(jnp.int32)
    g_gate = jnp.take_along_axis(group_p, g_top[:, None], axis=-1)[:, 0]
    logits = ((xt @ w_route).astype(jnp.float32) + b_route).reshape(T, N_GROUPS, EXPERTS_PER_GROUP)
    logits = jnp.take_along_axis(logits, g_top[:, None, None], axis=1)[:, 0]
    p_in = jax.nn.softmax(logits, axis=-1)
    top_p, top_j = lax.top_k(p_in, TOP_K)
    top_p = top_p / jnp.sum(top_p, axis=-1, keepdims=True)
    weights = g_gate[:, None] * top_p
    expert_id = g_top[:, None] * EXPERTS_PER_GROUP + top_j.astype(jnp.int32)
    A = T * TOP_K
    e = expert_id.reshape(A)
    tok = jnp.repeat(jnp.arange(T, dtype=jnp.int32), TOP_K)
    wts = weights.reshape(A)
    order = jnp.argsort(e)
    e_s, tok_s, w_s = e[order], tok[order], wts[order]
    counts = jnp.bincount(e, length=N_EXPERTS)
    starts = jnp.cumsum(counts) - counts
    padded = ((counts + MOE_BLOCK - 1) // MOE_BLOCK) * MOE_BLOCK
    pends = jnp.cumsum(padded)
    pstarts = pends - padded
    dest = pstarts[e_s] + (jnp.arange(A, dtype=jnp.int32) - starts[e_s])
    n_blocks = (A + N_EXPERTS * (MOE_BLOCK - 1) + MOE_BLOCK - 1) // MOE_BLOCK
    L = n_blocks * MOE_BLOCK
    buf_tok = jnp.full((L,), T, jnp.int32).at[dest].set(tok_s)
    buf_w = jnp.zeros((L,), wts.dtype).at[dest].set(w_s)
    block_expert = jnp.minimum(
        jnp.searchsorted(pends, jnp.arange(n_blocks) * MOE_BLOCK, side='right'), N_EXPERTS - 1).astype(jnp.int32)
    xs = jnp.concatenate([xt, jnp.zeros((1, D), xt.dtype)], axis=0)[buf_tok].reshape(n_blocks, MOE_BLOCK, D)

    def expert_block(args):
        xb, eid = args
        return (jax.nn.silu(xb @ w_gate[eid]) * (xb @ w_up[eid])) @ w_down[eid]

    yb = lax.map(expert_block, (xs, block_expert)).reshape(L, D)
    out = jnp.zeros((T + 1, D), yb.dtype).at[buf_tok].add(yb * buf_w[:, None].astype(yb.dtype))
    return out[:T].reshape(B, S, D)


def setup_inputs(seed: int = 0) -> dict:
    key = jax.random.key(seed)
    ks = iter(jax.random.split(key, 48))

    def nrm(shape, scale):
        return jax.random.normal(next(ks), shape, jnp.float32) * scale

    L, LV, D = DEPTH, DEPTH - 1, D_MODEL
    return {
        'x': nrm((BATCH, SEQ, D), 1.0),
        'mem': nrm((BATCH, N_MEM, D), 1.0),
        'norm_mix': 1.0 + nrm((L, D), 0.02),
        'w_in': nrm((L, D, IN_WIDTH), D ** -0.5),
        'shift_mu': jax.random.uniform(next(ks), (L, RWKV_IN), jnp.float32),
        'decay_w0': jax.random.uniform(next(ks), (L, RWKV_WIDTH), jnp.float32, -6.0, -1.0),
        'decay_w2': nrm((L, DECAY_LORA, RWKV_WIDTH), 0.1),
        'iclr_a0': nrm((L, RWKV_WIDTH), 0.1),
        'iclr_a2': nrm((L, AAA_LORA, RWKV_WIDTH), 0.1),
        'gate_g2': nrm((L, GATE_LORA, RWKV_WIDTH), GATE_LORA ** -0.5),
        'rw_k_k': 0.85 + nrm((L, RWKV_WIDTH), 0.05),
        'rw_k_a': 1.0 + nrm((L, RWKV_WIDTH), 0.05),
        'rw_r_k': nrm((L, RWKV_HEADS, RWKV_HEAD_DIM), 0.1),
        'lnx_w': 1.0 + nrm((L, RWKV_WIDTH), 0.05),
        'lnx_b': nrm((L, RWKV_WIDTH), 0.01),
        'vres_v0': 1.0 + nrm((LV, RWKV_WIDTH), 0.1),
        'vres_v1': nrm((LV, RWKV_WIDTH, MV_LORA), RWKV_WIDTH ** -0.5),
        'vres_v2': nrm((LV, MV_LORA, RWKV_WIDTH), 0.1),
        'att_q_norm': 1.0 + nrm((L, ATT_HEAD_DIM), 0.02),
        'att_k_norm': 1.0 + nrm((L, ATT_HEAD_DIM), 0.02),
        'att_rel_bias': nrm((L, ATT_HEADS, REL_BUCKETS), 0.5),
        'att_out_norm': 1.0 + nrm((L, ATT_WIDTH), 0.02),
        'w_out': nrm((L, MIX_WIDTH, D), 0.5 * MIX_WIDTH ** -0.5),
        'norm_xa': 1.0 + nrm((L, D), 0.02),
        'norm_mem': 1.0 + nrm((L, D), 0.02),
        'xa_w_q': nrm((L, D, XA_WIDTH), D ** -0.5),
        'xa_w_kv': nrm((L, D, 2 * XA_WIDTH), D ** -0.5),
        'xa_w_o': nrm((L, XA_WIDTH, D), 0.5 * XA_WIDTH ** -0.5),
        'xa_q_norm': 1.0 + nrm((L, XA_HEAD_DIM), 0.02),
        'xa_k_norm': 1.0 + nrm((L, XA_HEAD_DIM), 0.02),
        'norm_ffn': 1.0 + nrm((L, D), 0.02),
        'moe_w_group': nrm((L, D, N_GROUPS), D ** -0.5),
        'moe_b_group': nrm((L, N_GROUPS), 0.01),
        'moe_w_route': nrm((L, D, N_EXPERTS), D ** -0.5),
        'moe_b_route': nrm((L, N_EXPERTS), 0.01),
        'moe_w_gate': nrm((L, N_EXPERTS, D, EXPERT_FF), D ** -0.5),
        'moe_w_up': nrm((L, N_EXPERTS, D, EXPERT_FF), D ** -0.5),
        'moe_w_down': nrm((L, N_EXPERTS, EXPERT_FF, D), 0.5 * EXPERT_FF ** -0.5),
    }


def reference(x, mem, norm_mix, w_in, shift_mu, decay_w0, decay_w2, iclr_a0, iclr_a2, gate_g2,
              rw_k_k, rw_k_a, rw_r_k, lnx_w, lnx_b, vres_v0, vres_v1, vres_v2,
              att_q_norm, att_k_norm, att_rel_bias, att_out_norm, w_out,
              norm_xa, norm_mem, xa_w_q, xa_w_kv, xa_w_o, xa_q_norm, xa_k_norm,
              norm_ffn, moe_w_group, moe_b_group, moe_w_route, moe_b_route,
              moe_w_gate, moe_w_up, moe_w_down):
    B, S, _ = x.shape
    v_first = None
    for l in range(DEPTH):
        h = rms_norm(x, norm_mix[l])
        proj = h @ w_in[l]
        vres = None if l == 0 else (vres_v0[l - 1], vres_v1[l - 1], vres_v2[l - 1])
        y_r, v_first = rwkv7_time_mix(proj[..., :RWKV_IN], v_first, shift_mu[l], decay_w0[l], decay_w2[l],
                                      iclr_a0[l], iclr_a2[l], gate_g2[l], rw_k_k[l], rw_k_a[l], rw_r_k[l],
                                      lnx_w[l], lnx_b[l], vres)
        q, k, v = jnp.split(proj[..., RWKV_IN:], 3, axis=-1)
        q = rms_norm(q.reshape(B, S, ATT_HEADS, ATT_HEAD_DIM), att_q_norm[l])
        k = rms_norm(k.reshape(B, S, ATT_HEADS, ATT_HEAD_DIM), att_k_norm[l])
        v = v.reshape(B, S, ATT_HEADS, ATT_HEAD_DIM)
        y_a = chunk_band_attention(q, k, v, rel_bias_band(att_rel_bias[l]))
        y_a = rms_norm(y_a, att_out_norm[l])
        x = x + jnp.concatenate([y_r.astype(x.dtype), y_a.astype(x.dtype)], axis=-1) @ w_out[l]
        x = x + memory_cross_attention(rms_norm(x, norm_xa[l]), rms_norm(mem, norm_mem[l]),
                                       xa_w_q[l], xa_w_kv[l], xa_w_o[l], xa_q_norm[l], xa_k_norm[l])
        x = x + hierarchical_moe(rms_norm(x, norm_ffn[l]), moe_w_group[l], moe_b_group[l], moe_w_route[l],
                                 moe_b_route[l], moe_w_gate[l], moe_w_up[l], moe_w_down[l])
    return x
```

```python
import functools

import jax
import jax.numpy as jnp
from jax import lax
from jax.experimental import pallas as pl
from jax.experimental.pallas import tpu as pltpu

F32 = jnp.float32
BF16 = jnp.bfloat16

CHUNK = 64
RWKV_HEADS = 8
HEAD_DIM = 64
RWKV_WIDTH = 512
ATT_HEADS = 8
ATT_WIDTH = 512
DECAY_LORA = 64
AAA_LORA = 64
MV_LORA = 32
GATE_LORA = 128
RWKV_IN = 3 * RWKV_WIDTH + DECAY_LORA + AAA_LORA + GATE_LORA
LORA_OFF = 3 * RWKV_WIDTH
GATE_OFF = LORA_OFF + DECAY_LORA + AAA_LORA
IN_WIDTH = RWKV_IN + 3 * ATT_WIDTH
PREV_CHUNKS = 8
BAND = (PREV_CHUNKS + 1) * CHUNK
BAND_PAD = PREV_CHUNKS * CHUNK
REL_MAX = 256
XA_HEADS = 4
XA_HEAD_DIM = 128
XA_WIDTH = 512
N_GROUPS = 4
EXPERTS_PER_GROUP = 8
N_EXPERTS = 32
TOP_K = 2
EXPERT_FF = 512
MOE_BLOCK = 128
NORM_EPS = 1e-6
LNX_EPS = 64e-5
NEG_INF = -1e30

LANES = 128
SUBLANES = 8
ROW_TILE = 256
SCAN_STEPS = 32
ATT_ROWS = 256
MOE_ROWS = 512
ROUTE_LANES = 128


def _cparams(semantics, vmem_mib):
    return pltpu.CompilerParams(dimension_semantics=semantics,
                                vmem_limit_bytes=vmem_mib * 1024 * 1024)


def _dot(a, b):
    return jnp.dot(a, b, preferred_element_type=F32)


def _dot_nt(a, b):
    return lax.dot_general(a, b, (((1,), (1,)), ((), ())), preferred_element_type=F32)


def _rms(x, g):
    ms = jnp.mean(x * x, axis=-1, keepdims=True)
    return x * lax.rsqrt(ms + NORM_EPS) * g


def _group_sumsq(x, ones_bd):
    sq = x * x
    hi = sq.astype(BF16)
    lo = (sq - hi.astype(F32)).astype(BF16)
    return _dot(hi, ones_bd) + _dot(lo, ones_bd)


def _inproj_kernel(x_ref, g_ref, w_ref, gq_ref, gk_ref, bd_ref, u_ref, q_ref, k_ref, v_ref):
    h = _rms(x_ref[...], g_ref[...]).astype(BF16)
    for j in range(0, RWKV_IN, 256):
        u_ref[:, j:j + 256] = _dot(h, w_ref[:, j:j + 256])
    bd = bd_ref[...]
    q = _dot(h, w_ref[:, RWKV_IN:RWKV_IN + ATT_WIDTH])
    qn = q * lax.rsqrt(_group_sumsq(q, bd) * (1.0 / HEAD_DIM) + NORM_EPS) * gq_ref[...]
    q_ref[...] = (qn * (HEAD_DIM ** -0.5)).astype(BF16)
    k = _dot(h, w_ref[:, RWKV_IN + ATT_WIDTH:RWKV_IN + 2 * ATT_WIDTH])
    kn = k * lax.rsqrt(_group_sumsq(k, bd) * (1.0 / HEAD_DIM) + NORM_EPS) * gk_ref[...]
    k_ref[...] = kn.astype(BF16)
    v_ref[...] = _dot(h, w_ref[:, RWKV_IN + 2 * ATT_WIDTH:]).astype(BF16)


def _inproj(x2d, g, w, gq, gk, bd):
    T, D = x2d.shape
    tm = ROW_TILE
    full = lambda i: (0, 0)
    row = lambda i: (i, 0)
    return pl.pallas_call(
        _inproj_kernel,
        grid=(T // tm,),
        in_specs=[pl.BlockSpec((tm, D), row), pl.BlockSpec((1, D), full),
                  pl.BlockSpec((D, IN_WIDTH), full), pl.BlockSpec((1, ATT_WIDTH), full),
                  pl.BlockSpec((1, ATT_WIDTH), full), pl.BlockSpec((ATT_WIDTH, ATT_WIDTH), full)],
        out_specs=[pl.BlockSpec((tm, RWKV_IN), row), pl.BlockSpec((tm, ATT_WIDTH), row),
                   pl.BlockSpec((tm, ATT_WIDTH), row), pl.BlockSpec((tm, ATT_WIDTH), row)],
        out_shape=[jax.ShapeDtypeStruct((T, RWKV_IN), F32), jax.ShapeDtypeStruct((T, ATT_WIDTH), BF16),
                   jax.ShapeDtypeStruct((T, ATT_WIDTH), BF16), jax.ShapeDtypeStruct((T, ATT_WIDTH), BF16)],
        compiler_params=_cparams(("parallel",), 48),
        name="inproj",
    )(x2d, g, w, gq, gk, bd)


def _rwkv_prep_kernel(has_vres, *refs):
    if has_vres:
        (u_ref, up_ref, mu_ref, w0_ref, w2_ref, a0_ref, a2_ref, g2_ref,
         vf_ref, v0_ref, v1_ref, v2_ref,
         r_ref, w_ref, k_ref, v_ref, a_ref, g_ref) = refs
    else:
        (u_ref, up_ref, mu_ref, w0_ref, w2_ref, a0_ref, a2_ref, g2_ref,
         r_ref, w_ref, k_ref, v_ref, a_ref, g_ref) = refs
    i = pl.program_id(1)
    u = u_ref[...]
    ts = u.shape[0]
    prev_row = jnp.where(i > 0, up_ref[SUBLANES - 1:SUBLANES, :], 0.0)
    rolled = pltpu.roll(u, 1, 0)
    row_id = lax.broadcasted_iota(jnp.int32, u.shape, 0)
    shifted = jnp.where(row_id == 0, jnp.broadcast_to(prev_row, u.shape), rolled)
    u = u + mu_ref[...] * (shifted - u)

    r_ref[...] = u[:, 0:RWKV_WIDTH]
    k_ref[...] = u[:, RWKV_WIDTH:2 * RWKV_WIDTH]
    v = u[:, 2 * RWKV_WIDTH:3 * RWKV_WIDTH]
    lora = u[:, LORA_OFF:GATE_OFF]
    gl = u[:, GATE_OFF:RWKV_IN]

    w = w0_ref[...] + _dot(jnp.tanh(lora).astype(BF16), w2_ref[...])
    z = -w
    softplus = jnp.maximum(z, 0.0) + jnp.log1p(jnp.exp(-jnp.abs(z)))
    w = -softplus - 0.5
    w_ref[...] = jnp.exp(-jnp.exp(w))
    a = jax.nn.sigmoid(a0_ref[...] + _dot(lora.astype(BF16), a2_ref[...]))
    a_ref[...] = a
    g_ref[...] = _dot(jax.nn.sigmoid(gl).astype(BF16), g2_ref[...])
    if has_vres:
        mix = _dot(_dot(v.astype(BF16), v1_ref[...]).astype(BF16), v2_ref[...])
        v = v + (vf_ref[...] - v) * jax.nn.sigmoid(v0_ref[...] + mix)
    v_ref[...] = v


def _rwkv_prep(u3, mu, w0, w2p, a0, a2p, g2, vres):
    B, S, _ = u3.shape
    ts = ROW_TILE
    W = RWKV_WIDTH
    full2 = lambda b, i: (0, 0)
    tile = lambda b, i: (b, i, 0)
    prev = lambda b, i: (b, jnp.maximum(i * (ts // SUBLANES) - 1, 0), 0)
    in_specs = [pl.BlockSpec((None, ts, RWKV_IN), tile),
                pl.BlockSpec((None, SUBLANES, RWKV_IN), prev),
                pl.BlockSpec((1, RWKV_IN), full2), pl.BlockSpec((1, W), full2),
                pl.BlockSpec((LANES, W), full2), pl.BlockSpec((1, W), full2),
                pl.BlockSpec((LANES, W), full2), pl.BlockSpec((GATE_LORA, W), full2)]
    args = [u3, u3, mu, w0, w2p, a0, a2p, g2]
    if vres is not None:
        v_first, v0, v1p, v2p = vres
        in_specs += [pl.BlockSpec((None, ts, W), tile), pl.BlockSpec((1, W), full2),
                     pl.BlockSpec((W, LANES), full2), pl.BlockSpec((LANES, W), full2)]
        args += [v_first, v0, v1p, v2p]
    out_sds = jax.ShapeDtypeStruct((B, S, W), F32)
    return pl.pallas_call(
        functools.partial(_rwkv_prep_kernel, vres is not None),
        grid=(B, S // ts),
        in_specs=in_specs,
        out_specs=[pl.BlockSpec((None, ts, W), tile)] * 6,
        out_shape=[out_sds] * 6,
        compiler_params=_cparams(("parallel", "parallel"), 48),
        name="rwkv_prep",
    )(*args)


def _wkv_kernel(r_ref, w_ref, k_ref, v_ref, a_ref, kk_ref, ka_ref, rk_ref, lw_ref, lb_ref,
                y_ref, s_ref, vec_ref):
    N = HEAD_DIM

    @pl.when(pl.program_id(0) == 0)
    def _():
        s_ref[...] = jnp.zeros_like(s_ref)

    kkp = kk_ref[...]
    kap = ka_ref[...]
    rkp = rk_ref[...]
    lwp = lw_ref[...]
    lbp = lb_ref[...]

    def bcast(row):
        return jnp.broadcast_to(row, (N, LANES))

    def step(t, carry):
        r_t = r_ref[t]
        k_raw = k_ref[t]
        a_t = a_ref[t]
        v_t = v_ref[t]
        kk = k_raw * kkp
        ss = jnp.sum(kk * kk, axis=0, keepdims=True)
        kk = kk * lax.rsqrt(jnp.maximum(ss, 1e-24))
        k2 = k_raw * (1.0 + (a_t - 1.0) * kap)
        vec_ref[0] = -kk
        vec_ref[1] = kk * a_t
        vec_ref[2] = k2

        sa = jnp.zeros((N, LANES), F32)
        for j in range(N):
            sa = sa + s_ref[j] * bcast(vec_ref[0, j:j + 1, :])
        y = jnp.zeros((N, LANES), F32)
        for j in range(N):
            sn = (s_ref[j] * bcast(w_ref[t, j:j + 1, :]) + sa * bcast(vec_ref[1, j:j + 1, :])
                  + v_t * bcast(vec_ref[2, j:j + 1, :]))
            s_ref[j] = sn
            y = y + sn * bcast(r_ref[t, j:j + 1, :])

        mean = jnp.mean(y, axis=0, keepdims=True)
        yc = y - mean
        var = jnp.mean(yc * yc, axis=0, keepdims=True)
        yn = yc * lax.rsqrt(var + LNX_EPS) * lwp + lbp
        bonus = jnp.sum(r_t * k2 * rkp, axis=0, keepdims=True) * v_t
        y_ref[t] = yn + bonus
        return carry

    lax.fori_loop(0, r_ref.shape[0], step, 0)


def _wkv_scan(r, w, k, v, a, kkp, kap, rkp, lwp, lbp):
    S = r.shape[0]
    tt = SCAN_STEPS
    N = HEAD_DIM
    blk = pl.BlockSpec((tt, N, LANES), lambda i: (i, 0, 0))
    par = pl.BlockSpec((N, LANES), lambda i: (0, 0))
    return pl.pallas_call(
        _wkv_kernel,
        grid=(S // tt,),
        in_specs=[blk] * 5 + [par] * 5,
        out_specs=blk,
        out_shape=jax.ShapeDtypeStruct((S, N, LANES), F32),
        scratch_shapes=[pltpu.VMEM((N, N, LANES), F32), pltpu.VMEM((3, N, LANES), F32)],
        compiler_params=_cparams(("arbitrary",), 32),
        name="wkv_scan",
    )(r, w, k, v, a, kkp, kap, rkp, lwp, lbp)


def _band_attn_kernel(q_ref, k_ref, v_ref, bias_ref, o_ref):
    i = pl.program_id(1)
    n_chunks = q_ref.shape[0] // CHUNK
    lane = lax.broadcasted_iota(jnp.int32, (CHUNK, LANES), 1)
    even = lane < HEAD_DIM
    slot = lax.broadcasted_iota(jnp.int32, (2 * CHUNK, BAND), 1)

    def chunk_body(c, carry):
        cg = i * n_chunks + c
        start = pl.multiple_of(cg * CHUNK, CHUNK)
        row0 = pl.multiple_of(c * CHUNK, CHUNK)
        valid = slot >= (BAND_PAD - cg * CHUNK)
        for p in range(ATT_HEADS // 2):
            cols = slice(p * LANES, (p + 1) * LANES)
            q2 = q_ref[pl.ds(row0, CHUNK), cols]
            zero = jnp.zeros_like(q2)
            qs = jnp.concatenate([jnp.where(even, q2, zero), jnp.where(even, zero, q2)], axis=0)
            kb = k_ref[pl.ds(start, BAND), cols]
            vb = v_ref[pl.ds(start, BAND), cols]
            s = _dot_nt(qs, kb) + bias_ref[p]
            s = jnp.where(valid, s, NEG_INF)
            m = jnp.max(s, axis=-1, keepdims=True)
            e = jnp.exp(s - m)
            pr = e / jnp.sum(e, axis=-1, keepdims=True)
            o = _dot(pr.astype(BF16), vb)
            o_ref[pl.ds(row0, CHUNK), cols] = jnp.where(even, o[:CHUNK], o[CHUNK:])
        return carry

    lax.fori_loop(0, n_chunks, chunk_body, 0)


def _band_attn(q3, kp3, vp3, bias_pairs):
    B, S, W = q3.shape
    SP = kp3.shape[1]
    tq = ATT_ROWS
    return pl.pallas_call(
        _band_attn_kernel,
        grid=(B, S // tq),
        in_specs=[pl.BlockSpec((None, tq, W), lambda b, i: (b, i, 0)),
                  pl.BlockSpec((None, SP, W), lambda b, i: (b, 0, 0)),
                  pl.BlockSpec((None, SP, W), lambda b, i: (b, 0, 0)),
                  pl.BlockSpec((ATT_HEADS // 2, 2 * CHUNK, BAND), lambda b, i: (0, 0, 0))],
        out_specs=pl.BlockSpec((None, tq, W), lambda b, i: (b, i, 0)),
        out_shape=jax.ShapeDtypeStruct((B, S, W), F32),
        compiler_params=_cparams(("parallel", "arbitrary"), 48),
        name="band_attn",
    )(q3, kp3, vp3, bias_pairs)


def _mem_kv_kernel(m_ref, g_ref, w_ref, kn_ref, k_ref, v_ref):
    h = _rms(m_ref[...], g_ref[...]).astype(BF16)
    kv = _dot(h, w_ref[...])
    for hd in range(XA_HEADS):
        cols = slice(hd * XA_HEAD_DIM, (hd + 1) * XA_HEAD_DIM)
        k_ref[:, cols] = _rms(kv[:, cols], kn_ref[...]).astype(BF16)
    v_ref[...] = kv[:, XA_WIDTH:].astype(BF16)


def _mem_kv(mem2d, g, w_kv, k_norm):
    R, D = mem2d.shape
    tm = ROW_TILE
    full = lambda i: (0, 0)
    row = lambda i: (i, 0)
    return pl.pallas_call(
        _mem_kv_kernel,
        grid=(R // tm,),
        in_specs=[pl.BlockSpec((tm, D), row), pl.BlockSpec((1, D), full),
                  pl.BlockSpec((D, 2 * XA_WIDTH), full), pl.BlockSpec((1, XA_HEAD_DIM), full)],
        out_specs=[pl.BlockSpec((tm, XA_WIDTH), row)] * 2,
        out_shape=[jax.ShapeDtypeStruct((R, XA_WIDTH), BF16)] * 2,
        compiler_params=_cparams(("parallel",), 32),
        name="mem_kv",
    )(mem2d, g, w_kv, k_norm)


def _mix_xa_router_kernel(x_ref, yr_ref, g_ref, ya_ref, gao_ref, wout_ref, gxa_ref, wq_ref, qn_ref,
                          km_ref, vm_ref, wo_ref, gff_ref, wrt_ref, brt_ref,
                          xo_ref, h_ref, rt_ref):
    yr = (yr_ref[...] * g_ref[...]).astype(BF16)
    ya = _rms(ya_ref[...], gao_ref[...]).astype(BF16)
    x = x_ref[...] + _dot(yr, wout_ref[0:RWKV_WIDTH, :]) + _dot(ya, wout_ref[RWKV_WIDTH:, :])

    q = _dot(_rms(x, gxa_ref[...]).astype(BF16), wq_ref[...])
    heads = []
    for hd in range(XA_HEADS):
        cols = slice(hd * XA_HEAD_DIM, (hd + 1) * XA_HEAD_DIM)
        qh = _rms(q[:, cols], qn_ref[...]).astype(BF16)
        s = _dot_nt(qh, km_ref[:, cols]) * (XA_HEAD_DIM ** -0.5)
        m = jnp.max(s, axis=-1, keepdims=True)
        e = jnp.exp(s - m)
        pr = e / jnp.sum(e, axis=-1, keepdims=True)
        heads.append(_dot(pr.astype(BF16), vm_ref[:, cols]))
    o = jnp.concatenate(heads, axis=-1).astype(BF16)
    x = x + _dot(o, wo_ref[...])
    xo_ref[...] = x

    h = _rms(x, gff_ref[...])
    h_ref[...] = h
    logits = _dot(h.astype(BF16), wrt_ref[...]) + brt_ref[...]
    lane_i = lax.broadcasted_iota(jnp.int32, logits.shape, 1)
    lane = lane_i.astype(F32)
    big = float(ROUTE_LANES)
    gmask = lane < N_GROUPS
    gmax = jnp.max(jnp.where(gmask, logits, -jnp.inf), axis=-1, keepdims=True)
    ge = jnp.where(gmask, jnp.exp(logits - gmax), 0.0)
    gp = ge / jnp.sum(ge, axis=-1, keepdims=True)
    g_gate = jnp.max(gp, axis=-1, keepdims=True)
    g_top = jnp.min(jnp.where(gmask & (gp == g_gate), lane, big), axis=-1, keepdims=True)
    lo = N_GROUPS + g_top * EXPERTS_PER_GROUP
    rmask = (lane >= lo) & (lane < lo + EXPERTS_PER_GROUP)
    rmax = jnp.max(jnp.where(rmask, logits, -jnp.inf), axis=-1, keepdims=True)
    re = jnp.where(rmask, jnp.exp(logits - rmax), 0.0)
    p_in = re / jnp.sum(re, axis=-1, keepdims=True)
    p1 = jnp.max(jnp.where(rmask, p_in, -1.0), axis=-1, keepdims=True)
    j1 = jnp.min(jnp.where(rmask & (p_in == p1), lane, big), axis=-1, keepdims=True)
    rmask2 = rmask & (lane != j1)
    p2 = jnp.max(jnp.where(rmask2, p_in, -1.0), axis=-1, keepdims=True)
    j2 = jnp.min(jnp.where(rmask2 & (p_in == p2), lane, big), axis=-1, keepdims=True)
    denom = p1 + p2
    w1 = g_gate * (p1 / denom)
    w2 = g_gate * (p2 / denom)
    e1 = j1 - N_GROUPS
    e2 = j2 - N_GROUPS
    rt_ref[...] = jnp.where(lane_i == 0, e1, jnp.where(lane_i == 1, e2,
                            jnp.where(lane_i == 2, w1, jnp.where(lane_i == 3, w2, 0.0))))


def _mix_xa_router(x3, yr3, g3, ya3, gao, wout, gxa, wq, qn, km3, vm3, wo, gff, wrt, brt):
    B, S, D = x3.shape
    M = km3.shape[1]
    tm = ROW_TILE
    tile = lambda b, i: (b, i, 0)
    full = lambda b, i: (0, 0)
    memb = lambda b, i: (b, 0, 0)
    W = RWKV_WIDTH
    return pl.pallas_call(
        _mix_xa_router_kernel,
        grid=(B, S // tm),
        in_specs=[pl.BlockSpec((None, tm, D), tile), pl.BlockSpec((None, tm, W), tile),
                  pl.BlockSpec((None, tm, W), tile), pl.BlockSpec((None, tm, W), tile),
                  pl.BlockSpec((1, W), full), pl.BlockSpec((2 * W, D), full),
                  pl.BlockSpec((1, D), full), pl.BlockSpec((D, XA_WIDTH), full),
                  pl.BlockSpec((1, XA_HEAD_DIM), full),
                  pl.BlockSpec((None, M, XA_WIDTH), memb), pl.BlockSpec((None, M, XA_WIDTH), memb),
                  pl.BlockSpec((XA_WIDTH, D), full), pl.BlockSpec((1, D), full),
                  pl.BlockSpec((D, ROUTE_LANES), full), pl.BlockSpec((1, ROUTE_LANES), full)],
        out_specs=[pl.BlockSpec((None, tm, D), tile), pl.BlockSpec((None, tm, D), tile),
                   pl.BlockSpec((None, tm, ROUTE_LANES), tile)],
        out_shape=[jax.ShapeDtypeStruct((B, S, D), F32), jax.ShapeDtypeStruct((B, S, D), F32),
                   jax.ShapeDtypeStruct((B, S, ROUTE_LANES), F32)],
        compiler_params=_cparams(("parallel", "parallel"), 48),
        name="mix_xa_router",
    )(x3, yr3, g3, ya3, gao, wout, gxa, wq, qn, km3, vm3, wo, gff, wrt, brt)


def _row_copy(src_ref, src_row, dst_ref, dst_row, sem):
    return pltpu.make_async_copy(src_ref.at[pl.ds(src_row, 1)], dst_ref.at[pl.ds(dst_row, 1)], sem)


def _dispatch_kernel(dest_ref, h_ref, zeros_ref, xs_ref, sem):
    del zeros_ref
    tm = h_ref.shape[0]
    base = pl.program_id(0) * (tm * TOP_K)

    def issue(r, carry):
        for kk in range(TOP_K):
            _row_copy(h_ref, r, xs_ref, dest_ref[base + r * TOP_K + kk], sem).start()
        return carry

    lax.fori_loop(0, tm, issue, 0)

    def drain(r, carry):
        for kk in range(TOP_K):
            _row_copy(h_ref, r, xs_ref, dest_ref[base + r * TOP_K + kk], sem).wait()
        return carry

    lax.fori_loop(0, tm, drain, 0)


def _dispatch(dest, h2d, n_slots):
    T, D = h2d.shape
    tm = MOE_ROWS
    zeros = jnp.zeros((n_slots, D), F32)
    return pl.pallas_call(
        _dispatch_kernel,
        grid_spec=pltpu.PrefetchScalarGridSpec(
            num_scalar_prefetch=1,
            grid=(T // tm,),
            in_specs=[pl.BlockSpec((tm, D), lambda i, d: (i, 0)),
                      pl.BlockSpec(memory_space=pl.ANY)],
            out_specs=pl.BlockSpec(memory_space=pl.ANY),
            scratch_shapes=[pltpu.SemaphoreType.DMA(())],
        ),
        out_shape=jax.ShapeDtypeStruct((n_slots, D), F32),
        input_output_aliases={2: 0},
        compiler_params=_cparams(("arbitrary",), 32),
        name="moe_dispatch",
    )(dest, h2d, zeros)


def _expert_kernel(be_ref, nu_ref, xs_ref, wg_ref, wu_ref, wd_ref, y_ref):
    used = pl.program_id(0) < nu_ref[0]

    @pl.when(used)
    def _():
        xb = xs_ref[...].astype(BF16)
        gate = _dot(xb, wg_ref[...])
        up = _dot(xb, wu_ref[...])
        act = (gate * jax.nn.sigmoid(gate) * up).astype(BF16)
        y_ref[...] = _dot(act, wd_ref[...])

    @pl.when(jnp.logical_not(used))
    def _():
        y_ref[...] = jnp.zeros_like(y_ref)


def _expert_ffn(block_expert, n_used, xs, wg, wu, wd):
    L, D = xs.shape
    n_blocks = L // MOE_BLOCK
    blk = lambda i, be, nu: (jnp.minimum(i, nu[0] - 1), 0)
    wsel = lambda i, be, nu: (be[jnp.minimum(i, nu[0] - 1)], 0, 0)
    return pl.pallas_call(
        _expert_kernel,
        grid_spec=pltpu.PrefetchScalarGridSpec(
            num_scalar_prefetch=2,
            grid=(n_blocks,),
            in_specs=[pl.BlockSpec((MOE_BLOCK, D), blk),
                      pl.BlockSpec((None, D, EXPERT_FF), wsel),
                      pl.BlockSpec((None, D, EXPERT_FF), wsel),
                      pl.BlockSpec((None, EXPERT_FF, D), wsel)],
            out_specs=pl.BlockSpec((MOE_BLOCK, D), lambda i, be, nu: (i, 0)),
        ),
        out_shape=jax.ShapeDtypeStruct((L, D), F32),
        compiler_params=_cparams(("arbitrary",), 48),
        name="moe_experts",
    )(block_expert, n_used, xs, wg, wu, wd)


def _combine_kernel(dest_ref, x_ref, wt_ref, yb_ref, o_ref, buf_ref, sem):
    tm = x_ref.shape[0]
    base = pl.program_id(0) * (tm * TOP_K)

    def issue(r, carry):
        for kk in range(TOP_K):
            _row_copy(yb_ref, dest_ref[base + r * TOP_K + kk], buf_ref.at[kk], r, sem).start()
        return carry

    lax.fori_loop(0, tm, issue, 0)

    def drain(r, carry):
        for kk in range(TOP_K):
            _row_copy(yb_ref, dest_ref[base + r * TOP_K + kk], buf_ref.at[kk], r, sem).wait()
        return carry

    lax.fori_loop(0, tm, drain, 0)
    wt = wt_ref[...]
    o_ref[...] = x_ref[...] + (buf_ref[0] * wt[:, 2:3] + buf_ref[1] * wt[:, 3:4])


def _combine(dest, x2d, rt2d, yb):
    T, D = x2d.shape
    tm = MOE_ROWS
    return pl.pallas_call(
        _combine_kernel,
        grid_spec=pltpu.PrefetchScalarGridSpec(
            num_scalar_prefetch=1,
            grid=(T // tm,),
            in_specs=[pl.BlockSpec((tm, D), lambda i, d: (i, 0)),
                      pl.BlockSpec((tm, ROUTE_LANES), lambda i, d: (i, 0)),
                      pl.BlockSpec(memory_space=pl.ANY)],
            out_specs=pl.BlockSpec((tm, D), lambda i, d: (i, 0)),
            scratch_shapes=[pltpu.VMEM((TOP_K, tm, D), F32), pltpu.SemaphoreType.DMA(())],
        ),
        out_shape=jax.ShapeDtypeStruct((T, D), F32),
        compiler_params=_cparams(("arbitrary",), 48),
        name="moe_combine",
    )(dest, x2d, rt2d, yb)


def _moe_plan(rt2d):
    T = rt2d.shape[0]
    A = T * TOP_K
    e = rt2d[:, 0:TOP_K].astype(jnp.int32).reshape(A)
    onehot = (e[:, None] == jnp.arange(N_EXPERTS, dtype=jnp.int32)[None, :]).astype(jnp.int32)
    csum = jnp.cumsum(onehot, axis=0)
    rank = jnp.sum(csum * onehot, axis=1) - 1
    counts = csum[-1]
    padded = ((counts + MOE_BLOCK - 1) // MOE_BLOCK) * MOE_BLOCK
    pends = jnp.cumsum(padded)
    pstarts = pends - padded
    dest = (pstarts[e] + rank).astype(jnp.int32)
    n_blocks = (A + N_EXPERTS * (MOE_BLOCK - 1) + MOE_BLOCK - 1) // MOE_BLOCK
    block_expert = jnp.minimum(
        jnp.searchsorted(pends, jnp.arange(n_blocks) * MOE_BLOCK, side='right'),
        N_EXPERTS - 1).astype(jnp.int32)
    n_used = (pends[-1] // MOE_BLOCK).astype(jnp.int32).reshape(1)
    return dest, block_expert, n_used, n_blocks * MOE_BLOCK


def _to_scan_layout(t3):
    B, S, _ = t3.shape
    return t3.reshape(B, S, RWKV_HEADS, HEAD_DIM).transpose(1, 3, 0, 2).reshape(S, HEAD_DIM, B * RWKV_HEADS)


def _from_scan_layout(t, B):
    S = t.shape[0]
    return t.reshape(S, HEAD_DIM, B, RWKV_HEADS).transpose(2, 0, 3, 1).reshape(B, S, RWKV_WIDTH)


def _scan_param(p, B):
    return jnp.tile(p.reshape(RWKV_HEADS, HEAD_DIM).T, (1, B))


def _rel_bias_pairs(table):
    q_off = jnp.arange(CHUNK)[:, None]
    k_off = jnp.arange(BAND)[None, :] - PREV_CHUNKS * CHUNK
    dist = jnp.clip(q_off - k_off, -(CHUNK - 1), REL_MAX) + (CHUNK - 1)
    bias = table[:, dist].astype(F32)
    return bias.reshape(ATT_HEADS // 2, 2 * CHUNK, BAND)


def _pad_rows(w, rows, offset=0):
    out = jnp.zeros((rows, w.shape[1]), w.dtype)
    return out.at[offset:offset + w.shape[0]].set(w)


def kernel(x, mem, norm_mix, w_in, shift_mu, decay_w0, decay_w2, iclr_a0, iclr_a2, gate_g2, rw_k_k, rw_k_a, rw_r_k, lnx_w, lnx_b, vres_v0, vres_v1, vres_v2, att_q_norm, att_k_norm, att_rel_bias, att_out_norm, w_out, norm_xa, norm_mem, xa_w_q, xa_w_kv, xa_w_o, xa_q_norm, xa_k_norm, norm_ffn, moe_w_group, moe_b_group, moe_w_route, moe_b_route, moe_w_gate, moe_w_up, moe_w_down):
    B, S, D = x.shape
    T = B * S
    depth = w_in.shape[0]
    M = mem.shape[1]
    row = lambda p: p.reshape(1, -1)
    ones_bd = jnp.kron(jnp.eye(ATT_HEADS, dtype=F32), jnp.ones((HEAD_DIM, HEAD_DIM), F32)).astype(BF16)

    v_first = None
    for l in range(depth):
        u, qh, kh, vh = _inproj(x.reshape(T, D), row(norm_mix[l]), w_in[l].astype(BF16),
                                row(jnp.tile(att_q_norm[l], ATT_HEADS)),
                                row(jnp.tile(att_k_norm[l], ATT_HEADS)), ones_bd)
        w2p = _pad_rows(decay_w2[l], LANES, 0).astype(BF16)
        a2p = _pad_rows(iclr_a2[l], LANES, DECAY_LORA).astype(BF16)
        vres = None
        if l > 0:
            v1p = jnp.zeros((RWKV_WIDTH, LANES), F32).at[:, :MV_LORA].set(vres_v1[l - 1]).astype(BF16)
            v2p = _pad_rows(vres_v2[l - 1], LANES, 0).astype(BF16)
            vres = (v_first, row(vres_v0[l - 1]), v1p, v2p)
        r3, w3, k3, v3, a3, g3 = _rwkv_prep(u.reshape(B, S, RWKV_IN), row(shift_mu[l]), row(decay_w0[l]),
                                            w2p, row(iclr_a0[l]), a2p, gate_g2[l].astype(BF16), vres)
        if l == 0:
            v_first = v3
        y_scan = _wkv_scan(_to_scan_layout(r3), _to_scan_layout(w3), _to_scan_layout(k3),
                           _to_scan_layout(v3), _to_scan_layout(a3),
                           _scan_param(rw_k_k[l], B), _scan_param(rw_k_a[l], B), _scan_param(rw_r_k[l], B),
                           _scan_param(lnx_w[l], B), _scan_param(lnx_b[l], B))
        yr3 = _from_scan_layout(y_scan, B)
        pad = ((0, 0), (BAND_PAD, 0), (0, 0))
        ya3 = _band_attn(qh.reshape(B, S, ATT_WIDTH), jnp.pad(kh.reshape(B, S, ATT_WIDTH), pad),
                         jnp.pad(vh.reshape(B, S, ATT_WIDTH), pad), _rel_bias_pairs(att_rel_bias[l]))
        km, vm = _mem_kv(mem.reshape(B * M, D), row(norm_mem[l]), xa_w_kv[l].astype(BF16), row(xa_k_norm[l]))
        wrt = jnp.zeros((D, ROUTE_LANES), F32)
        wrt = wrt.at[:, :N_GROUPS].set(moe_w_group[l]).at[:, N_GROUPS:N_GROUPS + N_EXPERTS].set(moe_w_route[l])
        brt = jnp.zeros((ROUTE_LANES,), F32)
        brt = brt.at[:N_GROUPS].set(moe_b_group[l]).at[N_GROUPS:N_GROUPS + N_EXPERTS].set(moe_b_route[l])
        x3, h3, rt3 = _mix_xa_router(
            x, yr3, g3, ya3, row(att_out_norm[l]), w_out[l].astype(BF16), row(norm_xa[l]),
            xa_w_q[l].astype(BF16), row(xa_q_norm[l]), km.reshape(B, M, XA_WIDTH), vm.reshape(B, M, XA_WIDTH),
            xa_w_o[l].astype(BF16), row(norm_ffn[l]), wrt.astype(BF16), row(brt))
        rt2d = rt3.reshape(T, ROUTE_LANES)
        dest, block_expert, n_used, n_slots = _moe_plan(rt2d)
        xs = _dispatch(dest, h3.reshape(T, D), n_slots)
        yb = _expert_ffn(block_expert, n_used, xs, moe_w_gate[l].astype(BF16), moe_w_up[l].astype(BF16),
                         moe_w_down[l].astype(BF16))
        x = _combine(dest, x3.reshape(T, D), rt2d, yb).reshape(B, S, D)
    return x
```

```python
import functools

import jax
import jax.numpy as jnp
from jax import lax
from jax.experimental import pallas as pl
from jax.experimental.pallas import tpu as pltpu

F32 = jnp.float32
BF16 = jnp.bfloat16

CHUNK = 64
RWKV_HEADS = 8
HEAD_DIM = 64
RWKV_WIDTH = 512
ATT_HEADS = 8
ATT_WIDTH = 512
DECAY_LORA = 64
AAA_LORA = 64
MV_LORA = 32
GATE_LORA = 128
RWKV_IN = 3 * RWKV_WIDTH + DECAY_LORA + AAA_LORA + GATE_LORA
LORA_OFF = 3 * RWKV_WIDTH
GATE_OFF = LORA_OFF + DECAY_LORA + AAA_LORA
IN_WIDTH = RWKV_IN + 3 * ATT_WIDTH
PREV_CHUNKS = 8
BAND = (PREV_CHUNKS + 1) * CHUNK
BAND_PAD = PREV_CHUNKS * CHUNK
REL_MAX = 256
XA_HEADS = 4
XA_HEAD_DIM = 128
XA_WIDTH = 512
N_GROUPS = 4
EXPERTS_PER_GROUP = 8
N_EXPERTS = 32
TOP_K = 2
EXPERT_FF = 512
MOE_BLOCK = 128
NORM_EPS = 1e-6
LNX_EPS = 64e-5
NEG_INF = -1e30

LANES = 128
SUBLANES = 8
ROW_TILE = 256
SCAN_STEPS = 32
ATT_ROWS = 256
MOE_ROWS = 512
ROUTE_LANES = 128
DMA_UNROLL = 8


def _cparams(semantics, vmem_mib):
    return pltpu.CompilerParams(dimension_semantics=semantics,
                                vmem_limit_bytes=vmem_mib * 1024 * 1024)


def _dot(a, b):
    return jnp.dot(a, b, preferred_element_type=F32)


def _dot_nt(a, b):
    return lax.dot_general(a, b, (((1,), (1,)), ((), ())), preferred_element_type=F32)


def _rms(x, g):
    ms = jnp.mean(x * x, axis=-1, keepdims=True)
    return x * lax.rsqrt(ms + NORM_EPS) * g


def _group_sumsq(x, ones_bd):
    sq = x * x
    hi = sq.astype(BF16)
    lo = (sq - hi.astype(F32)).astype(BF16)
    return _dot(hi, ones_bd) + _dot(lo, ones_bd)


def _inproj_kernel(x_ref, g_ref, w_ref, gq_ref, gk_ref, bd_ref, u_ref, q_ref, k_ref, v_ref):
    h = _rms(x_ref[...], g_ref[...]).astype(BF16)
    for j in range(0, RWKV_IN, 256):
        u_ref[:, j:j + 256] = _dot(h, w_ref[:, j:j + 256])
    bd = bd_ref[...]
    q = _dot(h, w_ref[:, RWKV_IN:RWKV_IN + ATT_WIDTH])
    qn = q * lax.rsqrt(_group_sumsq(q, bd) * (1.0 / HEAD_DIM) + NORM_EPS) * gq_ref[...]
    q_ref[...] = (qn * (HEAD_DIM ** -0.5)).astype(BF16)
    k = _dot(h, w_ref[:, RWKV_IN + ATT_WIDTH:RWKV_IN + 2 * ATT_WIDTH])
    kn = k * lax.rsqrt(_group_sumsq(k, bd) * (1.0 / HEAD_DIM) + NORM_EPS) * gk_ref[...]
    k_ref[...] = kn.astype(BF16)
    v_ref[...] = _dot(h, w_ref[:, RWKV_IN + 2 * ATT_WIDTH:]).astype(BF16)


def _inproj(x2d, g, w, gq, gk, bd):
    T, D = x2d.shape
    tm = ROW_TILE
    full = lambda i: (0, 0)
    row = lambda i: (i, 0)
    return pl.pallas_call(
        _inproj_kernel,
        grid=(T // tm,),
        in_specs=[pl.BlockSpec((tm, D), row), pl.BlockSpec((1, D), full),
                  pl.BlockSpec((D, IN_WIDTH), full), pl.BlockSpec((1, ATT_WIDTH), full),
                  pl.BlockSpec((1, ATT_WIDTH), full), pl.BlockSpec((ATT_WIDTH, ATT_WIDTH), full)],
        out_specs=[pl.BlockSpec((tm, RWKV_IN), row), pl.BlockSpec((tm, ATT_WIDTH), row),
                   pl.BlockSpec((tm, ATT_WIDTH), row), pl.BlockSpec((tm, ATT_WIDTH), row)],
        out_shape=[jax.ShapeDtypeStruct((T, RWKV_IN), F32), jax.ShapeDtypeStruct((T, ATT_WIDTH), BF16),
                   jax.ShapeDtypeStruct((T, ATT_WIDTH), BF16), jax.ShapeDtypeStruct((T, ATT_WIDTH), BF16)],
        compiler_params=_cparams(("parallel",), 48),
        name="inproj",
    )(x2d, g, w, gq, gk, bd)


def _rwkv_prep_kernel(has_vres, *refs):
    if has_vres:
        (u_ref, up_ref, mu_ref, w0_ref, w2_ref, a0_ref, a2_ref, g2_ref,
         vf_ref, v0_ref, v1_ref, v2_ref,
         r_ref, w_ref, k_ref, v_ref, a_ref, g_ref) = refs
    else:
        (u_ref, up_ref, mu_ref, w0_ref, w2_ref, a0_ref, a2_ref, g2_ref,
         r_ref, w_ref, k_ref, v_ref, a_ref, g_ref) = refs
    i = pl.program_id(1)
    u = u_ref[...]
    ts = u.shape[0]
    prev_row = jnp.where(i > 0, up_ref[SUBLANES - 1:SUBLANES, :], 0.0)
    rolled = pltpu.roll(u, 1, 0)
    row_id = lax.broadcasted_iota(jnp.int32, u.shape, 0)
    shifted = jnp.where(row_id == 0, jnp.broadcast_to(prev_row, u.shape), rolled)
    u = u + mu_ref[...] * (shifted - u)

    r_ref[...] = u[:, 0:RWKV_WIDTH]
    k_ref[...] = u[:, RWKV_WIDTH:2 * RWKV_WIDTH]
    v = u[:, 2 * RWKV_WIDTH:3 * RWKV_WIDTH]
    lora = u[:, LORA_OFF:GATE_OFF]
    gl = u[:, GATE_OFF:RWKV_IN]

    w = w0_ref[...] + _dot(jnp.tanh(lora).astype(BF16), w2_ref[...])
    z = -w
    softplus = jnp.maximum(z, 0.0) + jnp.log1p(jnp.exp(-jnp.abs(z)))
    w = -softplus - 0.5
    w_ref[...] = jnp.exp(-jnp.exp(w))
    a = jax.nn.sigmoid(a0_ref[...] + _dot(lora.astype(BF16), a2_ref[...]))
    a_ref[...] = a
    g_ref[...] = _dot(jax.nn.sigmoid(gl).astype(BF16), g2_ref[...])
    if has_vres:
        mix = _dot(_dot(v.astype(BF16), v1_ref[...]).astype(BF16), v2_ref[...])
        v = v + (vf_ref[...] - v) * jax.nn.sigmoid(v0_ref[...] + mix)
    v_ref[...] = v


def _rwkv_prep(u3, mu, w0, w2p, a0, a2p, g2, vres):
    B, S, _ = u3.shape
    ts = ROW_TILE
    W = RWKV_WIDTH
    full2 = lambda b, i: (0, 0)
    tile = lambda b, i: (b, i, 0)
    prev = lambda b, i: (b, jnp.maximum(i * (ts // SUBLANES) - 1, 0), 0)
    in_specs = [pl.BlockSpec((None, ts, RWKV_IN), tile),
                pl.BlockSpec((None, SUBLANES, RWKV_IN), prev),
                pl.BlockSpec((1, RWKV_IN), full2), pl.BlockSpec((1, W), full2),
                pl.BlockSpec((LANES, W), full2), pl.BlockSpec((1, W), full2),
                pl.BlockSpec((LANES, W), full2), pl.BlockSpec((GATE_LORA, W), full2)]
    args = [u3, u3, mu, w0, w2p, a0, a2p, g2]
    if vres is not None:
        v_first, v0, v1p, v2p = vres
        in_specs += [pl.BlockSpec((None, ts, W), tile), pl.BlockSpec((1, W), full2),
                     pl.BlockSpec((W, LANES), full2), pl.BlockSpec((LANES, W), full2)]
        args += [v_first, v0, v1p, v2p]
    out_sds = jax.ShapeDtypeStruct((B, S, W), F32)
    return pl.pallas_call(
        functools.partial(_rwkv_prep_kernel, vres is not None),
        grid=(B, S // ts),
        in_specs=in_specs,
        out_specs=[pl.BlockSpec((None, ts, W), tile)] * 6,
        out_shape=[out_sds] * 6,
        compiler_params=_cparams(("parallel", "parallel"), 48),
        name="rwkv_prep",
    )(*args)


def _wkv_kernel(r_ref, w_ref, k_ref, v_ref, a_ref, kk_ref, ka_ref, rk_ref, lw_ref, lb_ref,
                y_ref, s_ref, vec_ref):
    N = HEAD_DIM

    @pl.when(pl.program_id(0) == 0)
    def _():
        s_ref[...] = jnp.zeros_like(s_ref)

    kkp = kk_ref[...]
    kap = ka_ref[...]
    rkp = rk_ref[...]
    lwp = lw_ref[...]
    lbp = lb_ref[...]

    def bcast(row):
        return jnp.broadcast_to(row, (N, LANES))

    def step(t, carry):
        r_t = r_ref[t]
        k_raw = k_ref[t]
        a_t = a_ref[t]
        v_t = v_ref[t]
        kk = k_raw * kkp
        ss = jnp.sum(kk * kk, axis=0, keepdims=True)
        kk = kk * lax.rsqrt(jnp.maximum(ss, 1e-24))
        k2 = k_raw * (1.0 + (a_t - 1.0) * kap)
        vec_ref[0] = -kk
        vec_ref[1] = kk * a_t
        vec_ref[2] = k2

        sa = jnp.zeros((N, LANES), F32)
        for j in range(N):
            sa = sa + s_ref[j] * bcast(vec_ref[0, j:j + 1, :])
        y = jnp.zeros((N, LANES), F32)
        for j in range(N):
            sn = (s_ref[j] * bcast(w_ref[t, j:j + 1, :]) + sa * bcast(vec_ref[1, j:j + 1, :])
                  + v_t * bcast(vec_ref[2, j:j + 1, :]))
            s_ref[j] = sn
            y = y + sn * bcast(r_ref[t, j:j + 1, :])

        mean = jnp.mean(y, axis=0, keepdims=True)
        yc = y - mean
        var = jnp.mean(yc * yc, axis=0, keepdims=True)
        yn = yc * lax.rsqrt(var + LNX_EPS) * lwp + lbp
        bonus = jnp.sum(r_t * k2 * rkp, axis=0, keepdims=True) * v_t
        y_ref[t] = yn + bonus
        return carry

    lax.fori_loop(0, r_ref.shape[0], step, 0)


def _wkv_scan(r, w, k, v, a, kkp, kap, rkp, lwp, lbp):
    S = r.shape[0]
    tt = SCAN_STEPS
    N = HEAD_DIM
    blk = pl.BlockSpec((tt, N, LANES), lambda i: (i, 0, 0))
    par = pl.BlockSpec((N, LANES), lambda i: (0, 0))
    return pl.pallas_call(
        _wkv_kernel,
        grid=(S // tt,),
        in_specs=[blk] * 5 + [par] * 5,
        out_specs=blk,
        out_shape=jax.ShapeDtypeStruct((S, N, LANES), F32),
        scratch_shapes=[pltpu.VMEM((N, N, LANES), F32), pltpu.VMEM((3, N, LANES), F32)],
        compiler_params=_cparams(("arbitrary",), 32),
        name="wkv_scan",
    )(r, w, k, v, a, kkp, kap, rkp, lwp, lbp)


def _band_attn_kernel(q_ref, k_ref, v_ref, bias_ref, o_ref):
    i = pl.program_id(1)
    n_chunks = q_ref.shape[0] // CHUNK
    lane = lax.broadcasted_iota(jnp.int32, (CHUNK, LANES), 1)
    even = lane < HEAD_DIM
    slot = lax.broadcasted_iota(jnp.int32, (2 * CHUNK, BAND), 1)

    def chunk_body(c, carry):
        cg = i * n_chunks + c
        start = pl.multiple_of(cg * CHUNK, CHUNK)
        row0 = pl.multiple_of(c * CHUNK, CHUNK)
        valid = slot >= (BAND_PAD - cg * CHUNK)
        for p in range(ATT_HEADS // 2):
            cols = slice(p * LANES, (p + 1) * LANES)
            q2 = q_ref[pl.ds(row0, CHUNK), cols]
            zero = jnp.zeros_like(q2)
            qs = jnp.concatenate([jnp.where(even, q2, zero), jnp.where(even, zero, q2)], axis=0)
            kb = k_ref[pl.ds(start, BAND), cols]
            vb = v_ref[pl.ds(start, BAND), cols]
            s = _dot_nt(qs, kb) + bias_ref[p]
            s = jnp.where(valid, s, NEG_INF)
            m = jnp.max(s, axis=-1, keepdims=True)
            e = jnp.exp(s - m)
            pr = e / jnp.sum(e, axis=-1, keepdims=True)
            o = _dot(pr.astype(BF16), vb)
            o_ref[pl.ds(row0, CHUNK), cols] = jnp.where(even, o[:CHUNK], o[CHUNK:])
        return carry

    lax.fori_loop(0, n_chunks, chunk_body, 0)


def _band_attn(q3, kp3, vp3, bias_pairs):
    B, S, W = q3.shape
    SP = kp3.shape[1]
    tq = ATT_ROWS
    return pl.pallas_call(
        _band_attn_kernel,
        grid=(B, S // tq),
        in_specs=[pl.BlockSpec((None, tq, W), lambda b, i: (b, i, 0)),
                  pl.BlockSpec((None, SP, W), lambda b, i: (b, 0, 0)),
                  pl.BlockSpec((None, SP, W), lambda b, i: (b, 0, 0)),
                  pl.BlockSpec((ATT_HEADS // 2, 2 * CHUNK, BAND), lambda b, i: (0, 0, 0))],
        out_specs=pl.BlockSpec((None, tq, W), lambda b, i: (b, i, 0)),
        out_shape=jax.ShapeDtypeStruct((B, S, W), F32),
        compiler_params=_cparams(("parallel", "arbitrary"), 48),
        name="band_attn",
    )(q3, kp3, vp3, bias_pairs)


def _mem_kv_kernel(m_ref, g_ref, w_ref, kn_ref, k_ref, v_ref):
    h = _rms(m_ref[...], g_ref[...]).astype(BF16)
    kv = _dot(h, w_ref[...])
    for hd in range(XA_HEADS):
        cols = slice(hd * XA_HEAD_DIM, (hd + 1) * XA_HEAD_DIM)
        k_ref[:, cols] = _rms(kv[:, cols], kn_ref[...]).astype(BF16)
    v_ref[...] = kv[:, XA_WIDTH:].astype(BF16)


def _mem_kv(mem2d, g, w_kv, k_norm):
    R, D = mem2d.shape
    tm = ROW_TILE
    full = lambda i: (0, 0)
    row = lambda i: (i, 0)
    return pl.pallas_call(
        _mem_kv_kernel,
        grid=(R // tm,),
        in_specs=[pl.BlockSpec((tm, D), row), pl.BlockSpec((1, D), full),
                  pl.BlockSpec((D, 2 * XA_WIDTH), full), pl.BlockSpec((1, XA_HEAD_DIM), full)],
        out_specs=[pl.BlockSpec((tm, XA_WIDTH), row)] * 2,
        out_shape=[jax.ShapeDtypeStruct((R, XA_WIDTH), BF16)] * 2,
        compiler_params=_cparams(("parallel",), 32),
        name="mem_kv",
    )(mem2d, g, w_kv, k_norm)


def _mix_xa_router_kernel(x_ref, yr_ref, g_ref, ya_ref, gao_ref, wout_ref, gxa_ref, wq_ref, qn_ref,
                          km_ref, vm_ref, wo_ref, gff_ref, wrt_ref, brt_ref,
                          tri_ref, xo_ref, h_ref, rt_ref, cnt_ref, run_ref):
    first = (pl.program_id(0) == 0) & (pl.program_id(1) == 0)

    @pl.when(first)
    def _():
        run_ref[...] = jnp.zeros_like(run_ref)

    yr = (yr_ref[...] * g_ref[...]).astype(BF16)
    ya = _rms(ya_ref[...], gao_ref[...]).astype(BF16)
    x = x_ref[...] + _dot(yr, wout_ref[0:RWKV_WIDTH, :]) + _dot(ya, wout_ref[RWKV_WIDTH:, :])

    q = _dot(_rms(x, gxa_ref[...]).astype(BF16), wq_ref[...])
    heads = []
    for hd in range(XA_HEADS):
        cols = slice(hd * XA_HEAD_DIM, (hd + 1) * XA_HEAD_DIM)
        qh = _rms(q[:, cols], qn_ref[...]).astype(BF16)
        s = _dot_nt(qh, km_ref[:, cols]) * (XA_HEAD_DIM ** -0.5)
        m = jnp.max(s, axis=-1, keepdims=True)
        e = jnp.exp(s - m)
        pr = e / jnp.sum(e, axis=-1, keepdims=True)
        heads.append(_dot(pr.astype(BF16), vm_ref[:, cols]))
    o = jnp.concatenate(heads, axis=-1).astype(BF16)
    x = x + _dot(o, wo_ref[...])
    xo_ref[...] = x

    h = _rms(x, gff_ref[...])
    tm = h.shape[0]
    for j in range(h.shape[1] // LANES):
        h_ref[pl.ds(j, tm, stride=SUBLANES), :] = h[:, j * LANES:(j + 1) * LANES]
    logits =_dot(h.astype(BF16), wrt_ref[...]) + brt_ref[...]
    lane_i = lax.broadcasted_iota(jnp.int32, logits.shape, 1)
    lane = lane_i.astype(F32)
    big = float(ROUTE_LANES)
    gmask = lane < N_GROUPS
    gmax = jnp.max(jnp.where(gmask, logits, -jnp.inf), axis=-1, keepdims=True)
    ge = jnp.where(gmask, jnp.exp(logits - gmax), 0.0)
    gp = ge / jnp.sum(ge, axis=-1, keepdims=True)
    g_gate = jnp.max(gp, axis=-1, keepdims=True)
    g_top = jnp.min(jnp.where(gmask & (gp == g_gate), lane, big), axis=-1, keepdims=True)
    lo = N_GROUPS + g_top * EXPERTS_PER_GROUP
    rmask = (lane >= lo) & (lane < lo + EXPERTS_PER_GROUP)
    rmax = jnp.max(jnp.where(rmask, logits, -jnp.inf), axis=-1, keepdims=True)
    re = jnp.where(rmask, jnp.exp(logits - rmax), 0.0)
    p_in = re / jnp.sum(re, axis=-1, keepdims=True)
    p1 = jnp.max(jnp.where(rmask, p_in, -1.0), axis=-1, keepdims=True)
    j1 = jnp.min(jnp.where(rmask & (p_in == p1), lane, big), axis=-1, keepdims=True)
    rmask2 = rmask & (lane != j1)
    p2 = jnp.max(jnp.where(rmask2, p_in, -1.0), axis=-1, keepdims=True)
    j2 = jnp.min(jnp.where(rmask2 & (p_in == p2), lane, big), axis=-1, keepdims=True)
    denom = p1 + p2
    w1 = g_gate * (p1 / denom)
    w2 = g_gate * (p2 / denom)
    e1 = j1 - N_GROUPS
    e2 = j2 - N_GROUPS
    hit1 = lane == j1
    hit2 = lane == j2
    hits = jnp.where(hit1 | hit2, 1.0, 0.0)
    before = _dot(tri_ref[...], hits.astype(BF16)) + run_ref[...]
    rank1 = jnp.sum(jnp.where(hit1, before, 0.0), axis=-1, keepdims=True)
    rank2 = jnp.sum(jnp.where(hit2, before, 0.0), axis=-1, keepdims=True)
    run = run_ref[...] + jnp.sum(hits, axis=0, keepdims=True)
    run_ref[...] = run
    cnt_ref[...] = jnp.broadcast_to(run, cnt_ref.shape)
    rt_ref[...] = jnp.where(lane_i == 0, e1, jnp.where(lane_i == 1, e2,
                            jnp.where(lane_i == 2, w1, jnp.where(lane_i == 3, w2,
                                      jnp.where(lane_i == 4, rank1, jnp.where(lane_i == 5, rank2, 0.0))))))


def _mix_xa_router(x3, yr3, g3, ya3, gao, wout, gxa, wq, qn, km3, vm3, wo, gff, wrt, brt):
    B, S, D = x3.shape
    M = km3.shape[1]
    tm = ROW_TILE
    tile = lambda b, i: (b, i, 0)
    full = lambda b, i: (0, 0)
    memb = lambda b, i: (b, 0, 0)
    W = RWKV_WIDTH
    n_i = S // tm
    tri = jnp.tril(jnp.ones((tm, tm), F32), -1).astype(BF16)
    return pl.pallas_call(
        _mix_xa_router_kernel,
        grid=(B, S // tm),
        in_specs=[pl.BlockSpec((None, tm, D), tile), pl.BlockSpec((None, tm, W), tile),
                  pl.BlockSpec((None, tm, W), tile), pl.BlockSpec((None, tm, W), tile),
                  pl.BlockSpec((1, W), full), pl.BlockSpec((2 * W, D), full),
                  pl.BlockSpec((1, D), full), pl.BlockSpec((D, XA_WIDTH), full),
                  pl.BlockSpec((1, XA_HEAD_DIM), full),
                  pl.BlockSpec((None, M, XA_WIDTH), memb), pl.BlockSpec((None, M, XA_WIDTH), memb),
                  pl.BlockSpec((XA_WIDTH, D), full), pl.BlockSpec((1, D), full),
                  pl.BlockSpec((D, ROUTE_LANES), full), pl.BlockSpec((1, ROUTE_LANES), full),
                  pl.BlockSpec((tm, tm), full)],
        out_specs=[pl.BlockSpec((None, tm, D), tile),
                   pl.BlockSpec((tm * SUBLANES, LANES), lambda b, i: (b * n_i + i, 0)),
                   pl.BlockSpec((None, tm, ROUTE_LANES), tile),
                   pl.BlockSpec((SUBLANES, ROUTE_LANES), full)],
        out_shape=[jax.ShapeDtypeStruct((B, S, D), F32),
                   jax.ShapeDtypeStruct((B * S * SUBLANES, LANES), F32),
                   jax.ShapeDtypeStruct((B, S, ROUTE_LANES), F32),
                   jax.ShapeDtypeStruct((SUBLANES, ROUTE_LANES), F32)],
        scratch_shapes=[pltpu.VMEM((1, ROUTE_LANES), F32)],
        compiler_params=_cparams(("arbitrary", "arbitrary"), 48),
        name="mix_xa_router",
    )(x3, yr3, g3, ya3, gao, wout, gxa, wq, qn, km3, vm3, wo, gff, wrt, brt, tri)


def _row_copy(src_ref, src_row, dst_ref, dst_row, sem):
    src = src_ref.at[pl.ds(pl.multiple_of(src_row * SUBLANES, SUBLANES), SUBLANES)]
    dst = dst_ref.at[pl.ds(pl.multiple_of(dst_row * SUBLANES, SUBLANES), SUBLANES)]
    return pltpu.make_async_copy(src, dst, sem)


def _load_row_tiles(ref, rows):
    return jnp.concatenate([ref[pl.ds(j, rows, stride=SUBLANES), :] for j in range(SUBLANES)], axis=-1)


def _dispatch_kernel(dest_ref, h_ref, zeros_ref, xs_ref, sem):
    del zeros_ref
    tm = h_ref.shape[0] // SUBLANES
    base = pl.program_id(0) * (tm * TOP_K)

    def issue(r, carry):
        for kk in range(TOP_K):
            _row_copy(h_ref, r, xs_ref, dest_ref[base + r * TOP_K + kk], sem).start()
        return carry

    lax.fori_loop(0, tm, issue, 0, unroll=DMA_UNROLL)

    def drain(r, carry):
        for kk in range(TOP_K):
            _row_copy(h_ref, r, xs_ref, dest_ref[base + r * TOP_K + kk], sem).wait()
        return carry

    lax.fori_loop(0, tm, drain, 0, unroll=DMA_UNROLL)


def _dispatch(dest, h_tiles, n_slots):
    tm = MOE_ROWS
    T = h_tiles.shape[0] // SUBLANES
    zeros = jnp.zeros((n_slots * SUBLANES, LANES), F32)
    return pl.pallas_call(
        _dispatch_kernel,
        grid_spec=pltpu.PrefetchScalarGridSpec(
            num_scalar_prefetch=1,
            grid=(T // tm,),
            in_specs=[pl.BlockSpec((tm * SUBLANES, LANES), lambda i, d: (i, 0)),
                      pl.BlockSpec(memory_space=pl.ANY)],
            out_specs=pl.BlockSpec(memory_space=pl.ANY),
            scratch_shapes=[pltpu.SemaphoreType.DMA(())],
        ),
        out_shape=jax.ShapeDtypeStruct((n_slots * SUBLANES, LANES), F32),
        input_output_aliases={2: 0},
        compiler_params=_cparams(("arbitrary",), 32),
        name="moe_dispatch",
    )(dest, h_tiles, zeros)


def _expert_kernel(be_ref, nu_ref, xs_ref, wg_ref, wu_ref, wd_ref, y_ref, wgb_ref, wub_ref, wdb_ref):
    i = pl.program_id(0)
    used = i < nu_ref[0]
    new_expert = (i == 0) | (be_ref[i] != be_ref[jnp.maximum(i - 1, 0)])

    @pl.when(used & new_expert)
    def _():
        wgb_ref[...] = wg_ref[...].astype(BF16)
        wub_ref[...] = wu_ref[...].astype(BF16)
        wdb_ref[...] = wd_ref[...].astype(BF16)

    @pl.when(used)
    def _():
        xb = _load_row_tiles(xs_ref, MOE_BLOCK).astype(BF16)
        gate = _dot(xb, wgb_ref[...])
        up = _dot(xb, wub_ref[...])
        act = (gate * jax.nn.sigmoid(gate) * up).astype(BF16)
        y = _dot(act, wdb_ref[...])
        for j in range(SUBLANES):
            y_ref[pl.ds(j, MOE_BLOCK, stride=SUBLANES), :] = y[:, j * LANES:(j + 1) * LANES]

    @pl.when(jnp.logical_not(used))
    def _():
        y_ref[...] = jnp.zeros_like(y_ref)


def _expert_ffn(block_expert, n_used, xs_tiles, wg, wu, wd):
    D, FF = wg.shape[1], wg.shape[2]
    n_blocks = xs_tiles.shape[0] // (MOE_BLOCK * SUBLANES)
    rows = MOE_BLOCK * SUBLANES
    blk = lambda i, be, nu: (jnp.minimum(i, nu[0] - 1), 0)
    wsel = lambda i, be, nu: (be[jnp.minimum(i, nu[0] - 1)], 0, 0)
    return pl.pallas_call(
        _expert_kernel,
        grid_spec=pltpu.PrefetchScalarGridSpec(
            num_scalar_prefetch=2,
            grid=(n_blocks,),
            in_specs=[pl.BlockSpec((rows, LANES), blk),
                      pl.BlockSpec((None, D, FF), wsel),
                      pl.BlockSpec((None, D, FF), wsel),
                      pl.BlockSpec((None, FF, D), wsel)],
            out_specs=pl.BlockSpec((rows, LANES), lambda i, be, nu: (i, 0)),
            scratch_shapes=[pltpu.VMEM((D, FF), BF16), pltpu.VMEM((D, FF), BF16), pltpu.VMEM((FF, D), BF16)],
        ),
        out_shape=jax.ShapeDtypeStruct(xs_tiles.shape, F32),
        compiler_params=_cparams(("arbitrary",), 48),
        name="moe_experts",
    )(block_expert, n_used, xs_tiles, wg, wu, wd)


def _combine_kernel(dest_ref, x_ref, wt_ref, yb_ref, o_ref, buf_ref, sem):
    tm = x_ref.shape[0]
    base = pl.program_id(0) * (tm * TOP_K)

    def issue(r, carry):
        for kk in range(TOP_K):
            _row_copy(yb_ref, dest_ref[base + r * TOP_K + kk], buf_ref.at[kk], r, sem).start()
        return carry

    lax.fori_loop(0, tm, issue, 0, unroll=DMA_UNROLL)

    def drain(r, carry):
        for kk in range(TOP_K):
            _row_copy(yb_ref, dest_ref[base + r * TOP_K + kk], buf_ref.at[kk], r, sem).wait()
        return carry

    lax.fori_loop(0, tm, drain, 0, unroll=DMA_UNROLL)
    wt = wt_ref[...]
    w1 = wt[:, 2:3]
    w2 = wt[:, 3:4]
    for j in range(SUBLANES):
        cols = slice(j * LANES, (j + 1) * LANES)
        y1 = buf_ref[0, pl.ds(j, tm, stride=SUBLANES), :]
        y2 = buf_ref[1, pl.ds(j, tm, stride=SUBLANES), :]
        o_ref[:, cols] = x_ref[:, cols] + (y1 * w1 + y2 * w2)


def _combine(dest, x2d, rt2d, yb_tiles):
    T, D = x2d.shape
    tm = MOE_ROWS
    return pl.pallas_call(
        _combine_kernel,
        grid_spec=pltpu.PrefetchScalarGridSpec(
            num_scalar_prefetch=1,
            grid=(T // tm,),
            in_specs=[pl.BlockSpec((tm, D), lambda i, d: (i, 0)),
                      pl.BlockSpec((tm, ROUTE_LANES), lambda i, d: (i, 0)),
                      pl.BlockSpec(memory_space=pl.ANY)],
            out_specs=pl.BlockSpec((tm, D), lambda i, d: (i, 0)),
            scratch_shapes=[pltpu.VMEM((TOP_K, tm * SUBLANES, LANES), F32), pltpu.SemaphoreType.DMA(())],
        ),
        out_shape=jax.ShapeDtypeStruct((T, D), F32),
        compiler_params=_cparams(("arbitrary",), 48),
        name="moe_combine",
    )(dest, x2d, rt2d, yb_tiles)


def _moe_plan(rt2d, counts_lanes):
    T = rt2d.shape[0]
    A = T * TOP_K
    e = rt2d[:, 0:TOP_K].astype(jnp.int32).reshape(A)
    rank = rt2d[:, 4:4 + TOP_K].astype(jnp.int32).reshape(A)
    counts = counts_lanes[0, N_GROUPS:N_GROUPS + N_EXPERTS].astype(jnp.int32)
    padded = ((counts + MOE_BLOCK - 1) // MOE_BLOCK) * MOE_BLOCK
    pends = jnp.cumsum(padded)
    pstarts = pends - padded
    onehot = e[:, None] == jnp.arange(N_EXPERTS, dtype=jnp.int32)[None, :]
    dest = jnp.sum(jnp.where(onehot, pstarts[None, :], 0), axis=1) + rank
    n_blocks = (A + N_EXPERTS * (MOE_BLOCK - 1) + MOE_BLOCK - 1) // MOE_BLOCK
    block_start = jnp.arange(n_blocks, dtype=jnp.int32) * MOE_BLOCK
    block_expert = jnp.minimum(jnp.sum(pends[None, :] <= block_start[:, None], axis=1), N_EXPERTS - 1)
    n_used = (pends[-1] // MOE_BLOCK).astype(jnp.int32).reshape(1)
    return dest.astype(jnp.int32), block_expert.astype(jnp.int32), n_used, n_blocks * MOE_BLOCK


def _to_scan_layout(t3):
    B, S, _ = t3.shape
    return t3.reshape(B, S, RWKV_HEADS, HEAD_DIM).transpose(1, 3, 0, 2).reshape(S, HEAD_DIM, B * RWKV_HEADS)


def _from_scan_layout(t, B):
    S = t.shape[0]
    return t.reshape(S, HEAD_DIM, B, RWKV_HEADS).transpose(2, 0, 3, 1).reshape(B, S, RWKV_WIDTH)


def _scan_param(p, B):
    return jnp.tile(p.reshape(RWKV_HEADS, HEAD_DIM).T, (1, B))


def _rel_bias_pairs(table):
    q_off = jnp.arange(CHUNK)[:, None]
    k_off = jnp.arange(BAND)[None, :] - PREV_CHUNKS * CHUNK
    dist = jnp.clip(q_off - k_off, -(CHUNK - 1), REL_MAX) + (CHUNK - 1)
    bias = table[:, dist].astype(F32)
    return bias.reshape(ATT_HEADS // 2, 2 * CHUNK, BAND)


def _pad_rows(w, rows, offset=0):
    out = jnp.zeros((rows, w.shape[1]), w.dtype)
    return out.at[offset:offset + w.shape[0]].set(w)


def kernel(x, mem, norm_mix, w_in, shift_mu, decay_w0, decay_w2, iclr_a0, iclr_a2, gate_g2, rw_k_k, rw_k_a, rw_r_k, lnx_w, lnx_b, vres_v0, vres_v1, vres_v2, att_q_norm, att_k_norm, att_rel_bias, att_out_norm, w_out, norm_xa, norm_mem, xa_w_q, xa_w_kv, xa_w_o, xa_q_norm, xa_k_norm, norm_ffn, moe_w_group, moe_b_group, moe_w_route, moe_b_route, moe_w_gate, moe_w_up, moe_w_down):
    B, S, D = x.shape
    T = B * S
    depth = w_in.shape[0]
    M = mem.shape[1]
    row = lambda p: p.reshape(1, -1)
    ones_bd = jnp.kron(jnp.eye(ATT_HEADS, dtype=F32), jnp.ones((HEAD_DIM, HEAD_DIM), F32)).astype(BF16)

    v_first = None
    for l in range(depth):
        u, qh, kh, vh = _inproj(x.reshape(T, D), row(norm_mix[l]), w_in[l].astype(BF16),
                                row(jnp.tile(att_q_norm[l], ATT_HEADS)),
                                row(jnp.tile(att_k_norm[l], ATT_HEADS)), ones_bd)
        w2p = _pad_rows(decay_w2[l], LANES, 0).astype(BF16)
        a2p = _pad_rows(iclr_a2[l], LANES, DECAY_LORA).astype(BF16)
        vres = None
        if l > 0:
            v1p = jnp.zeros((RWKV_WIDTH, LANES), F32).at[:, :MV_LORA].set(vres_v1[l - 1]).astype(BF16)
            v2p = _pad_rows(vres_v2[l - 1], LANES, 0).astype(BF16)
            vres = (v_first, row(vres_v0[l - 1]), v1p, v2p)
        r3, w3, k3, v3, a3, g3 = _rwkv_prep(u.reshape(B, S, RWKV_IN), row(shift_mu[l]), row(decay_w0[l]),
                                            w2p, row(iclr_a0[l]), a2p, gate_g2[l].astype(BF16), vres)
        if l == 0:
            v_first = v3
        y_scan = _wkv_scan(_to_scan_layout(r3), _to_scan_layout(w3), _to_scan_layout(k3),
                           _to_scan_layout(v3), _to_scan_layout(a3),
                           _scan_param(rw_k_k[l], B), _scan_param(rw_k_a[l], B), _scan_param(rw_r_k[l], B),
                           _scan_param(lnx_w[l], B), _scan_param(lnx_b[l], B))
        yr3 = _from_scan_layout(y_scan, B)
        pad = ((0, 0), (BAND_PAD, 0), (0, 0))
        ya3 = _band_attn(qh.reshape(B, S, ATT_WIDTH), jnp.pad(kh.reshape(B, S, ATT_WIDTH), pad),
                         jnp.pad(vh.reshape(B, S, ATT_WIDTH), pad), _rel_bias_pairs(att_rel_bias[l]))
        km, vm = _mem_kv(mem.reshape(B * M, D), row(norm_mem[l]), xa_w_kv[l].astype(BF16), row(xa_k_norm[l]))
        wrt = jnp.zeros((D, ROUTE_LANES), F32)
        wrt = wrt.at[:, :N_GROUPS].set(moe_w_group[l]).at[:, N_GROUPS:N_GROUPS + N_EXPERTS].set(moe_w_route[l])
        brt = jnp.zeros((ROUTE_LANES,), F32)
        brt = brt.at[:N_GROUPS].set(moe_b_group[l]).at[N_GROUPS:N_GROUPS + N_EXPERTS].set(moe_b_route[l])
        x3, h_tiles, rt3, counts = _mix_xa_router(
            x, yr3, g3, ya3, row(att_out_norm[l]), w_out[l].astype(BF16), row(norm_xa[l]),
            xa_w_q[l].astype(BF16), row(xa_q_norm[l]), km.reshape(B, M, XA_WIDTH), vm.reshape(B, M, XA_WIDTH),
            xa_w_o[l].astype(BF16), row(norm_ffn[l]), wrt.astype(BF16), row(brt))
        rt2d = rt3.reshape(T, ROUTE_LANES)
        dest, block_expert, n_used, n_slots = _moe_plan(rt2d, counts)
        xs = _dispatch(dest, h_tiles, n_slots)
        yb = _expert_ffn(block_expert, n_used, xs, moe_w_gate[l], moe_w_up[l], moe_w_down[l])
        x = _combine(dest, x3.reshape(T, D), rt2d, yb).reshape(B, S, D)
    return x
```

```python
import functools

import jax
import jax.numpy as jnp
from jax import lax
from jax.experimental import pallas as pl
from jax.experimental.pallas import tpu as pltpu

F32 = jnp.float32
BF16 = jnp.bfloat16

CHUNK = 64
RWKV_HEADS = 8
HEAD_DIM = 64
RWKV_WIDTH = 512
ATT_HEADS = 8
ATT_WIDTH = 512
DECAY_LORA = 64
AAA_LORA = 64
MV_LORA = 32
GATE_LORA = 128
RWKV_IN = 3 * RWKV_WIDTH + DECAY_LORA + AAA_LORA + GATE_LORA
LORA_OFF = 3 * RWKV_WIDTH
GATE_OFF = LORA_OFF + DECAY_LORA + AAA_LORA
IN_WIDTH = RWKV_IN + 3 * ATT_WIDTH
PREV_CHUNKS = 8
BAND = (PREV_CHUNKS + 1) * CHUNK
BAND_PAD = PREV_CHUNKS * CHUNK
REL_MAX = 256
XA_HEADS = 4
XA_HEAD_DIM = 128
XA_WIDTH = 512
N_GROUPS = 4
EXPERTS_PER_GROUP = 8
N_EXPERTS = 32
TOP_K = 2
EXPERT_FF = 512
MOE_BLOCK = 256
NORM_EPS = 1e-6
LNX_EPS = 64e-5
NEG_INF = -1e30

LANES = 128
SUBLANES = 8
ROW_TILE = 256
MIX_ROWS = 512
LOG2E = 1.4426950408889634
SCAN_STEPS = 32
ATT_ROWS = 256
MOE_ROWS = 512
ROUTE_LANES = 128
DMA_UNROLL = 8


def _cparams(semantics, vmem_mib):
    return pltpu.CompilerParams(dimension_semantics=semantics,
                                vmem_limit_bytes=vmem_mib * 1024 * 1024)


def _dot(a, b):
    return jnp.dot(a, b, preferred_element_type=F32)


def _dot_nt(a, b):
    return lax.dot_general(a, b, (((1,), (1,)), ((), ())), preferred_element_type=F32)


def _rms(x, g):
    ms = jnp.mean(x * x, axis=-1, keepdims=True)
    return x * lax.rsqrt(ms + NORM_EPS) * g


def _group_sumsq(x, ones_bd):
    sq = x * x
    hi = sq.astype(BF16)
    lo = (sq - hi.astype(F32)).astype(BF16)
    return _dot(hi, ones_bd) + _dot(lo, ones_bd)


def _inproj_kernel(x_ref, g_ref, w_ref, gq_ref, gk_ref, bd_ref, u_ref, q_ref, k_ref, v_ref):
    h = _rms(x_ref[...], g_ref[...]).astype(BF16)
    for j in range(0, RWKV_IN, 256):
        u_ref[:, j:j + 256] = _dot(h, w_ref[:, j:j + 256])
    bd = bd_ref[...]
    q = _dot(h, w_ref[:, RWKV_IN:RWKV_IN + ATT_WIDTH])
    qn = q * lax.rsqrt(_group_sumsq(q, bd) * (1.0 / HEAD_DIM) + NORM_EPS) * gq_ref[...]
    q_ref[...] = (qn * (HEAD_DIM ** -0.5 * LOG2E)).astype(BF16)
    k = _dot(h, w_ref[:, RWKV_IN + ATT_WIDTH:RWKV_IN + 2 * ATT_WIDTH])
    kn = k * lax.rsqrt(_group_sumsq(k, bd) * (1.0 / HEAD_DIM) + NORM_EPS) * gk_ref[...]
    k_ref[...] = kn.astype(BF16)
    v_ref[...] = _dot(h, w_ref[:, RWKV_IN + 2 * ATT_WIDTH:]).astype(BF16)


def _inproj(x2d, g, w, gq, gk, bd):
    T, D = x2d.shape
    tm = ROW_TILE
    full = lambda i: (0, 0)
    row = lambda i: (i, 0)
    return pl.pallas_call(
        _inproj_kernel,
        grid=(T // tm,),
        in_specs=[pl.BlockSpec((tm, D), row), pl.BlockSpec((1, D), full),
                  pl.BlockSpec((D, IN_WIDTH), full), pl.BlockSpec((1, ATT_WIDTH), full),
                  pl.BlockSpec((1, ATT_WIDTH), full), pl.BlockSpec((ATT_WIDTH, ATT_WIDTH), full)],
        out_specs=[pl.BlockSpec((tm, RWKV_IN), row), pl.BlockSpec((tm, ATT_WIDTH), row),
                   pl.BlockSpec((tm, ATT_WIDTH), row), pl.BlockSpec((tm, ATT_WIDTH), row)],
        out_shape=[jax.ShapeDtypeStruct((T, RWKV_IN), F32), jax.ShapeDtypeStruct((T, ATT_WIDTH), BF16),
                   jax.ShapeDtypeStruct((T, ATT_WIDTH), BF16), jax.ShapeDtypeStruct((T, ATT_WIDTH), BF16)],
        compiler_params=_cparams(("parallel",), 48),
        name="inproj",
    )(x2d, g, w, gq, gk, bd)


def _rwkv_prep_kernel(has_vres, *refs):
    if has_vres:
        (u_ref, up_ref, mu_ref, w0_ref, w2_ref, a0_ref, a2_ref, g2_ref,
         vf_ref, v0_ref, v1_ref, v2_ref,
         r_ref, w_ref, k_ref, v_ref, a_ref, g_ref) = refs
    else:
        (u_ref, up_ref, mu_ref, w0_ref, w2_ref, a0_ref, a2_ref, g2_ref,
         r_ref, w_ref, k_ref, v_ref, a_ref, g_ref) = refs
    i = pl.program_id(1)
    u = u_ref[...]
    ts = u.shape[0]
    prev_row = jnp.where(i > 0, up_ref[SUBLANES - 1:SUBLANES, :], 0.0)
    rolled = pltpu.roll(u, 1, 0)
    row_id = lax.broadcasted_iota(jnp.int32, u.shape, 0)
    shifted = jnp.where(row_id == 0, jnp.broadcast_to(prev_row, u.shape), rolled)
    u = u + mu_ref[...] * (shifted - u)

    r_ref[...] = u[:, 0:RWKV_WIDTH]
    k_ref[...] = u[:, RWKV_WIDTH:2 * RWKV_WIDTH]
    v = u[:, 2 * RWKV_WIDTH:3 * RWKV_WIDTH]
    lora = u[:, LORA_OFF:GATE_OFF]
    gl = u[:, GATE_OFF:RWKV_IN]

    w = w0_ref[...] + _dot(jnp.tanh(lora).astype(BF16), w2_ref[...])
    z = -w
    softplus = jnp.maximum(z, 0.0) + jnp.log1p(jnp.exp(-jnp.abs(z)))
    w = -softplus - 0.5
    w_ref[...] = jnp.exp(-jnp.exp(w))
    a = jax.nn.sigmoid(a0_ref[...] + _dot(lora.astype(BF16), a2_ref[...]))
    a_ref[...] = a
    g_ref[...] = _dot(jax.nn.sigmoid(gl).astype(BF16), g2_ref[...])
    if has_vres:
        mix = _dot(_dot(v.astype(BF16), v1_ref[...]).astype(BF16), v2_ref[...])
        v = v + (vf_ref[...] - v) * jax.nn.sigmoid(v0_ref[...] + mix)
    v_ref[...] = v


def _rwkv_prep(u3, mu, w0, w2p, a0, a2p, g2, vres):
    B, S, _ = u3.shape
    ts = ROW_TILE
    W = RWKV_WIDTH
    full2 = lambda b, i: (0, 0)
    tile = lambda b, i: (b, i, 0)
    prev = lambda b, i: (b, jnp.maximum(i * (ts // SUBLANES) - 1, 0), 0)
    in_specs = [pl.BlockSpec((None, ts, RWKV_IN), tile),
                pl.BlockSpec((None, SUBLANES, RWKV_IN), prev),
                pl.BlockSpec((1, RWKV_IN), full2), pl.BlockSpec((1, W), full2),
                pl.BlockSpec((LANES, W), full2), pl.BlockSpec((1, W), full2),
                pl.BlockSpec((LANES, W), full2), pl.BlockSpec((GATE_LORA, W), full2)]
    args = [u3, u3, mu, w0, w2p, a0, a2p, g2]
    if vres is not None:
        v_first, v0, v1p, v2p = vres
        in_specs += [pl.BlockSpec((None, ts, W), tile), pl.BlockSpec((1, W), full2),
                     pl.BlockSpec((W, LANES), full2), pl.BlockSpec((LANES, W), full2)]
        args += [v_first, v0, v1p, v2p]
    out_sds = jax.ShapeDtypeStruct((B, S, W), F32)
    return pl.pallas_call(
        functools.partial(_rwkv_prep_kernel, vres is not None),
        grid=(B, S // ts),
        in_specs=in_specs,
        out_specs=[pl.BlockSpec((None, ts, W), tile)] * 6,
        out_shape=[out_sds] * 6,
        compiler_params=_cparams(("parallel", "parallel"), 48),
        name="rwkv_prep",
    )(*args)


def _wkv_kernel(r_ref, w_ref, k_ref, v_ref, a_ref, kk_ref, ka_ref, rk_ref, lw_ref, lb_ref,
                y_ref, s_ref, vec_ref):
    N = HEAD_DIM

    @pl.when(pl.program_id(0) == 0)
    def _():
        s_ref[...] = jnp.zeros_like(s_ref)

    kkp = kk_ref[...]
    kap = ka_ref[...]
    rkp = rk_ref[...]
    lwp = lw_ref[...]
    lbp = lb_ref[...]

    def bcast(row):
        return jnp.broadcast_to(row, (N, LANES))

    def step(t, carry):
        r_t = r_ref[t]
        k_raw = k_ref[t]
        a_t = a_ref[t]
        v_t = v_ref[t]
        kk = k_raw * kkp
        ss = jnp.sum(kk * kk, axis=0, keepdims=True)
        kk = kk * lax.rsqrt(jnp.maximum(ss, 1e-24))
        k2 = k_raw * (1.0 + (a_t - 1.0) * kap)
        vec_ref[0] = -kk
        vec_ref[1] = kk * a_t
        vec_ref[2] = k2

        sa = jnp.zeros((N, LANES), F32)
        for j in range(N):
            sa = sa + s_ref[j] * bcast(vec_ref[0, j:j + 1, :])
        y = jnp.zeros((N, LANES), F32)
        for j in range(N):
            sn = (s_ref[j] * bcast(w_ref[t, j:j + 1, :]) + sa * bcast(vec_ref[1, j:j + 1, :])
                  + v_t * bcast(vec_ref[2, j:j + 1, :]))
            s_ref[j] = sn
            y = y + sn * bcast(r_ref[t, j:j + 1, :])

        mean = jnp.mean(y, axis=0, keepdims=True)
        yc = y - mean
        var = jnp.mean(yc * yc, axis=0, keepdims=True)
        yn = yc * lax.rsqrt(var + LNX_EPS) * lwp + lbp
        bonus = jnp.sum(r_t * k2 * rkp, axis=0, keepdims=True) * v_t
        y_ref[t] = yn + bonus
        return carry

    lax.fori_loop(0, r_ref.shape[0], step, 0)


def _wkv_scan(r, w, k, v, a, kkp, kap, rkp, lwp, lbp):
    S = r.shape[0]
    tt = SCAN_STEPS
    N = HEAD_DIM
    blk = pl.BlockSpec((tt, N, LANES), lambda i: (i, 0, 0))
    par = pl.BlockSpec((N, LANES), lambda i: (0, 0))
    return pl.pallas_call(
        _wkv_kernel,
        grid=(S // tt,),
        in_specs=[blk] * 5 + [par] * 5,
        out_specs=blk,
        out_shape=jax.ShapeDtypeStruct((S, N, LANES), F32),
        scratch_shapes=[pltpu.VMEM((N, N, LANES), F32), pltpu.VMEM((3, N, LANES), F32)],
        compiler_params=_cparams(("arbitrary",), 32),
        name="wkv_scan",
    )(r, w, k, v, a, kkp, kap, rkp, lwp, lbp)


def _band_attn_kernel(q_ref, k_ref, v_ref, bias_ref, o_ref, s_ref, e_ref):
    i = pl.program_id(1)
    n_chunks = q_ref.shape[0] // CHUNK
    n_pairs = ATT_HEADS // 2
    lane = lax.broadcasted_iota(jnp.int32, (CHUNK, LANES), 1)
    even = lane < HEAD_DIM
    slot = lax.broadcasted_iota(jnp.int32, (1, 1, BAND), 2)

    def chunk_body(masked, c, carry):
        cg = i * n_chunks + c
        start = pl.multiple_of(cg * CHUNK, CHUNK)
        row0 = pl.multiple_of(c * CHUNK, CHUNK)
        for p in range(n_pairs):
            cols = slice(p * LANES, (p + 1) * LANES)
            q2 = q_ref[pl.ds(row0, CHUNK), cols]
            zero = jnp.zeros_like(q2)
            qs = jnp.concatenate([jnp.where(even, q2, zero), jnp.where(even, zero, q2)], axis=0)
            s_ref[p] = _dot_nt(qs, k_ref[pl.ds(start, BAND), cols]) + bias_ref[p]
        s = s_ref[...]
        if masked:
            s = jnp.where(slot >= (BAND_PAD - cg * CHUNK), s, NEG_INF)
        e = jnp.exp2(s - jnp.max(s, axis=-1, keepdims=True))
        inv = 1.0 / jnp.sum(e, axis=-1, keepdims=True)
        e_ref[...] = e.astype(BF16)
        for p in range(n_pairs):
            cols = slice(p * LANES, (p + 1) * LANES)
            o = _dot(e_ref[p], v_ref[pl.ds(start, BAND), cols]) * inv[p]
            o_ref[pl.ds(row0, CHUNK), cols] = jnp.where(even, o[:CHUNK], o[CHUNK:])
        return carry

    @pl.when(i * n_chunks < PREV_CHUNKS)
    def _():
        lax.fori_loop(0, n_chunks, functools.partial(chunk_body, True), 0)

    @pl.when(i * n_chunks >= PREV_CHUNKS)
    def _():
        lax.fori_loop(0, n_chunks, functools.partial(chunk_body, False), 0)


def _band_attn(q3, kp3, vp3, bias_pairs):
    B, S, W = q3.shape
    SP = kp3.shape[1]
    tq = ATT_ROWS
    assert PREV_CHUNKS % (tq // CHUNK) == 0
    n_pairs = ATT_HEADS // 2
    return pl.pallas_call(
        _band_attn_kernel,
        grid=(B, S // tq),
        in_specs=[pl.BlockSpec((None, tq, W), lambda b, i: (b, i, 0)),
                  pl.BlockSpec((None, SP, W), lambda b, i: (b, 0, 0)),
                  pl.BlockSpec((None, SP, W), lambda b, i: (b, 0, 0)),
                  pl.BlockSpec((ATT_HEADS // 2, 2 * CHUNK, BAND), lambda b, i: (0, 0, 0))],
        out_specs=pl.BlockSpec((None, tq, W), lambda b, i: (b, i, 0)),
        out_shape=jax.ShapeDtypeStruct((B, S, W), F32),
        scratch_shapes=[pltpu.VMEM((n_pairs, 2 * CHUNK, BAND), F32),
                        pltpu.VMEM((n_pairs, 2 * CHUNK, BAND), BF16)],
        compiler_params=_cparams(("parallel", "arbitrary"), 48),
        name="band_attn",
    )(q3, kp3, vp3, bias_pairs)


def _mem_kv_kernel(m_ref, g_ref, w_ref, kn_ref, k_ref, v_ref):
    h = _rms(m_ref[...], g_ref[...]).astype(BF16)
    kv = _dot(h, w_ref[...])
    for hd in range(XA_HEADS):
        cols = slice(hd * XA_HEAD_DIM, (hd + 1) * XA_HEAD_DIM)
        k_ref[:, cols] = _rms(kv[:, cols], kn_ref[...]).astype(BF16)
    v_ref[...] = kv[:, XA_WIDTH:].astype(BF16)


def _mem_kv(mem2d, g, w_kv, k_norm):
    R, D = mem2d.shape
    tm = ROW_TILE
    full = lambda i: (0, 0)
    row = lambda i: (i, 0)
    return pl.pallas_call(
        _mem_kv_kernel,
        grid=(R // tm,),
        in_specs=[pl.BlockSpec((tm, D), row), pl.BlockSpec((1, D), full),
                  pl.BlockSpec((D, 2 * XA_WIDTH), full), pl.BlockSpec((1, XA_HEAD_DIM), full)],
        out_specs=[pl.BlockSpec((tm, XA_WIDTH), row)] * 2,
        out_shape=[jax.ShapeDtypeStruct((R, XA_WIDTH), BF16)] * 2,
        compiler_params=_cparams(("parallel",), 32),
        name="mem_kv",
    )(mem2d, g, w_kv, k_norm)


def _mix_xa_router_kernel(x_ref, yr_ref, g_ref, ya_ref, gao_ref, wout_ref, gxa_ref, wq_ref, qn_ref,
                          km_ref, vm_ref, wo_ref, gff_ref, wrt_ref, brt_ref,
                          tri_ref, xo_ref, h_ref, rt_ref, cnt_ref, run_ref):
    first = (pl.program_id(0) == 0) & (pl.program_id(1) == 0)

    @pl.when(first)
    def _():
        run_ref[...] = jnp.zeros_like(run_ref)

    yr = (yr_ref[...] * g_ref[...]).astype(BF16)
    ya = _rms(ya_ref[...], gao_ref[...]).astype(BF16)
    x = x_ref[...] + _dot(yr, wout_ref[0:RWKV_WIDTH, :]) + _dot(ya, wout_ref[RWKV_WIDTH:, :])

    q = _dot(_rms(x, gxa_ref[...]).astype(BF16), wq_ref[...])
    heads = []
    for hd in range(XA_HEADS):
        cols = slice(hd * XA_HEAD_DIM, (hd + 1) * XA_HEAD_DIM)
        qh = _rms(q[:, cols], qn_ref[...]).astype(BF16)
        s = _dot_nt(qh, km_ref[:, cols]) * (XA_HEAD_DIM ** -0.5)
        m = jnp.max(s, axis=-1, keepdims=True)
        e = jnp.exp(s - m)
        pr = e / jnp.sum(e, axis=-1, keepdims=True)
        heads.append(_dot(pr.astype(BF16), vm_ref[:, cols]))
    o = jnp.concatenate(heads, axis=-1).astype(BF16)
    x = x + _dot(o, wo_ref[...])
    xo_ref[...] = x

    h = _rms(x, gff_ref[...])
    tm = h.shape[0]
    for j in range(h.shape[1] // LANES):
        h_ref[pl.ds(j, tm, stride=SUBLANES), :] = h[:, j * LANES:(j + 1) * LANES]
    logits =_dot(h.astype(BF16), wrt_ref[...]) + brt_ref[...]
    lane_i = lax.broadcasted_iota(jnp.int32, logits.shape, 1)
    lane = lane_i.astype(F32)
    big = float(ROUTE_LANES)
    gmask = lane < N_GROUPS
    gmax = jnp.max(jnp.where(gmask, logits, -jnp.inf), axis=-1, keepdims=True)
    ge = jnp.where(gmask, jnp.exp(logits - gmax), 0.0)
    gp = ge / jnp.sum(ge, axis=-1, keepdims=True)
    g_gate = jnp.max(gp, axis=-1, keepdims=True)
    g_top = jnp.min(jnp.where(gmask & (gp == g_gate), lane, big), axis=-1, keepdims=True)
    lo = N_GROUPS + g_top * EXPERTS_PER_GROUP
    rmask = (lane >= lo) & (lane < lo + EXPERTS_PER_GROUP)
    rmax = jnp.max(jnp.where(rmask, logits, -jnp.inf), axis=-1, keepdims=True)
    re = jnp.where(rmask, jnp.exp(logits - rmax), 0.0)
    p_in = re / jnp.sum(re, axis=-1, keepdims=True)
    p1 = jnp.max(jnp.where(rmask, p_in, -1.0), axis=-1, keepdims=True)
    j1 = jnp.min(jnp.where(rmask & (p_in == p1), lane, big), axis=-1, keepdims=True)
    rmask2 = rmask & (lane != j1)
    p2 = jnp.max(jnp.where(rmask2, p_in, -1.0), axis=-1, keepdims=True)
    j2 = jnp.min(jnp.where(rmask2 & (p_in == p2), lane, big), axis=-1, keepdims=True)
    denom = p1 + p2
    w1 = g_gate * (p1 / denom)
    w2 = g_gate * (p2 / denom)
    e1 = j1 - N_GROUPS
    e2 = j2 - N_GROUPS
    hit1 = lane == j1
    hit2 = lane == j2
    hits = jnp.where(hit1 | hit2, 1.0, 0.0)
    before = _dot(tri_ref[...], hits.astype(BF16)) + run_ref[...]
    rank1 = jnp.sum(jnp.where(hit1, before, 0.0), axis=-1, keepdims=True)
    rank2 = jnp.sum(jnp.where(hit2, before, 0.0), axis=-1, keepdims=True)
    run = run_ref[...] + jnp.sum(hits, axis=0, keepdims=True)
    run_ref[...] = run
    cnt_ref[...] = jnp.broadcast_to(run, cnt_ref.shape)
    rt_ref[...] = jnp.where(lane_i == 0, e1, jnp.where(lane_i == 1, e2,
                            jnp.where(lane_i == 2, w1, jnp.where(lane_i == 3, w2,
                                      jnp.where(lane_i == 4, rank1, jnp.where(lane_i == 5, rank2, 0.0))))))


def _mix_xa_router(x3, yr3, g3, ya3, gao, wout, gxa, wq, qn, km3, vm3, wo, gff, wrt, brt):
    B, S, D = x3.shape
    M = km3.shape[1]
    tm = MIX_ROWS
    tile = lambda b, i: (b, i, 0)
    full = lambda b, i: (0, 0)
    memb = lambda b, i: (b, 0, 0)
    W = RWKV_WIDTH
    n_i = S // tm
    tri = jnp.tril(jnp.ones((tm, tm), F32), -1).astype(BF16)
    return pl.pallas_call(
        _mix_xa_router_kernel,
        grid=(B, S // tm),
        in_specs=[pl.BlockSpec((None, tm, D), tile), pl.BlockSpec((None, tm, W), tile),
                  pl.BlockSpec((None, tm, W), tile), pl.BlockSpec((None, tm, W), tile),
                  pl.BlockSpec((1, W), full), pl.BlockSpec((2 * W, D), full),
                  pl.BlockSpec((1, D), full), pl.BlockSpec((D, XA_WIDTH), full),
                  pl.BlockSpec((1, XA_HEAD_DIM), full),
                  pl.BlockSpec((None, M, XA_WIDTH), memb), pl.BlockSpec((None, M, XA_WIDTH), memb),
                  pl.BlockSpec((XA_WIDTH, D), full), pl.BlockSpec((1, D), full),
                  pl.BlockSpec((D, ROUTE_LANES), full), pl.BlockSpec((1, ROUTE_LANES), full),
                  pl.BlockSpec((tm, tm), full)],
        out_specs=[pl.BlockSpec((None, tm, D), tile),
                   pl.BlockSpec((tm * SUBLANES, LANES), lambda b, i: (b * n_i + i, 0)),
                   pl.BlockSpec((None, tm, ROUTE_LANES), tile),
                   pl.BlockSpec((SUBLANES, ROUTE_LANES), full)],
        out_shape=[jax.ShapeDtypeStruct((B, S, D), F32),
                   jax.ShapeDtypeStruct((B * S * SUBLANES, LANES), F32),
                   jax.ShapeDtypeStruct((B, S, ROUTE_LANES), F32),
                   jax.ShapeDtypeStruct((SUBLANES, ROUTE_LANES), F32)],
        scratch_shapes=[pltpu.VMEM((1, ROUTE_LANES), F32)],
        compiler_params=_cparams(("arbitrary", "arbitrary"), 48),
        name="mix_xa_router",
    )(x3, yr3, g3, ya3, gao, wout, gxa, wq, qn, km3, vm3, wo, gff, wrt, brt, tri)


def _row_copy(src_ref, src_row, dst_ref, dst_row, sem):
    src = src_ref.at[pl.ds(pl.multiple_of(src_row * SUBLANES, SUBLANES), SUBLANES)]
    dst = dst_ref.at[pl.ds(pl.multiple_of(dst_row * SUBLANES, SUBLANES), SUBLANES)]
    return pltpu.make_async_copy(src, dst, sem)


def _load_row_tiles(ref, rows):
    return jnp.concatenate([ref[pl.ds(j, rows, stride=SUBLANES), :] for j in range(SUBLANES)], axis=-1)


def _dispatch_kernel(dest_ref, h_ref, zeros_ref, xs_ref, sem):
    del zeros_ref
    tm = h_ref.shape[0] // SUBLANES
    base = pl.program_id(0) * (tm * TOP_K)

    def issue(r, carry):
        for kk in range(TOP_K):
            _row_copy(h_ref, r, xs_ref, dest_ref[base + r * TOP_K + kk], sem).start(priority=kk)
        return carry

    lax.fori_loop(0, tm, issue, 0, unroll=DMA_UNROLL)

    def drain(r, carry):
        for kk in range(TOP_K):
            _row_copy(h_ref, r, xs_ref, dest_ref[base + r * TOP_K + kk], sem).wait()
        return carry

    lax.fori_loop(0, tm, drain, 0, unroll=DMA_UNROLL)


def _dispatch(dest, h_tiles, n_slots):
    tm = MOE_ROWS
    T = h_tiles.shape[0] // SUBLANES
    zeros = jnp.zeros((n_slots * SUBLANES, LANES), F32)
    return pl.pallas_call(
        _dispatch_kernel,
        grid_spec=pltpu.PrefetchScalarGridSpec(
            num_scalar_prefetch=1,
            grid=(T // tm,),
            in_specs=[pl.BlockSpec((tm * SUBLANES, LANES), lambda i, d: (i, 0)),
                      pl.BlockSpec(memory_space=pl.ANY)],
            out_specs=pl.BlockSpec(memory_space=pl.ANY),
            scratch_shapes=[pltpu.SemaphoreType.DMA(())],
        ),
        out_shape=jax.ShapeDtypeStruct((n_slots * SUBLANES, LANES), F32),
        input_output_aliases={2: 0},
        compiler_params=_cparams(("arbitrary",), 32),
        name="moe_dispatch",
    )(dest, h_tiles, zeros)


def _expert_kernel(be_ref, nu_ref, xs_ref, wg_ref, wu_ref, wd_ref, y_ref, wgb_ref, wub_ref, wdb_ref):
    i = pl.program_id(0)
    used = i < nu_ref[0]
    new_expert = (i == 0) | (be_ref[i] != be_ref[jnp.maximum(i - 1, 0)])

    @pl.when(used & new_expert)
    def _():
        wgb_ref[...] = wg_ref[...].astype(BF16)
        wub_ref[...] = wu_ref[...].astype(BF16)
        wdb_ref[...] = wd_ref[...].astype(BF16)

    @pl.when(used)
    def _():
        xb = _load_row_tiles(xs_ref, MOE_BLOCK).astype(BF16)
        gate = _dot(xb, wgb_ref[...])
        up = _dot(xb, wub_ref[...])
        act = (gate * jax.nn.sigmoid(gate) * up).astype(BF16)
        y = _dot(act, wdb_ref[...])
        for j in range(SUBLANES):
            y_ref[pl.ds(j, MOE_BLOCK, stride=SUBLANES), :] = y[:, j * LANES:(j + 1) * LANES]

    @pl.when(jnp.logical_not(used))
    def _():
        y_ref[...] = jnp.zeros_like(y_ref)


def _expert_ffn(block_expert, n_used, xs_tiles, layer, wg, wu, wd):
    D, FF = wg.shape[2], wg.shape[3]
    n_blocks = xs_tiles.shape[0] // (MOE_BLOCK * SUBLANES)
    rows = MOE_BLOCK * SUBLANES
    blk = lambda i, be, nu: (jnp.minimum(i, nu[0] - 1), 0)
    wsel = lambda i, be, nu: (layer, be[jnp.minimum(i, nu[0] - 1)], 0, 0)
    return pl.pallas_call(
        _expert_kernel,
        grid_spec=pltpu.PrefetchScalarGridSpec(
            num_scalar_prefetch=2,
            grid=(n_blocks,),
            in_specs=[pl.BlockSpec((rows, LANES), blk),
                      pl.BlockSpec((None, None, D, FF), wsel),
                      pl.BlockSpec((None, None, D, FF), wsel),
                      pl.BlockSpec((None, None, FF, D), wsel)],
            out_specs=pl.BlockSpec((rows, LANES), lambda i, be, nu: (i, 0)),
            scratch_shapes=[pltpu.VMEM((D, FF), BF16), pltpu.VMEM((D, FF), BF16), pltpu.VMEM((FF, D), BF16)],
        ),
        out_shape=jax.ShapeDtypeStruct(xs_tiles.shape, F32),
        compiler_params=_cparams(("arbitrary",), 48),
        name="moe_experts",
    )(block_expert, n_used, xs_tiles, wg, wu, wd)


def _combine_kernel(dest_ref, x_ref, wt_ref, yb_ref, o_ref, buf_ref, sem):
    tm = x_ref.shape[0]
    base = pl.program_id(0) * (tm * TOP_K)

    def issue(r, carry):
        for kk in range(TOP_K):
            _row_copy(yb_ref, dest_ref[base + r * TOP_K + kk], buf_ref.at[kk], r, sem).start(priority=kk)
        return carry

    lax.fori_loop(0, tm, issue, 0, unroll=DMA_UNROLL)

    def drain(r, carry):
        for kk in range(TOP_K):
            _row_copy(yb_ref, dest_ref[base + r * TOP_K + kk], buf_ref.at[kk], r, sem).wait()
        return carry

    lax.fori_loop(0, tm, drain, 0, unroll=DMA_UNROLL)
    wt = wt_ref[...]
    w1 = wt[:, 2:3]
    w2 = wt[:, 3:4]
    for j in range(SUBLANES):
        cols = slice(j * LANES, (j + 1) * LANES)
        y1 = buf_ref[0, pl.ds(j, tm, stride=SUBLANES), :]
        y2 = buf_ref[1, pl.ds(j, tm, stride=SUBLANES), :]
        o_ref[:, cols] = x_ref[:, cols] + (y1 * w1 + y2 * w2)


def _combine(dest, x2d, rt2d, yb_tiles):
    T, D = x2d.shape
    tm = MOE_ROWS
    return pl.pallas_call(
        _combine_kernel,
        grid_spec=pltpu.PrefetchScalarGridSpec(
            num_scalar_prefetch=1,
            grid=(T // tm,),
            in_specs=[pl.BlockSpec((tm, D), lambda i, d: (i, 0)),
                      pl.BlockSpec((tm, ROUTE_LANES), lambda i, d: (i, 0)),
                      pl.BlockSpec(memory_space=pl.ANY)],
            out_specs=pl.BlockSpec((tm, D), lambda i, d: (i, 0)),
            scratch_shapes=[pltpu.VMEM((TOP_K, tm * SUBLANES, LANES), F32), pltpu.SemaphoreType.DMA(())],
        ),
        out_shape=jax.ShapeDtypeStruct((T, D), F32),
        compiler_params=_cparams(("arbitrary",), 48),
        name="moe_combine",
    )(dest, x2d, rt2d, yb_tiles)


def _moe_plan(rt2d, counts_lanes):
    T = rt2d.shape[0]
    A = T * TOP_K
    e = rt2d[:, 0:TOP_K].astype(jnp.int32).reshape(A)
    rank = rt2d[:, 4:4 + TOP_K].astype(jnp.int32).reshape(A)
    counts = counts_lanes[0, N_GROUPS:N_GROUPS + N_EXPERTS].astype(jnp.int32)
    padded = ((counts + MOE_BLOCK - 1) // MOE_BLOCK) * MOE_BLOCK
    pends = jnp.cumsum(padded)
    pstarts = pends - padded
    onehot = e[:, None] == jnp.arange(N_EXPERTS, dtype=jnp.int32)[None, :]
    dest = jnp.sum(jnp.where(onehot, pstarts[None, :], 0), axis=1) + rank
    n_blocks = (A + N_EXPERTS * (MOE_BLOCK - 1) + MOE_BLOCK - 1) // MOE_BLOCK
    block_start = jnp.arange(n_blocks, dtype=jnp.int32) * MOE_BLOCK
    block_expert = jnp.minimum(jnp.sum(pends[None, :] <= block_start[:, None], axis=1), N_EXPERTS - 1)
    n_used = (pends[-1] // MOE_BLOCK).astype(jnp.int32).reshape(1)
    return dest.astype(jnp.int32), block_expert.astype(jnp.int32), n_used, n_blocks * MOE_BLOCK


def _to_scan_layout(t3):
    B, S, _ = t3.shape
    return t3.reshape(B, S, RWKV_HEADS, HEAD_DIM).transpose(1, 3, 0, 2).reshape(S, HEAD_DIM, B * RWKV_HEADS)


def _from_scan_layout(t, B):
    S = t.shape[0]
    return t.reshape(S, HEAD_DIM, B, RWKV_HEADS).transpose(2, 0, 3, 1).reshape(B, S, RWKV_WIDTH)


def _scan_param(p, B):
    return jnp.tile(p.reshape(RWKV_HEADS, HEAD_DIM).T, (1, B))


def _rel_bias_pairs(table):
    n_ext = BAND + CHUNK - 1
    ext = jnp.concatenate([table, jnp.broadcast_to(table[:, -1:], (table.shape[0], n_ext - table.shape[1]))], axis=1)
    rev = ext[:, ::-1]
    bias = jnp.stack([rev[:, CHUNK - 1 - i:CHUNK - 1 - i + BAND] for i in range(CHUNK)], axis=1)
    return (bias.astype(F32) * LOG2E).reshape(ATT_HEADS // 2, 2 * CHUNK, BAND)


def _pad_rows(w, rows, offset=0):
    out = jnp.zeros((rows, w.shape[1]), w.dtype)
    return out.at[offset:offset + w.shape[0]].set(w)


def kernel(x, mem, norm_mix, w_in, shift_mu, decay_w0, decay_w2, iclr_a0, iclr_a2, gate_g2, rw_k_k, rw_k_a, rw_r_k, lnx_w, lnx_b, vres_v0, vres_v1, vres_v2, att_q_norm, att_k_norm, att_rel_bias, att_out_norm, w_out, norm_xa, norm_mem, xa_w_q, xa_w_kv, xa_w_o, xa_q_norm, xa_k_norm, norm_ffn, moe_w_group, moe_b_group, moe_w_route, moe_b_route, moe_w_gate, moe_w_up, moe_w_down):
    B, S, D = x.shape
    T = B * S
    depth = w_in.shape[0]
    M = mem.shape[1]
    row = lambda p: p.reshape(1, -1)
    ones_bd = jnp.kron(jnp.eye(ATT_HEADS, dtype=F32), jnp.ones((HEAD_DIM, HEAD_DIM), F32)).astype(BF16)

    v_first = None
    for l in range(depth):
        u, qh, kh, vh = _inproj(x.reshape(T, D), row(norm_mix[l]), w_in[l].astype(BF16),
                                row(jnp.tile(att_q_norm[l], ATT_HEADS)),
                                row(jnp.tile(att_k_norm[l], ATT_HEADS)), ones_bd)
        w2p = _pad_rows(decay_w2[l], LANES, 0).astype(BF16)
        a2p = _pad_rows(iclr_a2[l], LANES, DECAY_LORA).astype(BF16)
        vres = None
        if l > 0:
            v1p = jnp.zeros((RWKV_WIDTH, LANES), F32).at[:, :MV_LORA].set(vres_v1[l - 1]).astype(BF16)
            v2p = _pad_rows(vres_v2[l - 1], LANES, 0).astype(BF16)
            vres = (v_first, row(vres_v0[l - 1]), v1p, v2p)
        r3, w3, k3, v3, a3, g3 = _rwkv_prep(u.reshape(B, S, RWKV_IN), row(shift_mu[l]), row(decay_w0[l]),
                                            w2p, row(iclr_a0[l]), a2p, gate_g2[l].astype(BF16), vres)
        if l == 0:
            v_first = v3
        y_scan = _wkv_scan(_to_scan_layout(r3), _to_scan_layout(w3), _to_scan_layout(k3),
                           _to_scan_layout(v3), _to_scan_layout(a3),
                           _scan_param(rw_k_k[l], B), _scan_param(rw_k_a[l], B), _scan_param(rw_r_k[l], B),
                           _scan_param(lnx_w[l], B), _scan_param(lnx_b[l], B))
        yr3 = _from_scan_layout(y_scan, B)
        pad = ((0, 0), (BAND_PAD, 0), (0, 0))
        ya3 = _band_attn(qh.reshape(B, S, ATT_WIDTH), jnp.pad(kh.reshape(B, S, ATT_WIDTH), pad),
                         jnp.pad(vh.reshape(B, S, ATT_WIDTH), pad), _rel_bias_pairs(att_rel_bias[l]))
        km, vm = _mem_kv(mem.reshape(B * M, D), row(norm_mem[l]), xa_w_kv[l].astype(BF16), row(xa_k_norm[l]))
        wrt = jnp.zeros((D, ROUTE_LANES), F32)
        wrt = wrt.at[:, :N_GROUPS].set(moe_w_group[l]).at[:, N_GROUPS:N_GROUPS + N_EXPERTS].set(moe_w_route[l])
        brt = jnp.zeros((ROUTE_LANES,), F32)
        brt = brt.at[:N_GROUPS].set(moe_b_group[l]).at[N_GROUPS:N_GROUPS + N_EXPERTS].set(moe_b_route[l])
        x3, h_tiles, rt3, counts = _mix_xa_router(
            x, yr3, g3, ya3, row(att_out_norm[l]), w_out[l].astype(BF16), row(norm_xa[l]),
            xa_w_q[l].astype(BF16), row(xa_q_norm[l]), km.reshape(B, M, XA_WIDTH), vm.reshape(B, M, XA_WIDTH),
            xa_w_o[l].astype(BF16), row(norm_ffn[l]), wrt.astype(BF16), row(brt))
        rt2d = rt3.reshape(T, ROUTE_LANES)
        dest, block_expert, n_used, n_slots = _moe_plan(rt2d, counts)
        xs = _dispatch(dest, h_tiles, n_slots)
        yb = _expert_ffn(block_expert, n_used, xs, l, moe_w_gate, moe_w_up, moe_w_down)
        x = _combine(dest, x3.reshape(T, D), rt2d, yb).reshape(B, S, D)
    return x
```

```python
import functools

import jax
import jax.numpy as jnp
from jax import lax
from jax.experimental import pallas as pl
from jax.experimental.pallas import tpu as pltpu

F32 = jnp.float32
BF16 = jnp.bfloat16

CHUNK = 64
RWKV_HEADS = 8
HEAD_DIM = 64
RWKV_WIDTH = 512
ATT_HEADS = 8
ATT_WIDTH = 512
DECAY_LORA = 64
AAA_LORA = 64
MV_LORA = 32
GATE_LORA = 128
RWKV_IN = 3 * RWKV_WIDTH + DECAY_LORA + AAA_LORA + GATE_LORA
LORA_OFF = 3 * RWKV_WIDTH
GATE_OFF = LORA_OFF + DECAY_LORA + AAA_LORA
IN_WIDTH = RWKV_IN + 3 * ATT_WIDTH
PREV_CHUNKS = 8
BAND = (PREV_CHUNKS + 1) * CHUNK
BAND_PAD = PREV_CHUNKS * CHUNK
REL_MAX = 256
XA_HEADS = 4
XA_HEAD_DIM = 128
XA_WIDTH = 512
N_GROUPS = 4
EXPERTS_PER_GROUP = 8
N_EXPERTS = 32
TOP_K = 2
EXPERT_FF = 512
MOE_BLOCK = 256
NORM_EPS = 1e-6
LNX_EPS = 64e-5
NEG_INF = -1e30

LANES = 128
SUBLANES = 8
ROW_TILE = 256
MIX_ROWS = 512
LOG2E = 1.4426950408889634
WKV_ROWS = 256
WKV_SEQS = 2
NEUMANN_SQUARINGS =CHUNK.bit_length() - 2
ATT_ROWS = 256
MOE_ROWS = 512
ROUTE_LANES = 128
DMA_UNROLL = 8


def _cparams(semantics, vmem_mib):
    return pltpu.CompilerParams(dimension_semantics=semantics,
                                vmem_limit_bytes=vmem_mib * 1024 * 1024)


def _dot(a, b):
    return jnp.dot(a, b, preferred_element_type=F32)


def _dot_nt(a, b):
    return lax.dot_general(a, b, (((1,), (1,)), ((), ())), preferred_element_type=F32)


def _rms(x, g):
    ms = jnp.mean(x * x, axis=-1, keepdims=True)
    return x * lax.rsqrt(ms + NORM_EPS) * g


def _group_sum(x, ones_bd):
    hi = x.astype(BF16)
    lo = (x - hi.astype(F32)).astype(BF16)
    return _dot(hi, ones_bd) + _dot(lo, ones_bd)


def _group_sumsq(x, ones_bd):
    return _group_sum(x * x, ones_bd)


def _inproj_kernel(x_ref, g_ref, w_ref, gq_ref, gk_ref, bd_ref, u_ref, q_ref, k_ref, v_ref):
    h = _rms(x_ref[...], g_ref[...]).astype(BF16)
    for j in range(0, RWKV_IN, 256):
        u_ref[:, j:j + 256] = _dot(h, w_ref[:, j:j + 256])
    bd = bd_ref[...]
    q = _dot(h, w_ref[:, RWKV_IN:RWKV_IN + ATT_WIDTH])
    qn = q * lax.rsqrt(_group_sumsq(q, bd) * (1.0 / HEAD_DIM) + NORM_EPS) * gq_ref[...]
    q_ref[...] = (qn * (HEAD_DIM ** -0.5 * LOG2E)).astype(BF16)
    k = _dot(h, w_ref[:, RWKV_IN + ATT_WIDTH:RWKV_IN + 2 * ATT_WIDTH])
    kn = k * lax.rsqrt(_group_sumsq(k, bd) * (1.0 / HEAD_DIM) + NORM_EPS) * gk_ref[...]
    k_ref[...] = kn.astype(BF16)
    v_ref[...] = _dot(h, w_ref[:, RWKV_IN + 2 * ATT_WIDTH:]).astype(BF16)


def _inproj(x2d, g, w, gq, gk, bd):
    T, D = x2d.shape
    tm = ROW_TILE
    full = lambda i: (0, 0)
    row = lambda i: (i, 0)
    return pl.pallas_call(
        _inproj_kernel,
        grid=(T // tm,),
        in_specs=[pl.BlockSpec((tm, D), row), pl.BlockSpec((1, D), full),
                  pl.BlockSpec((D, IN_WIDTH), full), pl.BlockSpec((1, ATT_WIDTH), full),
                  pl.BlockSpec((1, ATT_WIDTH), full), pl.BlockSpec((ATT_WIDTH, ATT_WIDTH), full)],
        out_specs=[pl.BlockSpec((tm, RWKV_IN), row), pl.BlockSpec((tm, ATT_WIDTH), row),
                   pl.BlockSpec((tm, ATT_WIDTH), row), pl.BlockSpec((tm, ATT_WIDTH), row)],
        out_shape=[jax.ShapeDtypeStruct((T, RWKV_IN), F32), jax.ShapeDtypeStruct((T, ATT_WIDTH), BF16),
                   jax.ShapeDtypeStruct((T, ATT_WIDTH), BF16), jax.ShapeDtypeStruct((T, ATT_WIDTH), BF16)],
        compiler_params=_cparams(("parallel",), 48),
        name="inproj",
    )(x2d, g, w, gq, gk, bd)


def _rwkv_prep_kernel(has_vres, *refs):
    if has_vres:
        (u_ref, up_ref, mu_ref, w0_ref, w2_ref, a0_ref, a2_ref, g2_ref, kkp_ref, kap_ref, rkp_ref,
         bd_ref, tri_ref, vf_ref, v0_ref, v1_ref, v2_ref,
         at_ref, rt_ref, bt_ref, kt_ref, vb_ref, ge_ref, bonus_ref, g_ref) = refs
    else:
        (u_ref, up_ref, mu_ref, w0_ref, w2_ref, a0_ref, a2_ref, g2_ref, kkp_ref, kap_ref, rkp_ref,
         bd_ref, tri_ref,
         at_ref, rt_ref, bt_ref, kt_ref, vb_ref, ge_ref, bonus_ref, g_ref, v_ref) = refs
    i = pl.program_id(1)
    u = u_ref[...]
    ts = u.shape[0]
    prev_row = jnp.where(i > 0, up_ref[SUBLANES - 1:SUBLANES, :], 0.0)
    rolled = pltpu.roll(u, 1, 0)
    row_id = lax.broadcasted_iota(jnp.int32, u.shape, 0)
    shifted = jnp.where(row_id == 0, jnp.broadcast_to(prev_row, u.shape), rolled)
    u = u + mu_ref[...] * (shifted - u)

    r = u[:, 0:RWKV_WIDTH]
    k = u[:, RWKV_WIDTH:2 * RWKV_WIDTH]
    v = u[:, 2 * RWKV_WIDTH:3 * RWKV_WIDTH]
    lora = u[:, LORA_OFF:GATE_OFF]
    gl = u[:, GATE_OFF:RWKV_IN]

    w = w0_ref[...] + _dot(jnp.tanh(lora).astype(BF16), w2_ref[...])
    z = -w
    softplus = jnp.maximum(z, 0.0) + jnp.log1p(jnp.exp(-jnp.abs(z)))
    w = -softplus - 0.5
    log_decay = -jnp.exp(w)
    a = jax.nn.sigmoid(a0_ref[...] + _dot(lora.astype(BF16), a2_ref[...]))
    g_ref[...] = _dot(jax.nn.sigmoid(gl).astype(BF16), g2_ref[...])
    if has_vres:
        mix = _dot(_dot(v.astype(BF16), v1_ref[...]).astype(BF16), v2_ref[...])
        v = v + (vf_ref[...] - v) * jax.nn.sigmoid(v0_ref[...] + mix)
    else:
        v_ref[...] = v

    bd = bd_ref[...]
    kk = k * kkp_ref[...]
    kk = kk * lax.rsqrt(jnp.maximum(_group_sumsq(kk, bd), 1e-24))
    k2 = k * (1.0 + (a - 1.0) * kap_ref[...])
    bonus_ref[...] = _group_sum(r * k2 * rkp_ref[...], bd) * v

    hi = log_decay.astype(BF16)
    lo = (log_decay - hi.astype(F32)).astype(BF16)
    tri = tri_ref[...]
    cum = _dot(tri, hi) + _dot(tri, lo)
    inv_gamma = jnp.exp(-cum)
    at_ref[...] = (-kk * jnp.exp(cum - log_decay)).astype(BF16)
    rt_ref[...] = (r * jnp.exp(cum)).astype(BF16)
    bt_ref[...] = (kk * a * inv_gamma).astype(BF16)
    kt_ref[...] = (k2 * inv_gamma).astype(BF16)
    vb_ref[...] = v.astype(BF16)
    for c in range(ts // CHUNK):
        ge_ref[c] = jnp.exp(jnp.sum(log_decay[c * CHUNK:(c + 1) * CHUNK, :], axis=0, keepdims=True))


def _rwkv_prep(u3, mu, w0, w2p, a0, a2p, g2, kkp, kap, rkp, ones_bd, vres):
    B, S, _ = u3.shape
    ts = ROW_TILE
    W = RWKV_WIDTH
    n_c = ts // CHUNK
    full2 = lambda b, i: (0, 0)
    tile = lambda b, i: (b, i, 0)
    prev = lambda b, i: (b, jnp.maximum(i * (ts // SUBLANES) - 1, 0), 0)
    tri = (jnp.tril(jnp.ones((ts, ts), F32))
           * jnp.kron(jnp.eye(n_c, dtype=F32), jnp.ones((CHUNK, CHUNK), F32))).astype(BF16)
    in_specs = [pl.BlockSpec((None, ts, RWKV_IN), tile),
                pl.BlockSpec((None, SUBLANES, RWKV_IN), prev),
                pl.BlockSpec((1, RWKV_IN), full2), pl.BlockSpec((1, W), full2),
                pl.BlockSpec((LANES, W), full2), pl.BlockSpec((1, W), full2),
                pl.BlockSpec((LANES, W), full2), pl.BlockSpec((GATE_LORA, W), full2),
                pl.BlockSpec((1, W), full2), pl.BlockSpec((1, W), full2), pl.BlockSpec((1, W), full2),
                pl.BlockSpec((W, W), full2), pl.BlockSpec((ts, ts), full2)]
    args = [u3, u3, mu, w0, w2p, a0, a2p, g2, kkp, kap, rkp, ones_bd, tri]
    if vres is not None:
        v_first, v0, v1p, v2p = vres
        in_specs += [pl.BlockSpec((None, ts, W), tile), pl.BlockSpec((1, W), full2),
                     pl.BlockSpec((W, LANES), full2), pl.BlockSpec((LANES, W), full2)]
        args += [v_first, v0, v1p, v2p]
    tok = pl.BlockSpec((None, ts, W), tile)
    out_specs = [tok] * 5 + [pl.BlockSpec((None, n_c, 1, W), lambda b, i: (b, i, 0, 0)), tok, tok]
    out_shape = ([jax.ShapeDtypeStruct((B, S, W), BF16)] * 5
                 + [jax.ShapeDtypeStruct((B, S // CHUNK, 1, W), F32)]
                 + [jax.ShapeDtypeStruct((B, S, W), F32)] * 2)
    if vres is None:
        out_specs.append(tok)
        out_shape.append(jax.ShapeDtypeStruct((B, S, W), F32))
    return pl.pallas_call(
        functools.partial(_rwkv_prep_kernel, vres is not None),
        grid=(B, S // ts),
        in_specs=in_specs,
        out_specs=out_specs,
        out_shape=out_shape,
        compiler_params=_cparams(("parallel", "parallel"), 48),
        name="rwkv_prep",
    )(*args)


def _wkv_chunk_kernel(at_ref, rt_ref, bt_ref, kt_ref, vb_ref, ge_ref, bonus_ref, lw_ref, lb_ref,
                      y_ref, n_ref):
    C = CHUNK
    n_pairs = RWKV_HEADS // 2
    n_seqs = at_ref.shape[0]
    n_chunks = at_ref.shape[1] // C

    @pl.when(pl.program_id(1) == 0)
    def _():
        n_ref[...] = jnp.zeros_like(n_ref)

    row = lax.broadcasted_iota(jnp.int32, (2 * C, LANES), 0)
    lane = lax.broadcasted_iota(jnp.int32, (2 * C, LANES), 1)
    top = row < C
    left = lane < HEAD_DIM
    same = top == left
    t_row = jnp.where(top, row, row - C)
    s_col = jnp.where(left, lane, lane - HEAD_DIM)
    strict = s_col < t_row
    incl = s_col <= t_row
    even = lax.broadcasted_iota(jnp.int32, (C, LANES), 1) < HEAD_DIM
    zeros_c = jnp.zeros((C, LANES), BF16)

    units = [(c, b, p) for c in range(n_chunks) for b in range(n_seqs) for p in range(n_pairs)]

    def tile(ref, unit):
        c, b, p = unit
        return ref[b, c * C:(c + 1) * C, p * LANES:(p + 1) * LANES]

    def stack_heads(x):
        z = jnp.zeros_like(x)
        return jnp.concatenate([jnp.where(even, x, z), jnp.where(even, z, x)], axis=0)

    def half_sum(x):
        lo = jnp.sum(jnp.where(even, x, 0.0), axis=-1, keepdims=True)
        hi = jnp.sum(jnp.where(even, 0.0, x), axis=-1, keepdims=True)
        return jnp.where(even, lo, hi)

    at2 = [stack_heads(tile(at_ref, un)) for un in units]
    bk = [jnp.concatenate([tile(bt_ref, un), tile(kt_ref, un)], axis=0) for un in units]
    ga = [jnp.where(strict, _dot_nt(a, b), 0.0) for a, b in zip(at2, bk)]
    gr = [jnp.where(incl, _dot_nt(stack_heads(tile(rt_ref, un)), b), 0.0).astype(BF16)
          for un, b in zip(units, bk)]
    zs = [jnp.where(same, _dot(g.astype(BF16), jnp.concatenate([zeros_c, tile(vb_ref, un)], axis=0)), 0.0)
          for un, g in zip(units, ga)]
    pw = [jnp.where(same, jnp.where(top, g, pltpu.roll(g, HEAD_DIM, 1)), 0.0).astype(BF16) for g in ga]
    w = [jnp.concatenate([a.astype(F32), z], axis=1) for a, z in zip(at2, zs)]
    w = [x + _dot(m, x.astype(BF16)) for m, x in zip(pw, w)]
    for _ in range(NEUMANN_SQUARINGS):
        pw = [_dot(m, m).astype(BF16) for m in pw]
        w = [x + _dot(m, x.astype(BF16)) for m, x in zip(pw, w)]
    pm = [(x[:C, :LANES] + x[C:, :LANES]).astype(BF16) for x in w]
    q = [x[:C, LANES:] + x[C:, LANES:] for x in w]

    for i, un in enumerate(units):
        c, b, p = un
        cols = slice(p * LANES, (p + 1) * LANES)
        rows = slice(c * C, (c + 1) * C)
        n0 = n_ref[b, p]
        pr = _dot_nt(jnp.concatenate([pm[i], tile(rt_ref, un)], axis=0), n0.astype(BF16))
        u = pr[:C] + q[i]
        uv = jnp.concatenate([u.astype(BF16), tile(vb_ref, un)], axis=0)
        yp = _dot(gr[i], uv)
        y = pr[C:] + jnp.where(even, yp[:C], yp[C:])
        g_slab = ge_ref[b, c, :, cols]
        bkg = (bk[i].astype(F32) * g_slab).astype(BF16)
        dn = lax.dot_general(uv, bkg, (((0,), (0,)), ((), ())), preferred_element_type=F32)
        n_ref[b, p] = g_slab * n0 + jnp.where(same, dn, 0.0)
        mean = half_sum(y) * (1.0 / HEAD_DIM)
        yc = y - mean
        var = half_sum(yc * yc) * (1.0 / HEAD_DIM)
        yn = yc * lax.rsqrt(var + LNX_EPS) * lw_ref[:, cols] + lb_ref[:, cols]
        y_ref[b, rows, cols] = yn + bonus_ref[b, rows, cols]


def _wkv_chunked(at, rt, bt, kt, vb, ge, bonus, lnw, lnb):
    B, S, W = at.shape
    ts = WKV_ROWS
    nb = WKV_SEQS if B % WKV_SEQS == 0 else 1
    n_c = ts // CHUNK
    tile = lambda b, i: (b, i, 0)
    full = lambda b, i: (0, 0)
    tok = pl.BlockSpec((nb, ts, W), tile)
    return pl.pallas_call(
        _wkv_chunk_kernel,
        grid=(B // nb, S // ts),
        in_specs=[tok] * 5 + [pl.BlockSpec((nb, n_c, 1, W), lambda b, i: (b, i, 0, 0)), tok,
                              pl.BlockSpec((1, W), full), pl.BlockSpec((1, W), full)],
        out_specs=tok,
        out_shape=jax.ShapeDtypeStruct((B, S, W), F32),
        scratch_shapes=[pltpu.VMEM((nb, RWKV_HEADS // 2, 2 * HEAD_DIM, LANES), F32)],
        compiler_params=_cparams(("parallel", "arbitrary"), 32),
        name="wkv_chunked",
    )(at, rt, bt, kt, vb, ge, bonus, lnw, lnb)


def _band_attn_kernel(q_ref, k_ref, v_ref, bias_ref, o_ref, s_ref, e_ref):
    i = pl.program_id(1)
    n_chunks = q_ref.shape[0] // CHUNK
    n_pairs = ATT_HEADS // 2
    lane = lax.broadcasted_iota(jnp.int32, (CHUNK, LANES), 1)
    even = lane < HEAD_DIM
    slot = lax.broadcasted_iota(jnp.int32, (1, 1, BAND), 2)

    def chunk_body(masked, c, carry):
        cg = i * n_chunks + c
        start = pl.multiple_of(cg * CHUNK, CHUNK)
        row0 = pl.multiple_of(c * CHUNK, CHUNK)
        for p in range(n_pairs):
            cols = slice(p * LANES, (p + 1) * LANES)
            q2 = q_ref[pl.ds(row0, CHUNK), cols]
            zero = jnp.zeros_like(q2)
            qs = jnp.concatenate([jnp.where(even, q2, zero), jnp.where(even, zero, q2)], axis=0)
            s_ref[p] = _dot_nt(qs, k_ref[pl.ds(start, BAND), cols]) + bias_ref[p]
        s = s_ref[...]
        if masked:
            s = jnp.where(slot >= (BAND_PAD - cg * CHUNK), s, NEG_INF)
        e = jnp.exp2(s - jnp.max(s, axis=-1, keepdims=True))
        inv = 1.0 / jnp.sum(e, axis=-1, keepdims=True)
        e_ref[...] = e.astype(BF16)
        for p in range(n_pairs):
            cols = slice(p * LANES, (p + 1) * LANES)
            o = _dot(e_ref[p], v_ref[pl.ds(start, BAND), cols]) * inv[p]
            o_ref[pl.ds(row0, CHUNK), cols] = jnp.where(even, o[:CHUNK], o[CHUNK:])
        return carry

    @pl.when(i * n_chunks < PREV_CHUNKS)
    def _():
        lax.fori_loop(0, n_chunks, functools.partial(chunk_body, True), 0)

    @pl.when(i * n_chunks >= PREV_CHUNKS)
    def _():
        lax.fori_loop(0, n_chunks, functools.partial(chunk_body, False), 0)


def _band_attn(q3, kp3, vp3, bias_pairs):
    B, S, W = q3.shape
    SP = kp3.shape[1]
    tq = ATT_ROWS
    assert PREV_CHUNKS % (tq // CHUNK) == 0
    n_pairs = ATT_HEADS // 2
    return pl.pallas_call(
        _band_attn_kernel,
        grid=(B, S // tq),
        in_specs=[pl.BlockSpec((None, tq, W), lambda b, i: (b, i, 0)),
                  pl.BlockSpec((None, SP, W), lambda b, i: (b, 0, 0)),
                  pl.BlockSpec((None, SP, W), lambda b, i: (b, 0, 0)),
                  pl.BlockSpec((ATT_HEADS // 2, 2 * CHUNK, BAND), lambda b, i: (0, 0, 0))],
        out_specs=pl.BlockSpec((None, tq, W), lambda b, i: (b, i, 0)),
        out_shape=jax.ShapeDtypeStruct((B, S, W), F32),
        scratch_shapes=[pltpu.VMEM((n_pairs, 2 * CHUNK, BAND), F32),
                        pltpu.VMEM((n_pairs, 2 * CHUNK, BAND), BF16)],
        compiler_params=_cparams(("parallel", "arbitrary"), 48),
        name="band_attn",
    )(q3, kp3, vp3, bias_pairs)


def _mem_kv_kernel(m_ref, g_ref, w_ref, kn_ref, k_ref, v_ref):
    h = _rms(m_ref[...], g_ref[...]).astype(BF16)
    kv = _dot(h, w_ref[...])
    for hd in range(XA_HEADS):
        cols = slice(hd * XA_HEAD_DIM, (hd + 1) * XA_HEAD_DIM)
        k_ref[:, cols] = _rms(kv[:, cols], kn_ref[...]).astype(BF16)
    v_ref[...] = kv[:, XA_WIDTH:].astype(BF16)


def _mem_kv(mem2d, g, w_kv, k_norm):
    R, D = mem2d.shape
    tm = ROW_TILE
    full = lambda i: (0, 0)
    row = lambda i: (i, 0)
    return pl.pallas_call(
        _mem_kv_kernel,
        grid=(R // tm,),
        in_specs=[pl.BlockSpec((tm, D), row), pl.BlockSpec((1, D), full),
                  pl.BlockSpec((D, 2 * XA_WIDTH), full), pl.BlockSpec((1, XA_HEAD_DIM), full)],
        out_specs=[pl.BlockSpec((tm, XA_WIDTH), row)] * 2,
        out_shape=[jax.ShapeDtypeStruct((R, XA_WIDTH), BF16)] * 2,
        compiler_params=_cparams(("parallel",), 32),
        name="mem_kv",
    )(mem2d, g, w_kv, k_norm)


def _mix_xa_router_kernel(x_ref, yr_ref, g_ref, ya_ref, gao_ref, wout_ref, gxa_ref, wq_ref, qn_ref,
                          km_ref, vm_ref, wo_ref, gff_ref, wrt_ref, brt_ref,
                          tri_ref, xo_ref, h_ref, rt_ref, cnt_ref, run_ref):
    first = (pl.program_id(0) == 0) & (pl.program_id(1) == 0)

    @pl.when(first)
    def _():
        run_ref[...] = jnp.zeros_like(run_ref)

    yr = (yr_ref[...] * g_ref[...]).astype(BF16)
    ya = _rms(ya_ref[...], gao_ref[...]).astype(BF16)
    x = x_ref[...] + _dot(yr, wout_ref[0:RWKV_WIDTH, :]) + _dot(ya, wout_ref[RWKV_WIDTH:, :])

    q = _dot(_rms(x, gxa_ref[...]).astype(BF16), wq_ref[...])
    heads = []
    for hd in range(XA_HEADS):
        cols = slice(hd * XA_HEAD_DIM, (hd + 1) * XA_HEAD_DIM)
        qh = _rms(q[:, cols], qn_ref[...]).astype(BF16)
        s = _dot_nt(qh, km_ref[:, cols]) * (XA_HEAD_DIM ** -0.5)
        m = jnp.max(s, axis=-1, keepdims=True)
        e = jnp.exp(s - m)
        pr = e / jnp.sum(e, axis=-1, keepdims=True)
        heads.append(_dot(pr.astype(BF16), vm_ref[:, cols]))
    o = jnp.concatenate(heads, axis=-1).astype(BF16)
    x = x + _dot(o, wo_ref[...])
    xo_ref[...] = x

    h = _rms(x, gff_ref[...])
    tm = h.shape[0]
    for j in range(h.shape[1] // LANES):
        h_ref[pl.ds(j, tm, stride=SUBLANES), :] = h[:, j * LANES:(j + 1) * LANES]
    logits =_dot(h.astype(BF16), wrt_ref[...]) + brt_ref[...]
    lane_i = lax.broadcasted_iota(jnp.int32, logits.shape, 1)
    lane = lane_i.astype(F32)
    big = float(ROUTE_LANES)
    gmask = lane < N_GROUPS
    gmax = jnp.max(jnp.where(gmask, logits, -jnp.inf), axis=-1, keepdims=True)
    ge = jnp.where(gmask, jnp.exp(logits - gmax), 0.0)
    gp = ge / jnp.sum(ge, axis=-1, keepdims=True)
    g_gate = jnp.max(gp, axis=-1, keepdims=True)
    g_top = jnp.min(jnp.where(gmask & (gp == g_gate), lane, big), axis=-1, keepdims=True)
    lo = N_GROUPS + g_top * EXPERTS_PER_GROUP
    rmask = (lane >= lo) & (lane < lo + EXPERTS_PER_GROUP)
    rmax = jnp.max(jnp.where(rmask, logits, -jnp.inf), axis=-1, keepdims=True)
    re = jnp.where(rmask, jnp.exp(logits - rmax), 0.0)
    p_in = re / jnp.sum(re, axis=-1, keepdims=True)
    p1 = jnp.max(jnp.where(rmask, p_in, -1.0), axis=-1, keepdims=True)
    j1 = jnp.min(jnp.where(rmask & (p_in == p1), lane, big), axis=-1, keepdims=True)
    rmask2 = rmask & (lane != j1)
    p2 = jnp.max(jnp.where(rmask2, p_in, -1.0), axis=-1, keepdims=True)
    j2 = jnp.min(jnp.where(rmask2 & (p_in == p2), lane, big), axis=-1, keepdims=True)
    denom = p1 + p2
    w1 = g_gate * (p1 / denom)
    w2 = g_gate * (p2 / denom)
    e1 = j1 - N_GROUPS
    e2 = j2 - N_GROUPS
    hit1 = lane == j1
    hit2 = lane == j2
    hits = jnp.where(hit1 | hit2, 1.0, 0.0)
    before = _dot(tri_ref[...], hits.astype(BF16)) + run_ref[...]
    rank1 = jnp.sum(jnp.where(hit1, before, 0.0), axis=-1, keepdims=True)
    rank2 = jnp.sum(jnp.where(hit2, before, 0.0), axis=-1, keepdims=True)
    run = run_ref[...] + jnp.sum(hits, axis=0, keepdims=True)
    run_ref[...] = run
    cnt_ref[...] = jnp.broadcast_to(run, cnt_ref.shape)
    rt_ref[...] = jnp.where(lane_i == 0, e1, jnp.where(lane_i == 1, e2,
                            jnp.where(lane_i == 2, w1, jnp.where(lane_i == 3, w2,
                                      jnp.where(lane_i == 4, rank1, jnp.where(lane_i == 5, rank2, 0.0))))))


def _mix_xa_router(x3, yr3, g3, ya3, gao, wout, gxa, wq, qn, km3, vm3, wo, gff, wrt, brt):
    B, S, D = x3.shape
    M = km3.shape[1]
    tm = MIX_ROWS
    tile = lambda b, i: (b, i, 0)
    full = lambda b, i: (0, 0)
    memb = lambda b, i: (b, 0, 0)
    W = RWKV_WIDTH
    n_i = S // tm
    tri = jnp.tril(jnp.ones((tm, tm), F32), -1).astype(BF16)
    return pl.pallas_call(
        _mix_xa_router_kernel,
        grid=(B, S // tm),
        in_specs=[pl.BlockSpec((None, tm, D), tile), pl.BlockSpec((None, tm, W), tile),
                  pl.BlockSpec((None, tm, W), tile), pl.BlockSpec((None, tm, W), tile),
                  pl.BlockSpec((1, W), full), pl.BlockSpec((2 * W, D), full),
                  pl.BlockSpec((1, D), full), pl.BlockSpec((D, XA_WIDTH), full),
                  pl.BlockSpec((1, XA_HEAD_DIM), full),
                  pl.BlockSpec((None, M, XA_WIDTH), memb), pl.BlockSpec((None, M, XA_WIDTH), memb),
                  pl.BlockSpec((XA_WIDTH, D), full), pl.BlockSpec((1, D), full),
                  pl.BlockSpec((D, ROUTE_LANES), full), pl.BlockSpec((1, ROUTE_LANES), full),
                  pl.BlockSpec((tm, tm), full)],
        out_specs=[pl.BlockSpec((None, tm, D), tile),
                   pl.BlockSpec((tm * SUBLANES, LANES), lambda b, i: (b * n_i + i, 0)),
                   pl.BlockSpec((None, tm, ROUTE_LANES), tile),
                   pl.BlockSpec((SUBLANES, ROUTE_LANES), full)],
        out_shape=[jax.ShapeDtypeStruct((B, S, D), F32),
                   jax.ShapeDtypeStruct((B * S * SUBLANES, LANES), F32),
                   jax.ShapeDtypeStruct((B, S, ROUTE_LANES), F32),
                   jax.ShapeDtypeStruct((SUBLANES, ROUTE_LANES), F32)],
        scratch_shapes=[pltpu.VMEM((1, ROUTE_LANES), F32)],
        compiler_params=_cparams(("arbitrary", "arbitrary"), 48),
        name="mix_xa_router",
    )(x3, yr3, g3, ya3, gao, wout, gxa, wq, qn, km3, vm3, wo, gff, wrt, brt, tri)


def _row_copy(src_ref, src_row, dst_ref, dst_row, sem):
    src = src_ref.at[pl.ds(pl.multiple_of(src_row * SUBLANES, SUBLANES), SUBLANES)]
    dst = dst_ref.at[pl.ds(pl.multiple_of(dst_row * SUBLANES, SUBLANES), SUBLANES)]
    return pltpu.make_async_copy(src, dst, sem)


def _load_row_tiles(ref, rows):
    return jnp.concatenate([ref[pl.ds(j, rows, stride=SUBLANES), :] for j in range(SUBLANES)], axis=-1)


def _dispatch_kernel(dest_ref, h_ref, zeros_ref, xs_ref, sem):
    del zeros_ref
    tm = h_ref.shape[0] // SUBLANES
    base = pl.program_id(0) * (tm * TOP_K)

    def issue(r, carry):
        for kk in range(TOP_K):
            _row_copy(h_ref, r, xs_ref, dest_ref[base + r * TOP_K + kk], sem).start(priority=kk)
        return carry

    lax.fori_loop(0, tm, issue, 0, unroll=DMA_UNROLL)

    def drain(r, carry):
        for kk in range(TOP_K):
            _row_copy(h_ref, r, xs_ref, dest_ref[base + r * TOP_K + kk], sem).wait()
        return carry

    lax.fori_loop(0, tm, drain, 0, unroll=DMA_UNROLL)


def _dispatch(dest, h_tiles, n_slots):
    tm = MOE_ROWS
    T = h_tiles.shape[0] // SUBLANES
    zeros = jnp.zeros((n_slots * SUBLANES, LANES), F32)
    return pl.pallas_call(
        _dispatch_kernel,
        grid_spec=pltpu.PrefetchScalarGridSpec(
            num_scalar_prefetch=1,
            grid=(T // tm,),
            in_specs=[pl.BlockSpec((tm * SUBLANES, LANES), lambda i, d: (i, 0)),
                      pl.BlockSpec(memory_space=pl.ANY)],
            out_specs=pl.BlockSpec(memory_space=pl.ANY),
            scratch_shapes=[pltpu.SemaphoreType.DMA(())],
        ),
        out_shape=jax.ShapeDtypeStruct((n_slots * SUBLANES, LANES), F32),
        input_output_aliases={2: 0},
        compiler_params=_cparams(("arbitrary",), 32),
        name="moe_dispatch",
    )(dest, h_tiles, zeros)


def _expert_kernel(be_ref, nu_ref, xs_ref, wg_ref, wu_ref, wd_ref, y_ref, wgb_ref, wub_ref, wdb_ref):
    i = pl.program_id(0)
    used = i < nu_ref[0]
    new_expert = (i == 0) | (be_ref[i] != be_ref[jnp.maximum(i - 1, 0)])

    @pl.when(used & new_expert)
    def _():
        wgb_ref[...] = wg_ref[...].astype(BF16)
        wub_ref[...] = wu_ref[...].astype(BF16)
        wdb_ref[...] = wd_ref[...].astype(BF16)

    @pl.when(used)
    def _():
        xb = _load_row_tiles(xs_ref, MOE_BLOCK).astype(BF16)
        gate = _dot(xb, wgb_ref[...])
        up = _dot(xb, wub_ref[...])
        act = (gate * jax.nn.sigmoid(gate) * up).astype(BF16)
        y = _dot(act, wdb_ref[...])
        for j in range(SUBLANES):
            y_ref[pl.ds(j, MOE_BLOCK, stride=SUBLANES), :] = y[:, j * LANES:(j + 1) * LANES]

    @pl.when(jnp.logical_not(used))
    def _():
        y_ref[...] = jnp.zeros_like(y_ref)


def _expert_ffn(block_expert, n_used, xs_tiles, layer, wg, wu, wd):
    D, FF = wg.shape[2], wg.shape[3]
    n_blocks = xs_tiles.shape[0] // (MOE_BLOCK * SUBLANES)
    rows = MOE_BLOCK * SUBLANES
    blk = lambda i, be, nu: (jnp.minimum(i, nu[0] - 1), 0)
    wsel = lambda i, be, nu: (layer, be[jnp.minimum(i, nu[0] - 1)], 0, 0)
    return pl.pallas_call(
        _expert_kernel,
        grid_spec=pltpu.PrefetchScalarGridSpec(
            num_scalar_prefetch=2,
            grid=(n_blocks,),
            in_specs=[pl.BlockSpec((rows, LANES), blk),
                      pl.BlockSpec((None, None, D, FF), wsel),
                      pl.BlockSpec((None, None, D, FF), wsel),
                      pl.BlockSpec((None, None, FF, D), wsel)],
            out_specs=pl.BlockSpec((rows, LANES), lambda i, be, nu: (i, 0)),
            scratch_shapes=[pltpu.VMEM((D, FF), BF16), pltpu.VMEM((D, FF), BF16), pltpu.VMEM((FF, D), BF16)],
        ),
        out_shape=jax.ShapeDtypeStruct(xs_tiles.shape, F32),
        compiler_params=_cparams(("arbitrary",), 48),
        name="moe_experts",
    )(block_expert, n_used, xs_tiles, wg, wu, wd)


def _combine_kernel(dest_ref, x_ref, wt_ref, yb_ref, o_ref, buf_ref, sem):
    tm = x_ref.shape[0]
    base = pl.program_id(0) * (tm * TOP_K)

    def issue(r, carry):
        for kk in range(TOP_K):
            _row_copy(yb_ref, dest_ref[base + r * TOP_K + kk], buf_ref.at[kk], r, sem).start(priority=kk)
        return carry

    lax.fori_loop(0, tm, issue, 0, unroll=DMA_UNROLL)

    def drain(r, carry):
        for kk in range(TOP_K):
            _row_copy(yb_ref, dest_ref[base + r * TOP_K + kk], buf_ref.at[kk], r, sem).wait()
        return carry

    lax.fori_loop(0, tm, drain, 0, unroll=DMA_UNROLL)
    wt = wt_ref[...]
    w1 = wt[:, 2:3]
    w2 = wt[:, 3:4]
    for j in range(SUBLANES):
        cols = slice(j * LANES, (j + 1) * LANES)
        y1 = buf_ref[0, pl.ds(j, tm, stride=SUBLANES), :]
        y2 = buf_ref[1, pl.ds(j, tm, stride=SUBLANES), :]
        o_ref[:, cols] = x_ref[:, cols] + (y1 * w1 + y2 * w2)


def _combine(dest, x2d, rt2d, yb_tiles):
    T, D = x2d.shape
    tm = MOE_ROWS
    return pl.pallas_call(
        _combine_kernel,
        grid_spec=pltpu.PrefetchScalarGridSpec(
            num_scalar_prefetch=1,
            grid=(T // tm,),
            in_specs=[pl.BlockSpec((tm, D), lambda i, d: (i, 0)),
                      pl.BlockSpec((tm, ROUTE_LANES), lambda i, d: (i, 0)),
                      pl.BlockSpec(memory_space=pl.ANY)],
            out_specs=pl.BlockSpec((tm, D), lambda i, d: (i, 0)),
            scratch_shapes=[pltpu.VMEM((TOP_K, tm * SUBLANES, LANES), F32), pltpu.SemaphoreType.DMA(())],
        ),
        out_shape=jax.ShapeDtypeStruct((T, D), F32),
        compiler_params=_cparams(("arbitrary",), 48),
        name="moe_combine",
    )(dest, x2d, rt2d, yb_tiles)


def _moe_plan(rt2d, counts_lanes):
    T = rt2d.shape[0]
    A = T * TOP_K
    e = rt2d[:, 0:TOP_K].astype(jnp.int32).reshape(A)
    rank = rt2d[:, 4:4 + TOP_K].astype(jnp.int32).reshape(A)
    counts = counts_lanes[0, N_GROUPS:N_GROUPS + N_EXPERTS].astype(jnp.int32)
    padded = ((counts + MOE_BLOCK - 1) // MOE_BLOCK) * MOE_BLOCK
    pends = jnp.cumsum(padded)
    pstarts = pends - padded
    onehot = e[:, None] == jnp.arange(N_EXPERTS, dtype=jnp.int32)[None, :]
    dest = jnp.sum(jnp.where(onehot, pstarts[None, :], 0), axis=1) + rank
    n_blocks = (A + N_EXPERTS * (MOE_BLOCK - 1) + MOE_BLOCK - 1) // MOE_BLOCK
    block_start = jnp.arange(n_blocks, dtype=jnp.int32) * MOE_BLOCK
    block_expert = jnp.minimum(jnp.sum(pends[None, :] <= block_start[:, None], axis=1), N_EXPERTS - 1)
    n_used = (pends[-1] // MOE_BLOCK).astype(jnp.int32).reshape(1)
    return dest.astype(jnp.int32), block_expert.astype(jnp.int32), n_used, n_blocks * MOE_BLOCK


def _rel_bias_pairs(table):
    n_ext = BAND + CHUNK - 1
    ext = jnp.concatenate([table, jnp.broadcast_to(table[:, -1:], (table.shape[0], n_ext - table.shape[1]))], axis=1)
    rev = ext[:, ::-1]
    bias = jnp.stack([rev[:, CHUNK - 1 - i:CHUNK - 1 - i + BAND] for i in range(CHUNK)], axis=1)
    return (bias.astype(F32) * LOG2E).reshape(ATT_HEADS // 2, 2 * CHUNK, BAND)


def _pad_rows(w, rows, offset=0):
    out = jnp.zeros((rows, w.shape[1]), w.dtype)
    return out.at[offset:offset + w.shape[0]].set(w)


def kernel(x, mem, norm_mix, w_in, shift_mu, decay_w0, decay_w2, iclr_a0, iclr_a2, gate_g2, rw_k_k, rw_k_a, rw_r_k, lnx_w, lnx_b, vres_v0, vres_v1, vres_v2, att_q_norm, att_k_norm, att_rel_bias, att_out_norm, w_out, norm_xa, norm_mem, xa_w_q, xa_w_kv, xa_w_o, xa_q_norm, xa_k_norm, norm_ffn, moe_w_group, moe_b_group, moe_w_route, moe_b_route, moe_w_gate, moe_w_up, moe_w_down):
    B, S, D = x.shape
    T = B * S
    depth = w_in.shape[0]
    M = mem.shape[1]
    row = lambda p: p.reshape(1, -1)
    ones_bd = jnp.kron(jnp.eye(ATT_HEADS, dtype=F32), jnp.ones((HEAD_DIM, HEAD_DIM), F32)).astype(BF16)

    v_first = None
    for l in range(depth):
        u, qh, kh, vh = _inproj(x.reshape(T, D), row(norm_mix[l]), w_in[l].astype(BF16),
                                row(jnp.tile(att_q_norm[l], ATT_HEADS)),
                                row(jnp.tile(att_k_norm[l], ATT_HEADS)), ones_bd)
        w2p = _pad_rows(decay_w2[l], LANES, 0).astype(BF16)
        a2p = _pad_rows(iclr_a2[l], LANES, DECAY_LORA).astype(BF16)
        vres = None
        if l > 0:
            v1p = jnp.zeros((RWKV_WIDTH, LANES), F32).at[:, :MV_LORA].set(vres_v1[l - 1]).astype(BF16)
            v2p = _pad_rows(vres_v2[l - 1], LANES, 0).astype(BF16)
            vres = (v_first, row(vres_v0[l - 1]), v1p, v2p)
        prep = _rwkv_prep(u.reshape(B, S, RWKV_IN), row(shift_mu[l]), row(decay_w0[l]),
                          w2p, row(iclr_a0[l]), a2p, gate_g2[l].astype(BF16),
                          row(rw_k_k[l]), row(rw_k_a[l]), row(rw_r_k[l]), ones_bd, vres)
        at3, rt3, bt3, kt3, vb3, ge4, bonus3, g3 = prep[:8]
        if l == 0:
            v_first = prep[8]
        yr3 = _wkv_chunked(at3, rt3, bt3, kt3, vb3, ge4, bonus3, row(lnx_w[l]), row(lnx_b[l]))
        pad = ((0, 0), (BAND_PAD, 0), (0, 0))
        ya3 = _band_attn(qh.reshape(B, S, ATT_WIDTH), jnp.pad(kh.reshape(B, S, ATT_WIDTH), pad),
                         jnp.pad(vh.reshape(B, S, ATT_WIDTH), pad), _rel_bias_pairs(att_rel_bias[l]))
        km, vm = _mem_kv(mem.reshape(B * M, D), row(norm_mem[l]), xa_w_kv[l].astype(BF16), row(xa_k_norm[l]))
        wrt = jnp.zeros((D, ROUTE_LANES), F32)
        wrt = wrt.at[:, :N_GROUPS].set(moe_w_group[l]).at[:, N_GROUPS:N_GROUPS + N_EXPERTS].set(moe_w_route[l])
        brt = jnp.zeros((ROUTE_LANES,), F32)
        brt = brt.at[:N_GROUPS].set(moe_b_group[l]).at[N_GROUPS:N_GROUPS + N_EXPERTS].set(moe_b_route[l])
        x3, h_tiles, rt3, counts = _mix_xa_router(
            x, yr3, g3, ya3, row(att_out_norm[l]), w_out[l].astype(BF16), row(norm_xa[l]),
            xa_w_q[l].astype(BF16), row(xa_q_norm[l]), km.reshape(B, M, XA_WIDTH), vm.reshape(B, M, XA_WIDTH),
            xa_w_o[l].astype(BF16), row(norm_ffn[l]), wrt.astype(BF16), row(brt))
        rt2d = rt3.reshape(T, ROUTE_LANES)
        dest, block_expert, n_used, n_slots = _moe_plan(rt2d, counts)
        xs = _dispatch(dest, h_tiles, n_slots)
        yb = _expert_ffn(block_expert, n_used, xs, l, moe_w_gate, moe_w_up, moe_w_down)
        x = _combine(dest, x3.reshape(T, D), rt2d, yb).reshape(B, S, D)
    return x
```

```python
import functools

import jax
import jax.numpy as jnp
from jax import lax
from jax.experimental import pallas as pl
from jax.experimental.pallas import tpu as pltpu

F32 = jnp.float32
BF16 = jnp.bfloat16

CHUNK = 64
RWKV_HEADS = 8
HEAD_DIM = 64
RWKV_WIDTH = 512
ATT_HEADS = 8
ATT_WIDTH = 512
DECAY_LORA = 64
AAA_LORA = 64
MV_LORA = 32
GATE_LORA = 128
RWKV_IN = 3 * RWKV_WIDTH + DECAY_LORA + AAA_LORA + GATE_LORA
LORA_OFF = 3 * RWKV_WIDTH
GATE_OFF = LORA_OFF + DECAY_LORA + AAA_LORA
IN_WIDTH = RWKV_IN + 3 * ATT_WIDTH
PREV_CHUNKS = 8
BAND = (PREV_CHUNKS + 1) * CHUNK
BAND_PAD = PREV_CHUNKS * CHUNK
REL_MAX = 256
XA_HEADS = 4
XA_HEAD_DIM = 128
XA_WIDTH = 512
N_GROUPS = 4
EXPERTS_PER_GROUP = 8
N_EXPERTS = 32
TOP_K = 2
EXPERT_FF = 512
MOE_BLOCK = 256
NORM_EPS = 1e-6
LNX_EPS = 64e-5
NEG_INF = -1e30

LANES = 128
SUBLANES = 8
ROW_TILE = 256
MIX_ROWS = 512
MIX_GROUPS = 2
LOG2E = 1.4426950408889634
WKV_ROWS = 256
WKV_SEQS = 2
NEUMANN_SQUARINGS =CHUNK.bit_length() - 2
ATT_ROWS = 256
MOE_ROWS = 512
ROUTE_LANES = 128
DMA_UNROLL = 8


def _cparams(semantics, vmem_mib):
    return pltpu.CompilerParams(dimension_semantics=semantics,
                                vmem_limit_bytes=vmem_mib * 1024 * 1024)


def _dot(a, b):
    return jnp.dot(a, b, preferred_element_type=F32)


def _dot_nt(a, b):
    return lax.dot_general(a, b, (((1,), (1,)), ((), ())), preferred_element_type=F32)


def _rms(x, g):
    ms = jnp.mean(x * x, axis=-1, keepdims=True)
    return x * lax.rsqrt(ms + NORM_EPS) * g


def _group_sum(x, ones_bd):
    hi = x.astype(BF16)
    lo = (x - hi.astype(F32)).astype(BF16)
    return _dot(hi, ones_bd) + _dot(lo, ones_bd)


def _group_sumsq(x, ones_bd):
    return _group_sum(x * x, ones_bd)


def _inproj_kernel(x_ref, g_ref, w_ref, gq_ref, gk_ref, u_ref, q_ref, k_ref, v_ref):
    h = _rms(x_ref[...], g_ref[...]).astype(BF16)
    for j in range(0, RWKV_IN, 256):
        u_ref[:, j:j + 256] = _dot(h, w_ref[:, j:j + 256])
    even = lax.broadcasted_iota(jnp.int32, (x_ref.shape[0], LANES), 1) < HEAD_DIM

    def head_rms(x, gain_ref, scale):
        for j in range(0, ATT_WIDTH, LANES):
            xs = x[:, j:j + LANES]
            sq = xs * xs
            ss = jnp.where(even, jnp.sum(jnp.where(even, sq, 0.0), axis=-1, keepdims=True),
                           jnp.sum(jnp.where(even, 0.0, sq), axis=-1, keepdims=True))
            yield j, xs * lax.rsqrt(ss * (1.0 / HEAD_DIM) + NORM_EPS) * (gain_ref[:, j:j + LANES] * scale)

    q = _dot(h, w_ref[:, RWKV_IN:RWKV_IN + ATT_WIDTH])
    for j, qn in head_rms(q, gq_ref, HEAD_DIM ** -0.5 * LOG2E):
        q_ref[:, j:j + LANES] = qn.astype(BF16)
    k = _dot(h, w_ref[:, RWKV_IN + ATT_WIDTH:RWKV_IN + 2 * ATT_WIDTH])
    for j, kn in head_rms(k, gk_ref, 1.0):
        k_ref[:, j:j + LANES] = kn.astype(BF16)
    v_ref[...] = _dot(h, w_ref[:, RWKV_IN + 2 * ATT_WIDTH:]).astype(BF16)


def _inproj(x2d, g, w, gq, gk):
    T, D = x2d.shape
    tm = ROW_TILE
    full = lambda i: (0, 0)
    row = lambda i: (i, 0)
    return pl.pallas_call(
        _inproj_kernel,
        grid=(T // tm,),
        in_specs=[pl.BlockSpec((tm, D), row), pl.BlockSpec((1, D), full),
                  pl.BlockSpec((D, IN_WIDTH), full), pl.BlockSpec((1, ATT_WIDTH), full),
                  pl.BlockSpec((1, ATT_WIDTH), full)],
        out_specs=[pl.BlockSpec((tm, RWKV_IN), row), pl.BlockSpec((tm, ATT_WIDTH), row),
                   pl.BlockSpec((tm, ATT_WIDTH), row), pl.BlockSpec((tm, ATT_WIDTH), row)],
        out_shape=[jax.ShapeDtypeStruct((T, RWKV_IN), F32), jax.ShapeDtypeStruct((T, ATT_WIDTH), BF16),
                   jax.ShapeDtypeStruct((T, ATT_WIDTH), BF16), jax.ShapeDtypeStruct((T, ATT_WIDTH), BF16)],
        compiler_params=_cparams(("parallel",), 48),
        name="inproj",
    )(x2d, g, w, gq, gk)


def _rwkv_prep_kernel(has_vres, *refs):
    if has_vres:
        (u_ref, up_ref, mu_ref, w0_ref, w2_ref, a0_ref, a2_ref, g2_ref, kkp_ref, kap_ref, rkp_ref,
         bd_ref, tri_ref, vf_ref, v0_ref, v1_ref, v2_ref,
         at_ref, rt_ref, bt_ref, kt_ref, vb_ref, ge_ref, bonus_ref, g_ref) = refs
    else:
        (u_ref, up_ref, mu_ref, w0_ref, w2_ref, a0_ref, a2_ref, g2_ref, kkp_ref, kap_ref, rkp_ref,
         bd_ref, tri_ref,
         at_ref, rt_ref, bt_ref, kt_ref, vb_ref, ge_ref, bonus_ref, g_ref, v_ref) = refs
    i = pl.program_id(1)
    u = u_ref[...]
    ts = u.shape[0]
    prev_row = jnp.where(i > 0, up_ref[SUBLANES - 1:SUBLANES, :], 0.0)
    rolled = pltpu.roll(u, 1, 0)
    row_id = lax.broadcasted_iota(jnp.int32, u.shape, 0)
    shifted = jnp.where(row_id == 0, jnp.broadcast_to(prev_row, u.shape), rolled)
    u = u + mu_ref[...] * (shifted - u)

    r = u[:, 0:RWKV_WIDTH]
    k = u[:, RWKV_WIDTH:2 * RWKV_WIDTH]
    v = u[:, 2 * RWKV_WIDTH:3 * RWKV_WIDTH]
    lora = u[:, LORA_OFF:GATE_OFF]
    gl = u[:, GATE_OFF:RWKV_IN]

    w = w0_ref[...] + _dot(jnp.tanh(lora).astype(BF16), w2_ref[...])
    z = -w
    softplus = jnp.maximum(z, 0.0) + jnp.log1p(jnp.exp(-jnp.abs(z)))
    w = -softplus - 0.5
    log_decay = -jnp.exp(w)
    a = jax.nn.sigmoid(a0_ref[...] + _dot(lora.astype(BF16), a2_ref[...]))
    g_ref[...] = _dot(jax.nn.sigmoid(gl).astype(BF16), g2_ref[...])
    if has_vres:
        mix = _dot(_dot(v.astype(BF16), v1_ref[...]).astype(BF16), v2_ref[...])
        v = v + (vf_ref[...] - v) * jax.nn.sigmoid(v0_ref[...] + mix)
    else:
        v_ref[...] = v

    bd = bd_ref[...]
    kk = k * kkp_ref[...]
    kk = kk * lax.rsqrt(jnp.maximum(_group_sumsq(kk, bd), 1e-24))
    k2 = k * (1.0 + (a - 1.0) * kap_ref[...])
    bonus_ref[...] = _group_sum(r * k2 * rkp_ref[...], bd) * v

    hi = log_decay.astype(BF16)
    lo = (log_decay - hi.astype(F32)).astype(BF16)
    tri = tri_ref[...]
    cum = _dot(tri, hi) + _dot(tri, lo)
    inv_gamma = jnp.exp(-cum)
    at_ref[...] = (-kk * jnp.exp(cum - log_decay)).astype(BF16)
    rt_ref[...] = (r * jnp.exp(cum)).astype(BF16)
    bt_ref[...] = (kk * a * inv_gamma).astype(BF16)
    kt_ref[...] = (k2 * inv_gamma).astype(BF16)
    vb_ref[...] = v.astype(BF16)
    for c in range(ts // CHUNK):
        ge_ref[c] = jnp.exp(jnp.sum(log_decay[c * CHUNK:(c + 1) * CHUNK, :], axis=0, keepdims=True))


def _rwkv_prep(u3, mu, w0, w2p, a0, a2p, g2, kkp, kap, rkp, ones_bd, vres):
    B, S, _ = u3.shape
    ts = ROW_TILE
    W = RWKV_WIDTH
    n_c = ts // CHUNK
    full2 = lambda b, i: (0, 0)
    tile = lambda b, i: (b, i, 0)
    prev = lambda b, i: (b, jnp.maximum(i * (ts // SUBLANES) - 1, 0), 0)
    tri = (jnp.tril(jnp.ones((ts, ts), F32))
           * jnp.kron(jnp.eye(n_c, dtype=F32), jnp.ones((CHUNK, CHUNK), F32))).astype(BF16)
    in_specs = [pl.BlockSpec((None, ts, RWKV_IN), tile),
                pl.BlockSpec((None, SUBLANES, RWKV_IN), prev),
                pl.BlockSpec((1, RWKV_IN), full2), pl.BlockSpec((1, W), full2),
                pl.BlockSpec((LANES, W), full2), pl.BlockSpec((1, W), full2),
                pl.BlockSpec((LANES, W), full2), pl.BlockSpec((GATE_LORA, W), full2),
                pl.BlockSpec((1, W), full2), pl.BlockSpec((1, W), full2), pl.BlockSpec((1, W), full2),
                pl.BlockSpec((W, W), full2), pl.BlockSpec((ts, ts), full2)]
    args = [u3, u3, mu, w0, w2p, a0, a2p, g2, kkp, kap, rkp, ones_bd, tri]
    if vres is not None:
        v_first, v0, v1p, v2p = vres
        in_specs += [pl.BlockSpec((None, ts, W), tile), pl.BlockSpec((1, W), full2),
                     pl.BlockSpec((W, LANES), full2), pl.BlockSpec((LANES, W), full2)]
        args += [v_first, v0, v1p, v2p]
    tok = pl.BlockSpec((None, ts, W), tile)
    out_specs = [tok] * 5 + [pl.BlockSpec((None, n_c, 1, W), lambda b, i: (b, i, 0, 0)), tok, tok]
    out_shape = ([jax.ShapeDtypeStruct((B, S, W), BF16)] * 5
                 + [jax.ShapeDtypeStruct((B, S // CHUNK, 1, W), F32)]
                 + [jax.ShapeDtypeStruct((B, S, W), F32)] * 2)
    if vres is None:
        out_specs.append(tok)
        out_shape.append(jax.ShapeDtypeStruct((B, S, W), F32))
    return pl.pallas_call(
        functools.partial(_rwkv_prep_kernel, vres is not None),
        grid=(B, S // ts),
        in_specs=in_specs,
        out_specs=out_specs,
        out_shape=out_shape,
        compiler_params=_cparams(("parallel", "parallel"), 48),
        name="rwkv_prep",
    )(*args)


def _wkv_chunk_kernel(at_ref, rt_ref, bt_ref, kt_ref, vb_ref, ge_ref, bonus_ref, lw_ref, lb_ref,
                      y_ref, n_ref):
    C = CHUNK
    n_pairs = RWKV_HEADS // 2
    n_seqs = at_ref.shape[0]
    n_chunks = at_ref.shape[1] // C

    @pl.when(pl.program_id(1) == 0)
    def _():
        n_ref[...] = jnp.zeros_like(n_ref)

    row = lax.broadcasted_iota(jnp.int32, (2 * C, LANES), 0)
    lane = lax.broadcasted_iota(jnp.int32, (2 * C, LANES), 1)
    top = row < C
    left = lane < HEAD_DIM
    same = top == left
    t_row = jnp.where(top, row, row - C)
    s_col = jnp.where(left, lane, lane - HEAD_DIM)
    strict = s_col < t_row
    incl = s_col <= t_row
    even = lax.broadcasted_iota(jnp.int32, (C, LANES), 1) < HEAD_DIM
    zeros_c = jnp.zeros((C, LANES), BF16)

    units = [(c, b, p) for c in range(n_chunks) for b in range(n_seqs) for p in range(n_pairs)]

    def tile(ref, unit):
        c, b, p = unit
        return ref[b, c * C:(c + 1) * C, p * LANES:(p + 1) * LANES]

    def stack_heads(x):
        z = jnp.zeros_like(x)
        return jnp.concatenate([jnp.where(even, x, z), jnp.where(even, z, x)], axis=0)

    def half_sum(x):
        lo = jnp.sum(jnp.where(even, x, 0.0), axis=-1, keepdims=True)
        hi = jnp.sum(jnp.where(even, 0.0, x), axis=-1, keepdims=True)
        return jnp.where(even, lo, hi)

    at2 = [stack_heads(tile(at_ref, un)) for un in units]
    bk = [jnp.concatenate([tile(bt_ref, un), tile(kt_ref, un)], axis=0) for un in units]
    ga = [jnp.where(strict, _dot_nt(a, b), 0.0) for a, b in zip(at2, bk)]
    gr = [jnp.where(incl, _dot_nt(stack_heads(tile(rt_ref, un)), b), 0.0).astype(BF16)
          for un, b in zip(units, bk)]
    zs = [jnp.where(same, _dot(g.astype(BF16), jnp.concatenate([zeros_c, tile(vb_ref, un)], axis=0)), 0.0)
          for un, g in zip(units, ga)]
    pw = [jnp.where(same, jnp.where(top, g, pltpu.roll(g, HEAD_DIM, 1)), 0.0).astype(BF16) for g in ga]
    w = [jnp.concatenate([a.astype(F32), z], axis=1) for a, z in zip(at2, zs)]
    w = [x + _dot(m, x.astype(BF16)) for m, x in zip(pw, w)]
    for _ in range(NEUMANN_SQUARINGS):
        pw = [_dot(m, m).astype(BF16) for m in pw]
        w = [x + _dot(m, x.astype(BF16)) for m, x in zip(pw, w)]
    pm = [(x[:C, :LANES] + x[C:, :LANES]).astype(BF16) for x in w]
    q = [x[:C, LANES:] + x[C:, LANES:] for x in w]

    for i, un in enumerate(units):
        c, b, p = un
        cols = slice(p * LANES, (p + 1) * LANES)
        rows = slice(c * C, (c + 1) * C)
        n0 = n_ref[b, p]
        pr = _dot_nt(jnp.concatenate([pm[i], tile(rt_ref, un)], axis=0), n0.astype(BF16))
        u = pr[:C] + q[i]
        uv = jnp.concatenate([u.astype(BF16), tile(vb_ref, un)], axis=0)
        yp = _dot(gr[i], uv)
        y = pr[C:] + jnp.where(even, yp[:C], yp[C:])
        g_slab = ge_ref[b, c, :, cols]
        bkg = (bk[i].astype(F32) * g_slab).astype(BF16)
        dn = lax.dot_general(uv, bkg, (((0,), (0,)), ((), ())), preferred_element_type=F32)
        n_ref[b, p] = g_slab * n0 + jnp.where(same, dn, 0.0)
        mean = half_sum(y) * (1.0 / HEAD_DIM)
        yc = y - mean
        var = half_sum(yc * yc) * (1.0 / HEAD_DIM)
        yn = yc * lax.rsqrt(var + LNX_EPS) * lw_ref[:, cols] + lb_ref[:, cols]
        y_ref[b, rows, cols] = yn + bonus_ref[b, rows, cols]


def _wkv_chunked(at, rt, bt, kt, vb, ge, bonus, lnw, lnb):
    B, S, W = at.shape
    ts = WKV_ROWS
    nb = WKV_SEQS if B % WKV_SEQS == 0 else 1
    n_c = ts // CHUNK
    tile = lambda b, i: (b, i, 0)
    full = lambda b, i: (0, 0)
    tok = pl.BlockSpec((nb, ts, W), tile)
    return pl.pallas_call(
        _wkv_chunk_kernel,
        grid=(B // nb, S // ts),
        in_specs=[tok] * 5 + [pl.BlockSpec((nb, n_c, 1, W), lambda b, i: (b, i, 0, 0)), tok,
                              pl.BlockSpec((1, W), full), pl.BlockSpec((1, W), full)],
        out_specs=tok,
        out_shape=jax.ShapeDtypeStruct((B, S, W), F32),
        scratch_shapes=[pltpu.VMEM((nb, RWKV_HEADS // 2, 2 * HEAD_DIM, LANES), F32)],
        compiler_params=_cparams(("parallel", "arbitrary"), 32),
        name="wkv_chunked",
    )(at, rt, bt, kt, vb, ge, bonus, lnw, lnb)


def _band_attn_kernel(q_ref, k_ref, v_ref, bias_ref, o_ref, s_ref, e_ref):
    i = pl.program_id(1)
    n_chunks = q_ref.shape[0] // CHUNK
    n_pairs = ATT_HEADS // 2
    lane = lax.broadcasted_iota(jnp.int32, (CHUNK, LANES), 1)
    even = lane < HEAD_DIM
    slot = lax.broadcasted_iota(jnp.int32, (1, 1, BAND), 2)

    def chunk_body(masked, c, carry):
        cg = i * n_chunks + c
        start = pl.multiple_of(cg * CHUNK, CHUNK)
        row0 = pl.multiple_of(c * CHUNK, CHUNK)
        for p in range(n_pairs):
            cols = slice(p * LANES, (p + 1) * LANES)
            q2 = q_ref[pl.ds(row0, CHUNK), cols]
            zero = jnp.zeros_like(q2)
            qs = jnp.concatenate([jnp.where(even, q2, zero), jnp.where(even, zero, q2)], axis=0)
            s_ref[p] = _dot_nt(qs, k_ref[pl.ds(start, BAND), cols]) + bias_ref[p]
        s = s_ref[...]
        if masked:
            s = jnp.where(slot >= (BAND_PAD - cg * CHUNK), s, NEG_INF)
        e = jnp.exp2(s - jnp.max(s, axis=-1, keepdims=True))
        inv = 1.0 / jnp.sum(e, axis=-1, keepdims=True)
        e_ref[...] = e.astype(BF16)
        for p in range(n_pairs):
            cols = slice(p * LANES, (p + 1) * LANES)
            o = _dot(e_ref[p], v_ref[pl.ds(start, BAND), cols]) * inv[p]
            o_ref[pl.ds(row0, CHUNK), cols] = jnp.where(even, o[:CHUNK], o[CHUNK:])
        return carry

    @pl.when(i * n_chunks < PREV_CHUNKS)
    def _():
        lax.fori_loop(0, n_chunks, functools.partial(chunk_body, True), 0)

    @pl.when(i * n_chunks >= PREV_CHUNKS)
    def _():
        lax.fori_loop(0, n_chunks, functools.partial(chunk_body, False), 0)


def _band_attn(q3, kp3, vp3, bias_pairs):
    B, S, W = q3.shape
    SP = kp3.shape[1]
    tq = ATT_ROWS
    assert PREV_CHUNKS % (tq // CHUNK) == 0
    n_pairs = ATT_HEADS // 2
    return pl.pallas_call(
        _band_attn_kernel,
        grid=(B, S // tq),
        in_specs=[pl.BlockSpec((None, tq, W), lambda b, i: (b, i, 0)),
                  pl.BlockSpec((None, SP, W), lambda b, i: (b, 0, 0)),
                  pl.BlockSpec((None, SP, W), lambda b, i: (b, 0, 0)),
                  pl.BlockSpec((ATT_HEADS // 2, 2 * CHUNK, BAND), lambda b, i: (0, 0, 0))],
        out_specs=pl.BlockSpec((None, tq, W), lambda b, i: (b, i, 0)),
        out_shape=jax.ShapeDtypeStruct((B, S, W), F32),
        scratch_shapes=[pltpu.VMEM((n_pairs, 2 * CHUNK, BAND), F32),
                        pltpu.VMEM((n_pairs, 2 * CHUNK, BAND), BF16)],
        compiler_params=_cparams(("parallel", "arbitrary"), 48),
        name="band_attn",
    )(q3, kp3, vp3, bias_pairs)


def _mem_kv_kernel(m_ref, g_ref, w_ref, kn_ref, k_ref, v_ref):
    h = _rms(m_ref[...], g_ref[...]).astype(BF16)
    kv = _dot(h, w_ref[...])
    for hd in range(XA_HEADS):
        cols = slice(hd * XA_HEAD_DIM, (hd + 1) * XA_HEAD_DIM)
        k_ref[:, cols] = _rms(kv[:, cols], kn_ref[...]).astype(BF16)
    v_ref[...] = kv[:, XA_WIDTH:].astype(BF16)


def _mem_kv(mem2d, g, w_kv, k_norm):
    R, D = mem2d.shape
    tm = ROW_TILE
    full = lambda i: (0, 0)
    row = lambda i: (i, 0)
    return pl.pallas_call(
        _mem_kv_kernel,
        grid=(R // tm,),
        in_specs=[pl.BlockSpec((tm, D), row), pl.BlockSpec((1, D), full),
                  pl.BlockSpec((D, 2 * XA_WIDTH), full), pl.BlockSpec((1, XA_HEAD_DIM), full)],
        out_specs=[pl.BlockSpec((tm, XA_WIDTH), row)] * 2,
        out_shape=[jax.ShapeDtypeStruct((R, XA_WIDTH), BF16)] * 2,
        compiler_params=_cparams(("parallel",), 32),
        name="mem_kv",
    )(mem2d, g, w_kv, k_norm)


def _mix_xa_router_kernel(x_ref, yr_ref, g_ref, ya_ref, gao_ref, wout_ref, gxa_ref, wq_ref, qn_ref,
                          km_ref, vm_ref, wo_ref, gff_ref, wrt_ref, brt_ref,
                          tri_ref, xo_ref, h_ref, rt_ref, cnt_ref, run_ref, stage_ref):
    first = (pl.program_id(0) == 0) & (pl.program_id(1) == 0)

    @pl.when(first)
    def _():
        run_ref[...] = jnp.zeros_like(run_ref)

    n_rows = x_ref.shape[0]
    groups = [slice(r, r + n_rows // MIX_GROUPS) for r in range(0, n_rows, n_rows // MIX_GROUPS)]
    head_cols = [slice(hd * XA_HEAD_DIM, (hd + 1) * XA_HEAD_DIM) for hd in range(XA_HEADS)]

    yr = [(yr_ref[rows, :] * g_ref[rows, :]).astype(BF16) for rows in groups]
    ya = [_rms(ya_ref[rows, :], gao_ref[...]).astype(BF16) for rows in groups]
    x1 = [x_ref[rows, :] + _dot(a, wout_ref[0:RWKV_WIDTH, :]) + _dot(b, wout_ref[RWKV_WIDTH:, :])
          for rows, a, b in zip(groups, yr, ya)]

    q = [_dot(_rms(xg, gxa_ref[...]).astype(BF16), wq_ref[...]) for xg in x1]
    qh = [[_rms(qg[:, cols], qn_ref[...]).astype(BF16) for cols in head_cols] for qg in q]
    s = [[_dot_nt(qg[hd], km_ref[:, cols]) * (XA_HEAD_DIM ** -0.5) for hd, cols in enumerate(head_cols)]
         for qg in qh]
    e = [[jnp.exp(sh - jnp.max(sh, axis=-1, keepdims=True)) for sh in sg] for sg in s]
    pr = [[(eh / jnp.sum(eh, axis=-1, keepdims=True)).astype(BF16) for eh in eg] for eg in e]
    o = [jnp.concatenate([_dot(pg[hd], vm_ref[:, cols]) for hd, cols in enumerate(head_cols)],
                         axis=-1).astype(BF16) for pg in pr]
    x = jnp.concatenate([xg + _dot(og, wo_ref[...]) for xg, og in zip(x1, o)], axis=0)
    xo_ref[...] = x

    h = _rms(x, gff_ref[...])
    tm = h.shape[0]
    _store_row_tiles(stage_ref, h)
    h_ref[...] = stage_ref[...].astype(BF16)
    logits =_dot(h.astype(BF16), wrt_ref[...]) + brt_ref[...]
    lane_i = lax.broadcasted_iota(jnp.int32, logits.shape, 1)
    lane = lane_i.astype(F32)
    big = float(ROUTE_LANES)
    gmask = lane < N_GROUPS
    gmax = jnp.max(jnp.where(gmask, logits, -jnp.inf), axis=-1, keepdims=True)
    ge = jnp.where(gmask, jnp.exp(logits - gmax), 0.0)
    gp = ge / jnp.sum(ge, axis=-1, keepdims=True)
    g_gate = jnp.max(gp, axis=-1, keepdims=True)
    g_top = jnp.min(jnp.where(gmask & (gp == g_gate), lane, big), axis=-1, keepdims=True)
    lo = N_GROUPS + g_top * EXPERTS_PER_GROUP
    rmask = (lane >= lo) & (lane < lo + EXPERTS_PER_GROUP)
    rmax = jnp.max(jnp.where(rmask, logits, -jnp.inf), axis=-1, keepdims=True)
    re = jnp.where(rmask, jnp.exp(logits - rmax), 0.0)
    p_in = re / jnp.sum(re, axis=-1, keepdims=True)
    p1 = jnp.max(jnp.where(rmask, p_in, -1.0), axis=-1, keepdims=True)
    j1 = jnp.min(jnp.where(rmask & (p_in == p1), lane, big), axis=-1, keepdims=True)
    rmask2 = rmask & (lane != j1)
    p2 = jnp.max(jnp.where(rmask2, p_in, -1.0), axis=-1, keepdims=True)
    j2 = jnp.min(jnp.where(rmask2 & (p_in == p2), lane, big), axis=-1, keepdims=True)
    denom = p1 + p2
    w1 = g_gate * (p1 / denom)
    w2 = g_gate * (p2 / denom)
    e1 = j1 - N_GROUPS
    e2 = j2 - N_GROUPS
    hit1 = lane == j1
    hit2 = lane == j2
    hits = jnp.where(hit1 | hit2, 1.0, 0.0)
    before = _dot(tri_ref[...], hits.astype(BF16)) + run_ref[...]
    rank1 = jnp.sum(jnp.where(hit1, before, 0.0), axis=-1, keepdims=True)
    rank2 = jnp.sum(jnp.where(hit2, before, 0.0), axis=-1, keepdims=True)
    run = run_ref[...] + jnp.sum(hits, axis=0, keepdims=True)
    run_ref[...] = run
    cnt_ref[...] = jnp.broadcast_to(run, cnt_ref.shape)
    rt_ref[...] = jnp.where(lane_i == 0, e1, jnp.where(lane_i == 1, e2,
                            jnp.where(lane_i == 2, w1, jnp.where(lane_i == 3, w2,
                                      jnp.where(lane_i == 4, rank1, jnp.where(lane_i == 5, rank2, 0.0))))))


def _mix_xa_router(x3, yr3, g3, ya3, gao, wout, gxa, wq, qn, km3, vm3, wo, gff, wrt, brt):
    B, S, D = x3.shape
    M = km3.shape[1]
    tm = MIX_ROWS
    tile = lambda b, i: (b, i, 0)
    full = lambda b, i: (0, 0)
    memb = lambda b, i: (b, 0, 0)
    W = RWKV_WIDTH
    n_i = S // tm
    tri = jnp.tril(jnp.ones((tm, tm), F32), -1).astype(BF16)
    return pl.pallas_call(
        _mix_xa_router_kernel,
        grid=(B, S // tm),
        in_specs=[pl.BlockSpec((None, tm, D), tile), pl.BlockSpec((None, tm, W), tile),
                  pl.BlockSpec((None, tm, W), tile), pl.BlockSpec((None, tm, W), tile),
                  pl.BlockSpec((1, W), full), pl.BlockSpec((2 * W, D), full),
                  pl.BlockSpec((1, D), full), pl.BlockSpec((D, XA_WIDTH), full),
                  pl.BlockSpec((1, XA_HEAD_DIM), full),
                  pl.BlockSpec((None, M, XA_WIDTH), memb), pl.BlockSpec((None, M, XA_WIDTH), memb),
                  pl.BlockSpec((XA_WIDTH, D), full), pl.BlockSpec((1, D), full),
                  pl.BlockSpec((D, ROUTE_LANES), full), pl.BlockSpec((1, ROUTE_LANES), full),
                  pl.BlockSpec((tm, tm), full)],
        out_specs=[pl.BlockSpec((None, tm, D), tile),
                   pl.BlockSpec((tm * SUBLANES, LANES), lambda b, i: (b * n_i + i, 0)),
                   pl.BlockSpec((None, tm, ROUTE_LANES), tile),
                   pl.BlockSpec((SUBLANES, ROUTE_LANES), full)],
        out_shape=[jax.ShapeDtypeStruct((B, S, D), F32),
                   jax.ShapeDtypeStruct((B * S * SUBLANES, LANES), BF16),
                   jax.ShapeDtypeStruct((B, S, ROUTE_LANES), F32),
                   jax.ShapeDtypeStruct((SUBLANES, ROUTE_LANES), F32)],
        scratch_shapes=[pltpu.VMEM((1, ROUTE_LANES), F32), pltpu.VMEM((tm * SUBLANES, LANES), F32)],
        compiler_params=_cparams(("arbitrary", "arbitrary"), 48),
        name="mix_xa_router",
    )(x3, yr3, g3, ya3, gao, wout, gxa, wq, qn, km3, vm3, wo, gff, wrt, brt, tri)


def _row_copy(src_ref, src_row, dst_ref, dst_row, sem):
    src = src_ref.at[pl.ds(pl.multiple_of(src_row * SUBLANES, SUBLANES), SUBLANES)]
    dst = dst_ref.at[pl.ds(pl.multiple_of(dst_row * SUBLANES, SUBLANES), SUBLANES)]
    return pltpu.make_async_copy(src, dst, sem)


def _load_row_tiles(stage_ref, rows):
    return jnp.concatenate([stage_ref[pl.ds(j, rows, stride=SUBLANES), :] for j in range(SUBLANES)], axis=-1)


def _store_row_tiles(stage_ref, x):
    for j in range(SUBLANES):
        stage_ref[pl.ds(j, x.shape[0], stride=SUBLANES), :] = x[:, j * LANES:(j + 1) * LANES]


def _dispatch_kernel(dest_ref, h_ref, zeros_ref, xs_ref, sem):
    del zeros_ref
    tm = h_ref.shape[0] // SUBLANES
    base = pl.program_id(0) * (tm * TOP_K)

    def issue(r, carry):
        for kk in range(TOP_K):
            _row_copy(h_ref, r, xs_ref, dest_ref[base + r * TOP_K + kk], sem).start(priority=kk)
        return carry

    lax.fori_loop(0, tm, issue, 0, unroll=DMA_UNROLL)

    def drain(r, carry):
        for kk in range(TOP_K):
            _row_copy(h_ref, r, xs_ref, dest_ref[base + r * TOP_K + kk], sem).wait()
        return carry

    lax.fori_loop(0, tm, drain, 0, unroll=DMA_UNROLL)


def _dispatch(dest, h_tiles, n_slots):
    tm = MOE_ROWS
    T = h_tiles.shape[0] // SUBLANES
    zeros = jnp.zeros((n_slots * SUBLANES, LANES), BF16)
    return pl.pallas_call(
        _dispatch_kernel,
        grid_spec=pltpu.PrefetchScalarGridSpec(
            num_scalar_prefetch=1,
            grid=(T // tm,),
            in_specs=[pl.BlockSpec((tm * SUBLANES, LANES), lambda i, d: (i, 0)),
                      pl.BlockSpec(memory_space=pl.ANY)],
            out_specs=pl.BlockSpec(memory_space=pl.ANY),
            scratch_shapes=[pltpu.SemaphoreType.DMA(())],
        ),
        out_shape=jax.ShapeDtypeStruct((n_slots * SUBLANES, LANES), BF16),
        input_output_aliases={2: 0},
        compiler_params=_cparams(("arbitrary",), 32),
        name="moe_dispatch",
    )(dest, h_tiles, zeros)


def _expert_kernel(be_ref, nu_ref, xs_ref, wg_ref, wu_ref, wd_ref, y_ref, wgb_ref, wub_ref, wdb_ref,
                   stage_ref):
    i = pl.program_id(0)
    used = i < nu_ref[0]
    new_expert = (i == 0) | (be_ref[i] != be_ref[jnp.maximum(i - 1, 0)])

    @pl.when(used & new_expert)
    def _():
        wgb_ref[...] = wg_ref[...].astype(BF16)
        wub_ref[...] = wu_ref[...].astype(BF16)
        wdb_ref[...] = wd_ref[...].astype(BF16)

    @pl.when(used)
    def _():
        stage_ref[...] = xs_ref[...].astype(F32)
        xb = _load_row_tiles(stage_ref, MOE_BLOCK).astype(BF16)
        gate = _dot(xb, wgb_ref[...])
        up = _dot(xb, wub_ref[...])
        act = (gate * jax.nn.sigmoid(gate) * up).astype(BF16)
        _store_row_tiles(stage_ref, _dot(act, wdb_ref[...]))
        y_ref[...] = stage_ref[...].astype(BF16)

    @pl.when(jnp.logical_not(used))
    def _():
        y_ref[...] = jnp.zeros_like(y_ref)


def _expert_ffn(block_expert, n_used, xs_tiles, layer, wg, wu, wd):
    D, FF = wg.shape[2], wg.shape[3]
    n_blocks = xs_tiles.shape[0] // (MOE_BLOCK * SUBLANES)
    rows = MOE_BLOCK * SUBLANES
    blk = lambda i, be, nu: (jnp.minimum(i, nu[0] - 1), 0)
    wsel = lambda i, be, nu: (layer, be[jnp.minimum(i, nu[0] - 1)], 0, 0)
    return pl.pallas_call(
        _expert_kernel,
        grid_spec=pltpu.PrefetchScalarGridSpec(
            num_scalar_prefetch=2,
            grid=(n_blocks,),
            in_specs=[pl.BlockSpec((rows, LANES), blk),
                      pl.BlockSpec((None, None, D, FF), wsel),
                      pl.BlockSpec((None, None, D, FF), wsel),
                      pl.BlockSpec((None, None, FF, D), wsel)],
            out_specs=pl.BlockSpec((rows, LANES), lambda i, be, nu: (i, 0)),
            scratch_shapes=[pltpu.VMEM((D, FF), BF16), pltpu.VMEM((D, FF), BF16), pltpu.VMEM((FF, D), BF16),
                            pltpu.VMEM((rows, LANES), F32)],
        ),
        out_shape=jax.ShapeDtypeStruct(xs_tiles.shape, BF16),
        compiler_params=_cparams(("arbitrary",), 48),
        name="moe_experts",
    )(block_expert, n_used, xs_tiles, wg, wu, wd)


def _combine_kernel(dest_ref, x_ref, wt_ref, yb_ref, o_ref, buf_ref, sem, stage_ref):
    tm = x_ref.shape[0]
    base = pl.program_id(0) * (tm * TOP_K)

    def issue(r, carry):
        for kk in range(TOP_K):
            _row_copy(yb_ref, dest_ref[base + r * TOP_K + kk], buf_ref.at[kk], r, sem).start(priority=kk)
        return carry

    lax.fori_loop(0, tm, issue, 0, unroll=DMA_UNROLL)

    def drain(r, carry):
        for kk in range(TOP_K):
            _row_copy(yb_ref, dest_ref[base + r * TOP_K + kk], buf_ref.at[kk], r, sem).wait()
        return carry

    lax.fori_loop(0, tm, drain, 0, unroll=DMA_UNROLL)
    wt = wt_ref[...]
    w1 = wt[:, 2:3]
    w2 = wt[:, 3:4]
    stage_ref[...] = buf_ref[...].astype(F32)
    for j in range(SUBLANES):
        cols = slice(j * LANES, (j + 1) * LANES)
        y1 = stage_ref[0, pl.ds(j, tm, stride=SUBLANES), :]
        y2 = stage_ref[1, pl.ds(j, tm, stride=SUBLANES), :]
        o_ref[:, cols] = x_ref[:, cols] + (y1 * w1 + y2 * w2)


def _combine(dest, x2d, rt2d, yb_tiles):
    T, D = x2d.shape
    tm = MOE_ROWS
    return pl.pallas_call(
        _combine_kernel,
        grid_spec=pltpu.PrefetchScalarGridSpec(
            num_scalar_prefetch=1,
            grid=(T // tm,),
            in_specs=[pl.BlockSpec((tm, D), lambda i, d: (i, 0)),
                      pl.BlockSpec((tm, ROUTE_LANES), lambda i, d: (i, 0)),
                      pl.BlockSpec(memory_space=pl.ANY)],
            out_specs=pl.BlockSpec((tm, D), lambda i, d: (i, 0)),
            scratch_shapes=[pltpu.VMEM((TOP_K, tm * SUBLANES, LANES), BF16), pltpu.SemaphoreType.DMA(()),
                            pltpu.VMEM((TOP_K, tm * SUBLANES, LANES), F32)],
        ),
        out_shape=jax.ShapeDtypeStruct((T, D), F32),
        compiler_params=_cparams(("arbitrary",), 48),
        name="moe_combine",
    )(dest, x2d, rt2d, yb_tiles)


def _moe_plan(rt2d, counts_lanes):
    T = rt2d.shape[0]
    A = T * TOP_K
    e = rt2d[:, 0:TOP_K].astype(jnp.int32).reshape(A)
    rank = rt2d[:, 4:4 + TOP_K].astype(jnp.int32).reshape(A)
    counts = counts_lanes[0, N_GROUPS:N_GROUPS + N_EXPERTS].astype(jnp.int32)
    padded = ((counts + MOE_BLOCK - 1) // MOE_BLOCK) * MOE_BLOCK
    pends = jnp.cumsum(padded)
    pstarts = pends - padded
    onehot = e[:, None] == jnp.arange(N_EXPERTS, dtype=jnp.int32)[None, :]
    dest = jnp.sum(jnp.where(onehot, pstarts[None, :], 0), axis=1) + rank
    n_blocks = (A + N_EXPERTS * (MOE_BLOCK - 1) + MOE_BLOCK - 1) // MOE_BLOCK
    block_start = jnp.arange(n_blocks, dtype=jnp.int32) * MOE_BLOCK
    block_expert = jnp.minimum(jnp.sum(pends[None, :] <= block_start[:, None], axis=1), N_EXPERTS - 1)
    n_used = (pends[-1] // MOE_BLOCK).astype(jnp.int32).reshape(1)
    return dest.astype(jnp.int32), block_expert.astype(jnp.int32), n_used, n_blocks * MOE_BLOCK


def _rel_bias_pairs(table):
    n_ext = BAND + CHUNK - 1
    ext = jnp.concatenate([table, jnp.broadcast_to(table[:, -1:], (table.shape[0], n_ext - table.shape[1]))], axis=1)
    rev = ext[:, ::-1]
    bias = jnp.stack([rev[:, CHUNK - 1 - i:CHUNK - 1 - i + BAND] for i in range(CHUNK)], axis=1)
    return (bias.astype(F32) * LOG2E).reshape(-1, ATT_HEADS // 2, 2 * CHUNK, BAND)


def _pad_rows(w, rows, offset=0):
    out = jnp.zeros((rows, w.shape[1]), w.dtype)
    return out.at[offset:offset + w.shape[0]].set(w)


def kernel(x, mem, norm_mix, w_in, shift_mu, decay_w0, decay_w2, iclr_a0, iclr_a2, gate_g2, rw_k_k, rw_k_a, rw_r_k, lnx_w, lnx_b, vres_v0, vres_v1, vres_v2, att_q_norm, att_k_norm, att_rel_bias, att_out_norm, w_out, norm_xa, norm_mem, xa_w_q, xa_w_kv, xa_w_o, xa_q_norm, xa_k_norm, norm_ffn, moe_w_group, moe_b_group, moe_w_route, moe_b_route, moe_w_gate, moe_w_up, moe_w_down):
    B, S, D = x.shape
    T = B * S
    depth = w_in.shape[0]
    M = mem.shape[1]
    row = lambda p: p.reshape(1, -1)
    ones_bd = jnp.kron(jnp.eye(ATT_HEADS, dtype=F32), jnp.ones((HEAD_DIM, HEAD_DIM), F32)).astype(BF16)

    bias_pairs = _rel_bias_pairs(att_rel_bias.reshape(depth * ATT_HEADS, -1))

    v_first = None
    for l in range(depth):
        u, qh, kh, vh = _inproj(x.reshape(T, D), row(norm_mix[l]), w_in[l].astype(BF16),
                                row(jnp.tile(att_q_norm[l], ATT_HEADS)),
                                row(jnp.tile(att_k_norm[l], ATT_HEADS)))
        w2p = _pad_rows(decay_w2[l], LANES, 0).astype(BF16)
        a2p = _pad_rows(iclr_a2[l], LANES, DECAY_LORA).astype(BF16)
        vres = None
        if l > 0:
            v1p = jnp.zeros((RWKV_WIDTH, LANES), F32).at[:, :MV_LORA].set(vres_v1[l - 1]).astype(BF16)
            v2p = _pad_rows(vres_v2[l - 1], LANES, 0).astype(BF16)
            vres = (v_first, row(vres_v0[l - 1]), v1p, v2p)
        prep = _rwkv_prep(u.reshape(B, S, RWKV_IN), row(shift_mu[l]), row(decay_w0[l]),
                          w2p, row(iclr_a0[l]), a2p, gate_g2[l].astype(BF16),
                          row(rw_k_k[l]), row(rw_k_a[l]), row(rw_r_k[l]), ones_bd, vres)
        at3, rt3, bt3, kt3, vb3, ge4, bonus3, g3 = prep[:8]
        if l == 0:
            v_first = prep[8]
        yr3 = _wkv_chunked(at3, rt3, bt3, kt3, vb3, ge4, bonus3, row(lnx_w[l]), row(lnx_b[l]))
        pad = ((0, 0), (BAND_PAD, 0), (0, 0))
        ya3 = _band_attn(qh.reshape(B, S, ATT_WIDTH), jnp.pad(kh.reshape(B, S, ATT_WIDTH), pad),
                         jnp.pad(vh.reshape(B, S, ATT_WIDTH), pad), bias_pairs[l])
        km, vm = _mem_kv(mem.reshape(B * M, D), row(norm_mem[l]), xa_w_kv[l].astype(BF16), row(xa_k_norm[l]))
        wrt = jnp.zeros((D, ROUTE_LANES), F32)
        wrt = wrt.at[:, :N_GROUPS].set(moe_w_group[l]).at[:, N_GROUPS:N_GROUPS + N_EXPERTS].set(moe_w_route[l])
        brt = jnp.zeros((ROUTE_LANES,), F32)
        brt = brt.at[:N_GROUPS].set(moe_b_group[l]).at[N_GROUPS:N_GROUPS + N_EXPERTS].set(moe_b_route[l])
        x3, h_tiles, rt3, counts = _mix_xa_router(
            x, yr3, g3, ya3, row(att_out_norm[l]), w_out[l].astype(BF16), row(norm_xa[l]),
            xa_w_q[l].astype(BF16), row(xa_q_norm[l]), km.reshape(B, M, XA_WIDTH), vm.reshape(B, M, XA_WIDTH),
            xa_w_o[l].astype(BF16), row(norm_ffn[l]), wrt.astype(BF16), row(brt))
        rt2d = rt3.reshape(T, ROUTE_LANES)
        dest, block_expert, n_used, n_slots = _moe_plan(rt2d, counts)
        xs = _dispatch(dest, h_tiles, n_slots)
        yb = _expert_ffn(block_expert, n_used, xs, l, moe_w_gate, moe_w_up, moe_w_down)
        x = _combine(dest, x3.reshape(T, D), rt2d, yb).reshape(B, S, D)
    return x
```

```python
import functools

import jax
import jax.numpy as jnp
from jax import lax
from jax.experimental import pallas as pl
from jax.experimental.pallas import tpu as pltpu

F32 = jnp.float32
BF16 = jnp.bfloat16

CHUNK = 64
RWKV_HEADS = 8
HEAD_DIM = 64
RWKV_WIDTH = 512
ATT_HEADS = 8
ATT_WIDTH = 512
DECAY_LORA = 64
AAA_LORA = 64
MV_LORA = 32
GATE_LORA = 128
RWKV_IN = 3 * RWKV_WIDTH + DECAY_LORA + AAA_LORA + GATE_LORA
LORA_OFF = 3 * RWKV_WIDTH
GATE_OFF = LORA_OFF + DECAY_LORA + AAA_LORA
IN_WIDTH = RWKV_IN + 3 * ATT_WIDTH
PREV_CHUNKS = 8
BAND = (PREV_CHUNKS + 1) * CHUNK
BAND_PAD = PREV_CHUNKS * CHUNK
REL_MAX = 256
XA_HEADS = 4
XA_HEAD_DIM = 128
XA_WIDTH = 512
N_GROUPS = 4
EXPERTS_PER_GROUP = 8
N_EXPERTS = 32
TOP_K = 2
EXPERT_FF = 512
MOE_BLOCK = 512
NORM_EPS = 1e-6
LNX_EPS = 64e-5
NEG_INF = -1e30

LANES = 128
SUBLANES = 8
ROW_TILE = 256
INPROJ_ROWS = 512
MIX_ROWS = 512
MIX_GROUPS = 2
LOG2E = 1.4426950408889634
WKV_ROWS = 256
WKV_GROUP_CHUNKS = 2
WKV_SEQS = 2
NEUMANN_SQUARINGS =CHUNK.bit_length() - 2
ATT_ROWS = 256
ATT_GROUP = 2
MOE_ROWS = 512
ROUTE_LANES = 128
DMA_UNROLL = 8


def _cparams(semantics, vmem_mib):
    return pltpu.CompilerParams(dimension_semantics=semantics,
                                vmem_limit_bytes=vmem_mib * 1024 * 1024)


def _dot(a, b):
    return jnp.dot(a, b, preferred_element_type=F32)


def _dot_nt(a, b):
    return lax.dot_general(a, b, (((1,), (1,)), ((), ())), preferred_element_type=F32)


def _rms(x, g):
    ms = jnp.mean(x * x, axis=-1, keepdims=True)
    return x * lax.rsqrt(ms + NORM_EPS) * g


def _group_sum(x, ones_bd):
    hi = x.astype(BF16)
    lo = (x - hi.astype(F32)).astype(BF16)
    return _dot(hi, ones_bd) + _dot(lo, ones_bd)


def _group_sumsq(x, ones_bd):
    return _group_sum(x * x, ones_bd)


def _inproj_kernel(x_ref, g_ref, w_ref, gq_ref, gk_ref, u_ref, q_ref, k_ref, v_ref):
    h = _rms(x_ref[...], g_ref[...]).astype(BF16)
    for j in range(0, RWKV_IN, 256):
        u_ref[:, j:j + 256] = _dot(h, w_ref[:, j:j + 256])
    even = lax.broadcasted_iota(jnp.int32, (x_ref.shape[0], LANES), 1) < HEAD_DIM

    def head_rms(x, gain_ref, scale):
        for j in range(0, ATT_WIDTH, LANES):
            xs = x[:, j:j + LANES]
            sq = xs * xs
            ss = jnp.where(even, jnp.sum(jnp.where(even, sq, 0.0), axis=-1, keepdims=True),
                           jnp.sum(jnp.where(even, 0.0, sq), axis=-1, keepdims=True))
            yield j, xs * lax.rsqrt(ss * (1.0 / HEAD_DIM) + NORM_EPS) * (gain_ref[:, j:j + LANES] * scale)

    q = _dot(h, w_ref[:, RWKV_IN:RWKV_IN + ATT_WIDTH])
    for j, qn in head_rms(q, gq_ref, HEAD_DIM ** -0.5 * LOG2E):
        q_ref[:, j:j + LANES] = qn.astype(BF16)
    k = _dot(h, w_ref[:, RWKV_IN + ATT_WIDTH:RWKV_IN + 2 * ATT_WIDTH])
    for j, kn in head_rms(k, gk_ref, 1.0):
        k_ref[:, j:j + LANES] = kn.astype(BF16)
    v_ref[...] = _dot(h, w_ref[:, RWKV_IN + 2 * ATT_WIDTH:]).astype(BF16)


def _inproj(x2d, g, w, gq, gk):
    T, D = x2d.shape
    tm = INPROJ_ROWS
    full = lambda i: (0, 0)
    row = lambda i: (i, 0)
    return pl.pallas_call(
        _inproj_kernel,
        grid=(T // tm,),
        in_specs=[pl.BlockSpec((tm, D), row), pl.BlockSpec((1, D), full),
                  pl.BlockSpec((D, IN_WIDTH), full), pl.BlockSpec((1, ATT_WIDTH), full),
                  pl.BlockSpec((1, ATT_WIDTH), full)],
        out_specs=[pl.BlockSpec((tm, RWKV_IN), row), pl.BlockSpec((tm, ATT_WIDTH), row),
                   pl.BlockSpec((tm, ATT_WIDTH), row), pl.BlockSpec((tm, ATT_WIDTH), row)],
        out_shape=[jax.ShapeDtypeStruct((T, RWKV_IN), F32), jax.ShapeDtypeStruct((T, ATT_WIDTH), BF16),
                   jax.ShapeDtypeStruct((T, ATT_WIDTH), BF16), jax.ShapeDtypeStruct((T, ATT_WIDTH), BF16)],
        compiler_params=_cparams(("parallel",), 48),
        name="inproj",
    )(x2d, g, w, gq, gk)


def _rwkv_prep_kernel(has_vres, *refs):
    if has_vres:
        (u_ref, up_ref, mu_ref, w0_ref, w2_ref, a0_ref, a2_ref, g2_ref, kkp_ref, kap_ref, rkp_ref,
         bd_ref, tri_ref, vf_ref, v0_ref, v1_ref, v2_ref,
         at_ref, rt_ref, bt_ref, kt_ref, vb_ref, ge_ref, bonus_ref, g_ref) = refs
    else:
        (u_ref, up_ref, mu_ref, w0_ref, w2_ref, a0_ref, a2_ref, g2_ref, kkp_ref, kap_ref, rkp_ref,
         bd_ref, tri_ref,
         at_ref, rt_ref, bt_ref, kt_ref, vb_ref, ge_ref, bonus_ref, g_ref, v_ref) = refs
    i = pl.program_id(1)
    u = u_ref[...]
    ts = u.shape[0]
    prev_row = jnp.where(i > 0, up_ref[SUBLANES - 1:SUBLANES, :], 0.0)
    rolled = pltpu.roll(u, 1, 0)
    row_id = lax.broadcasted_iota(jnp.int32, u.shape, 0)
    shifted = jnp.where(row_id == 0, jnp.broadcast_to(prev_row, u.shape), rolled)
    u = u + mu_ref[...] * (shifted - u)

    r = u[:, 0:RWKV_WIDTH]
    k = u[:, RWKV_WIDTH:2 * RWKV_WIDTH]
    v = u[:, 2 * RWKV_WIDTH:3 * RWKV_WIDTH]
    lora = u[:, LORA_OFF:GATE_OFF]
    gl = u[:, GATE_OFF:RWKV_IN]

    w = w0_ref[...] + _dot(jnp.tanh(lora).astype(BF16), w2_ref[...])
    z = -w
    softplus = jnp.maximum(z, 0.0) + jnp.log1p(jnp.exp(-jnp.abs(z)))
    w = -softplus - 0.5
    log_decay = -jnp.exp(w)
    a = jax.nn.sigmoid(a0_ref[...] + _dot(lora.astype(BF16), a2_ref[...]))
    g_ref[...] = _dot(jax.nn.sigmoid(gl).astype(BF16), g2_ref[...])
    if has_vres:
        mix = _dot(_dot(v.astype(BF16), v1_ref[...]).astype(BF16), v2_ref[...])
        v = v + (vf_ref[...] - v) * jax.nn.sigmoid(v0_ref[...] + mix)
    else:
        v_ref[...] = v

    bd = bd_ref[...]
    kk = k * kkp_ref[...]
    kk = kk * lax.rsqrt(jnp.maximum(_group_sumsq(kk, bd), 1e-24))
    k2 = k * (1.0 + (a - 1.0) * kap_ref[...])
    bonus_ref[...] = _group_sum(r * k2 * rkp_ref[...], bd) * v

    hi = log_decay.astype(BF16)
    lo = (log_decay - hi.astype(F32)).astype(BF16)
    tri = tri_ref[...]
    cum = _dot(tri, hi) + _dot(tri, lo)
    inv_gamma = jnp.exp(-cum)
    at_ref[...] = (-kk * jnp.exp(cum - log_decay)).astype(BF16)
    rt_ref[...] = (r * jnp.exp(cum)).astype(BF16)
    bt_ref[...] = (kk * a * inv_gamma).astype(BF16)
    kt_ref[...] = (k2 * inv_gamma).astype(BF16)
    vb_ref[...] = v.astype(BF16)
    for c in range(ts // CHUNK):
        ge_ref[c] = jnp.exp(jnp.sum(log_decay[c * CHUNK:(c + 1) * CHUNK, :], axis=0, keepdims=True))


def _rwkv_prep(u3, mu, w0, w2p, a0, a2p, g2, kkp, kap, rkp, ones_bd, vres):
    B, S, _ = u3.shape
    ts = ROW_TILE
    W = RWKV_WIDTH
    n_c = ts // CHUNK
    full2 = lambda b, i: (0, 0)
    tile = lambda b, i: (b, i, 0)
    prev = lambda b, i: (b, jnp.maximum(i * (ts // SUBLANES) - 1, 0), 0)
    tri = (jnp.tril(jnp.ones((ts, ts), F32))
           * jnp.kron(jnp.eye(n_c, dtype=F32), jnp.ones((CHUNK, CHUNK), F32))).astype(BF16)
    in_specs = [pl.BlockSpec((None, ts, RWKV_IN), tile),
                pl.BlockSpec((None, SUBLANES, RWKV_IN), prev),
                pl.BlockSpec((1, RWKV_IN), full2), pl.BlockSpec((1, W), full2),
                pl.BlockSpec((LANES, W), full2), pl.BlockSpec((1, W), full2),
                pl.BlockSpec((LANES, W), full2), pl.BlockSpec((GATE_LORA, W), full2),
                pl.BlockSpec((1, W), full2), pl.BlockSpec((1, W), full2), pl.BlockSpec((1, W), full2),
                pl.BlockSpec((W, W), full2), pl.BlockSpec((ts, ts), full2)]
    args = [u3, u3, mu, w0, w2p, a0, a2p, g2, kkp, kap, rkp, ones_bd, tri]
    if vres is not None:
        v_first, v0, v1p, v2p = vres
        in_specs += [pl.BlockSpec((None, ts, W), tile), pl.BlockSpec((1, W), full2),
                     pl.BlockSpec((W, LANES), full2), pl.BlockSpec((LANES, W), full2)]
        args += [v_first, v0, v1p, v2p]
    tok = pl.BlockSpec((None, ts, W), tile)
    out_specs = [tok] * 5 + [pl.BlockSpec((None, n_c, 1, W), lambda b, i: (b, i, 0, 0)), tok, tok]
    out_shape = ([jax.ShapeDtypeStruct((B, S, W), BF16)] * 5
                 + [jax.ShapeDtypeStruct((B, S // CHUNK, 1, W), F32)]
                 + [jax.ShapeDtypeStruct((B, S, W), F32)] * 2)
    if vres is None:
        out_specs.append(tok)
        out_shape.append(jax.ShapeDtypeStruct((B, S, W), F32))
    return pl.pallas_call(
        functools.partial(_rwkv_prep_kernel, vres is not None),
        grid=(B, S // ts),
        in_specs=in_specs,
        out_specs=out_specs,
        out_shape=out_shape,
        compiler_params=_cparams(("parallel", "parallel"), 48),
        name="rwkv_prep",
    )(*args)


def _wkv_chunk_kernel(at_ref, rt_ref, bt_ref, kt_ref, vb_ref, ge_ref, bonus_ref, lw_ref, lb_ref,
                      y_ref, n_ref):
    C = CHUNK
    n_pairs = RWKV_HEADS // 2
    n_seqs = at_ref.shape[0]
    n_chunks = at_ref.shape[1] // C

    @pl.when(pl.program_id(1) == 0)
    def _():
        n_ref[...] = jnp.zeros_like(n_ref)

    row = lax.broadcasted_iota(jnp.int32, (2 * C, LANES), 0)
    lane = lax.broadcasted_iota(jnp.int32, (2 * C, LANES), 1)
    top = row < C
    left = lane < HEAD_DIM
    same = top == left
    t_row = jnp.where(top, row, row - C)
    s_col = jnp.where(left, lane, lane - HEAD_DIM)
    strict = s_col < t_row
    incl = s_col <= t_row
    even = lax.broadcasted_iota(jnp.int32, (C, LANES), 1) < HEAD_DIM
    zeros_c = jnp.zeros((C, LANES), BF16)

    unit_groups = [[(c, b, p) for c in range(c0, c0 + WKV_GROUP_CHUNKS) for b in range(n_seqs)
                    for p in range(n_pairs)] for c0 in range(0, n_chunks, WKV_GROUP_CHUNKS)]

    def tile(ref, unit):
        c, b, p = unit
        return ref[b, c * C:(c + 1) * C, p * LANES:(p + 1) * LANES]

    def stack_heads(x):
        z = jnp.zeros_like(x)
        return jnp.concatenate([jnp.where(even, x, z), jnp.where(even, z, x)], axis=0)

    def half_sum(x):
        lo = jnp.sum(jnp.where(even, x, 0.0), axis=-1, keepdims=True)
        hi = jnp.sum(jnp.where(even, 0.0, x), axis=-1, keepdims=True)
        return jnp.where(even, lo, hi)

    for units in unit_groups:
        at2 = [stack_heads(tile(at_ref, un)) for un in units]
        bk = [jnp.concatenate([tile(bt_ref, un), tile(kt_ref, un)], axis=0) for un in units]
        ga = [jnp.where(strict, _dot_nt(a, b), 0.0) for a, b in zip(at2, bk)]
        gr = [jnp.where(incl, _dot_nt(stack_heads(tile(rt_ref, un)), b), 0.0).astype(BF16)
              for un, b in zip(units, bk)]
        zs = [jnp.where(same, _dot(g.astype(BF16), jnp.concatenate([zeros_c, tile(vb_ref, un)], axis=0)), 0.0)
              for un, g in zip(units, ga)]
        pw = [jnp.where(same, jnp.where(top, g, pltpu.roll(g, HEAD_DIM, 1)), 0.0).astype(BF16) for g in ga]
        w = [jnp.concatenate([a.astype(F32), z], axis=1) for a, z in zip(at2, zs)]
        w = [x + _dot(m, x.astype(BF16)) for m, x in zip(pw, w)]
        for _ in range(NEUMANN_SQUARINGS):
            pw = [_dot(m, m).astype(BF16) for m in pw]
            w = [x + _dot(m, x.astype(BF16)) for m, x in zip(pw, w)]
        pm = [(x[:C, :LANES] + x[C:, :LANES]).astype(BF16) for x in w]
        q = [x[:C, LANES:] + x[C:, LANES:] for x in w]

        for i, un in enumerate(units):
            c, b, p = un
            cols = slice(p * LANES, (p + 1) * LANES)
            rows = slice(c * C, (c + 1) * C)
            n0 = n_ref[b, p]
            pr = _dot_nt(jnp.concatenate([pm[i], tile(rt_ref, un)], axis=0), n0.astype(BF16))
            u = pr[:C] + q[i]
            uv = jnp.concatenate([u.astype(BF16), tile(vb_ref, un)], axis=0)
            yp = _dot(gr[i], uv)
            y = pr[C:] + jnp.where(even, yp[:C], yp[C:])
            g_slab = ge_ref[b, c, :, cols]
            bkg = (bk[i].astype(F32) * g_slab).astype(BF16)
            dn = lax.dot_general(uv, bkg, (((0,), (0,)), ((), ())), preferred_element_type=F32)
            n_ref[b, p] = g_slab * n0 + jnp.where(same, dn, 0.0)
            mean = half_sum(y) * (1.0 / HEAD_DIM)
            yc = y - mean
            var = half_sum(yc * yc) * (1.0 / HEAD_DIM)
            yn = yc * lax.rsqrt(var + LNX_EPS) * lw_ref[:, cols] + lb_ref[:, cols]
            y_ref[b, rows, cols] = yn + bonus_ref[b, rows, cols]


def _wkv_chunked(at, rt, bt, kt, vb, ge, bonus, lnw, lnb):
    B, S, W = at.shape
    ts = WKV_ROWS
    nb = WKV_SEQS if B % WKV_SEQS == 0 else 1
    n_c = ts // CHUNK
    tile = lambda b, i: (b, i, 0)
    full = lambda b, i: (0, 0)
    tok = pl.BlockSpec((nb, ts, W), tile)
    return pl.pallas_call(
        _wkv_chunk_kernel,
        grid=(B // nb, S // ts),
        in_specs=[tok] * 5 + [pl.BlockSpec((nb, n_c, 1, W), lambda b, i: (b, i, 0, 0)), tok,
                              pl.BlockSpec((1, W), full), pl.BlockSpec((1, W), full)],
        out_specs=tok,
        out_shape=jax.ShapeDtypeStruct((B, S, W), F32),
        scratch_shapes=[pltpu.VMEM((nb, RWKV_HEADS // 2, 2 * HEAD_DIM, LANES), F32)],
        compiler_params=_cparams(("parallel", "arbitrary"), 32),
        name="wkv_chunked",
    )(at, rt, bt, kt, vb, ge, bonus, lnw, lnb)


def _band_attn_kernel(q_ref, k_ref, v_ref, bias_ref, o_ref, s_ref, e_ref):
    i = pl.program_id(1)
    n_chunks = q_ref.shape[0] // CHUNK
    n_pairs = ATT_HEADS // 2
    lane = lax.broadcasted_iota(jnp.int32, (CHUNK, LANES), 1)
    even = lane < HEAD_DIM
    slot = lax.broadcasted_iota(jnp.int32, (1, 1, BAND), 2)

    group = s_ref.shape[0]

    def chunk_body(masked, cc, carry):
        cgs = [i * n_chunks + cc * group + j for j in range(group)]
        starts = [pl.multiple_of(cg * CHUNK, CHUNK) for cg in cgs]
        rows = [pl.multiple_of((cc * group + j) * CHUNK, CHUNK) for j in range(group)]
        for j in range(group):
            for p in range(n_pairs):
                cols = slice(p * LANES, (p + 1) * LANES)
                q2 = q_ref[pl.ds(rows[j], CHUNK), cols]
                zero = jnp.zeros_like(q2)
                qs = jnp.concatenate([jnp.where(even, q2, zero), jnp.where(even, zero, q2)], axis=0)
                s_ref[j, p] = _dot_nt(qs, k_ref[pl.ds(starts[j], BAND), cols]) + bias_ref[p]
        inv = []
        for j in range(group):
            s = s_ref[j]
            if masked:
                s = jnp.where(slot >= (BAND_PAD - cgs[j] * CHUNK), s, NEG_INF)
            e = jnp.exp2(s - jnp.max(s, axis=-1, keepdims=True))
            inv.append(1.0 / jnp.sum(e, axis=-1, keepdims=True))
            e_ref[j] = e.astype(BF16)
        for j in range(group):
            for p in range(n_pairs):
                cols = slice(p * LANES, (p + 1) * LANES)
                o = _dot(e_ref[j, p], v_ref[pl.ds(starts[j], BAND), cols]) * inv[j][p]
                o_ref[pl.ds(rows[j], CHUNK), cols] = jnp.where(even, o[:CHUNK], o[CHUNK:])
        return carry

    @pl.when(i * n_chunks < PREV_CHUNKS)
    def _():
        lax.fori_loop(0, n_chunks // group, functools.partial(chunk_body, True), 0)

    @pl.when(i * n_chunks >= PREV_CHUNKS)
    def _():
        lax.fori_loop(0, n_chunks // group, functools.partial(chunk_body, False), 0)


def _band_attn(q3, kp3, vp3, bias_pairs):
    B, S, W = q3.shape
    SP = kp3.shape[1]
    tq = ATT_ROWS
    assert PREV_CHUNKS % (tq // CHUNK) == 0
    n_pairs = ATT_HEADS // 2
    return pl.pallas_call(
        _band_attn_kernel,
        grid=(B, S // tq),
        in_specs=[pl.BlockSpec((None, tq, W), lambda b, i: (b, i, 0)),
                  pl.BlockSpec((None, SP, W), lambda b, i: (b, 0, 0)),
                  pl.BlockSpec((None, SP, W), lambda b, i: (b, 0, 0)),
                  pl.BlockSpec((ATT_HEADS // 2, 2 * CHUNK, BAND), lambda b, i: (0, 0, 0))],
        out_specs=pl.BlockSpec((None, tq, W), lambda b, i: (b, i, 0)),
        out_shape=jax.ShapeDtypeStruct((B, S, W), F32),
        scratch_shapes=[pltpu.VMEM((ATT_GROUP, n_pairs, 2 * CHUNK, BAND), F32),
                        pltpu.VMEM((ATT_GROUP, n_pairs, 2 * CHUNK, BAND), BF16)],
        compiler_params=_cparams(("parallel", "arbitrary"), 48),
        name="band_attn",
    )(q3, kp3, vp3, bias_pairs)


def _mem_kv_kernel(m_ref, g_ref, w_ref, kn_ref, k_ref, v_ref):
    h = _rms(m_ref[...], g_ref[...]).astype(BF16)
    kv = _dot(h, w_ref[...])
    for hd in range(XA_HEADS):
        cols = slice(hd * XA_HEAD_DIM, (hd + 1) * XA_HEAD_DIM)
        k_ref[:, cols] = _rms(kv[:, cols], kn_ref[...]).astype(BF16)
    v_ref[...] = kv[:, XA_WIDTH:].astype(BF16)


def _mem_kv(mem2d, g, w_kv, k_norm):
    R, D = mem2d.shape
    tm = ROW_TILE
    full = lambda i: (0, 0)
    row = lambda i: (i, 0)
    return pl.pallas_call(
        _mem_kv_kernel,
        grid=(R // tm,),
        in_specs=[pl.BlockSpec((tm, D), row), pl.BlockSpec((1, D), full),
                  pl.BlockSpec((D, 2 * XA_WIDTH), full), pl.BlockSpec((1, XA_HEAD_DIM), full)],
        out_specs=[pl.BlockSpec((tm, XA_WIDTH), row)] * 2,
        out_shape=[jax.ShapeDtypeStruct((R, XA_WIDTH), BF16)] * 2,
        compiler_params=_cparams(("parallel",), 32),
        name="mem_kv",
    )(mem2d, g, w_kv, k_norm)


def _mix_xa_router_kernel(x_ref, yr_ref, g_ref, ya_ref, gao_ref, wout_ref, gxa_ref, wq_ref, qn_ref,
                          km_ref, vm_ref, wo_ref, gff_ref, wrt_ref, brt_ref,
                          tri_ref, xo_ref, h_ref, rt_ref, cnt_ref, run_ref, stage_ref):
    first = (pl.program_id(0) == 0) & (pl.program_id(1) == 0)

    @pl.when(first)
    def _():
        run_ref[...] = jnp.zeros_like(run_ref)

    n_rows = x_ref.shape[0]
    groups = [slice(r, r + n_rows // MIX_GROUPS) for r in range(0, n_rows, n_rows // MIX_GROUPS)]
    head_cols = [slice(hd * XA_HEAD_DIM, (hd + 1) * XA_HEAD_DIM) for hd in range(XA_HEADS)]

    yr = [(yr_ref[rows, :] * g_ref[rows, :]).astype(BF16) for rows in groups]
    ya = [_rms(ya_ref[rows, :], gao_ref[...]).astype(BF16) for rows in groups]
    x1 = [x_ref[rows, :] + _dot(a, wout_ref[0:RWKV_WIDTH, :]) + _dot(b, wout_ref[RWKV_WIDTH:, :])
          for rows, a, b in zip(groups, yr, ya)]

    q = [_dot(_rms(xg, gxa_ref[...]).astype(BF16), wq_ref[...]) for xg in x1]
    qh = [[_rms(qg[:, cols], qn_ref[...]).astype(BF16) for cols in head_cols] for qg in q]
    s = [[_dot_nt(qg[hd], km_ref[:, cols]) * (XA_HEAD_DIM ** -0.5) for hd, cols in enumerate(head_cols)]
         for qg in qh]
    e = [[jnp.exp(sh - jnp.max(sh, axis=-1, keepdims=True)) for sh in sg] for sg in s]
    pr = [[(eh / jnp.sum(eh, axis=-1, keepdims=True)).astype(BF16) for eh in eg] for eg in e]
    o = [jnp.concatenate([_dot(pg[hd], vm_ref[:, cols]) for hd, cols in enumerate(head_cols)],
                         axis=-1).astype(BF16) for pg in pr]
    x = jnp.concatenate([xg + _dot(og, wo_ref[...]) for xg, og in zip(x1, o)], axis=0)
    xo_ref[...] = x

    h = _rms(x, gff_ref[...])
    tm = h.shape[0]
    _store_row_tiles(stage_ref, h)
    h_ref[...] = stage_ref[...].astype(BF16)
    logits =_dot(h.astype(BF16), wrt_ref[...]) + brt_ref[...]
    lane_i = lax.broadcasted_iota(jnp.int32, logits.shape, 1)
    lane = lane_i.astype(F32)
    big = float(ROUTE_LANES)
    gmask = lane < N_GROUPS
    gmax = jnp.max(jnp.where(gmask, logits, -jnp.inf), axis=-1, keepdims=True)
    ge = jnp.where(gmask, jnp.exp(logits - gmax), 0.0)
    gp = ge / jnp.sum(ge, axis=-1, keepdims=True)
    g_gate = jnp.max(gp, axis=-1, keepdims=True)
    g_top = jnp.min(jnp.where(gmask & (gp == g_gate), lane, big), axis=-1, keepdims=True)
    lo = N_GROUPS + g_top * EXPERTS_PER_GROUP
    rmask = (lane >= lo) & (lane < lo + EXPERTS_PER_GROUP)
    rmax = jnp.max(jnp.where(rmask, logits, -jnp.inf), axis=-1, keepdims=True)
    re = jnp.where(rmask, jnp.exp(logits - rmax), 0.0)
    p_in = re / jnp.sum(re, axis=-1, keepdims=True)
    p1 = jnp.max(jnp.where(rmask, p_in, -1.0), axis=-1, keepdims=True)
    j1 = jnp.min(jnp.where(rmask & (p_in == p1), lane, big), axis=-1, keepdims=True)
    rmask2 = rmask & (lane != j1)
    p2 = jnp.max(jnp.where(rmask2, p_in, -1.0), axis=-1, keepdims=True)
    j2 = jnp.min(jnp.where(rmask2 & (p_in == p2), lane, big), axis=-1, keepdims=True)
    denom = p1 + p2
    w1 = g_gate * (p1 / denom)
    w2 = g_gate * (p2 / denom)
    e1 = j1 - N_GROUPS
    e2 = j2 - N_GROUPS
    hit1 = lane == j1
    hit2 = lane == j2
    hits = jnp.where(hit1 | hit2, 1.0, 0.0)
    before = _dot(tri_ref[...], hits.astype(BF16)) + run_ref[...]
    rank1 = jnp.sum(jnp.where(hit1, before, 0.0), axis=-1, keepdims=True)
    rank2 = jnp.sum(jnp.where(hit2, before, 0.0), axis=-1, keepdims=True)
    run = run_ref[...] + jnp.sum(hits, axis=0, keepdims=True)
    run_ref[...] = run
    cnt_ref[...] = jnp.broadcast_to(run, cnt_ref.shape)
    rt_ref[...] = jnp.where(lane_i == 0, e1, jnp.where(lane_i == 1, e2,
                            jnp.where(lane_i == 2, w1, jnp.where(lane_i == 3, w2,
                                      jnp.where(lane_i == 4, rank1, jnp.where(lane_i == 5, rank2, 0.0))))))


def _mix_xa_router(x3, yr3, g3, ya3, gao, wout, gxa, wq, qn, km3, vm3, wo, gff, wrt, brt):
    B, S, D = x3.shape
    M = km3.shape[1]
    tm = MIX_ROWS
    tile = lambda b, i: (b, i, 0)
    full = lambda b, i: (0, 0)
    memb = lambda b, i: (b, 0, 0)
    W = RWKV_WIDTH
    n_i = S // tm
    tri = jnp.tril(jnp.ones((tm, tm), F32), -1).astype(BF16)
    return pl.pallas_call(
        _mix_xa_router_kernel,
        grid=(B, S // tm),
        in_specs=[pl.BlockSpec((None, tm, D), tile), pl.BlockSpec((None, tm, W), tile),
                  pl.BlockSpec((None, tm, W), tile), pl.BlockSpec((None, tm, W), tile),
                  pl.BlockSpec((1, W), full), pl.BlockSpec((2 * W, D), full),
                  pl.BlockSpec((1, D), full), pl.BlockSpec((D, XA_WIDTH), full),
                  pl.BlockSpec((1, XA_HEAD_DIM), full),
                  pl.BlockSpec((None, M, XA_WIDTH), memb), pl.BlockSpec((None, M, XA_WIDTH), memb),
                  pl.BlockSpec((XA_WIDTH, D), full), pl.BlockSpec((1, D), full),
                  pl.BlockSpec((D, ROUTE_LANES), full), pl.BlockSpec((1, ROUTE_LANES), full),
                  pl.BlockSpec((tm, tm), full)],
        out_specs=[pl.BlockSpec((None, tm, D), tile),
                   pl.BlockSpec((tm * SUBLANES, LANES), lambda b, i: (b * n_i + i, 0)),
                   pl.BlockSpec((None, tm, ROUTE_LANES), tile),
                   pl.BlockSpec((SUBLANES, ROUTE_LANES), full)],
        out_shape=[jax.ShapeDtypeStruct((B, S, D), F32),
                   jax.ShapeDtypeStruct((B * S * SUBLANES, LANES), BF16),
                   jax.ShapeDtypeStruct((B, S, ROUTE_LANES), F32),
                   jax.ShapeDtypeStruct((SUBLANES, ROUTE_LANES), F32)],
        scratch_shapes=[pltpu.VMEM((1, ROUTE_LANES), F32), pltpu.VMEM((tm * SUBLANES, LANES), F32)],
        compiler_params=_cparams(("arbitrary", "arbitrary"), 48),
        name="mix_xa_router",
    )(x3, yr3, g3, ya3, gao, wout, gxa, wq, qn, km3, vm3, wo, gff, wrt, brt, tri)


def _row_copy(src_ref, src_row, dst_ref, dst_row, sem):
    src = src_ref.at[pl.ds(pl.multiple_of(src_row * SUBLANES, SUBLANES), SUBLANES)]
    dst = dst_ref.at[pl.ds(pl.multiple_of(dst_row * SUBLANES, SUBLANES), SUBLANES)]
    return pltpu.make_async_copy(src, dst, sem)


def _load_row_tiles(stage_ref, rows):
    return jnp.concatenate([stage_ref[pl.ds(j, rows, stride=SUBLANES), :] for j in range(SUBLANES)], axis=-1)


def _store_row_tiles(stage_ref, x):
    for j in range(SUBLANES):
        stage_ref[pl.ds(j, x.shape[0], stride=SUBLANES), :] = x[:, j * LANES:(j + 1) * LANES]


def _dispatch_kernel(dest_ref, h_ref, zeros_ref, xs_ref, sem):
    del zeros_ref
    tm = h_ref.shape[0] // SUBLANES
    base = pl.program_id(0) * (tm * TOP_K)

    def issue(r, carry):
        for kk in range(TOP_K):
            _row_copy(h_ref, r, xs_ref, dest_ref[base + r * TOP_K + kk], sem).start(priority=kk)
        return carry

    lax.fori_loop(0, tm, issue, 0, unroll=DMA_UNROLL)

    def drain(r, carry):
        for kk in range(TOP_K):
            _row_copy(h_ref, r, xs_ref, dest_ref[base + r * TOP_K + kk], sem).wait()
        return carry

    lax.fori_loop(0, tm, drain, 0, unroll=DMA_UNROLL)


def _dispatch(dest, h_tiles, n_slots):
    tm = MOE_ROWS
    T = h_tiles.shape[0] // SUBLANES
    zeros = jnp.zeros((n_slots * SUBLANES, LANES), BF16)
    return pl.pallas_call(
        _dispatch_kernel,
        grid_spec=pltpu.PrefetchScalarGridSpec(
            num_scalar_prefetch=1,
            grid=(T // tm,),
            in_specs=[pl.BlockSpec((tm * SUBLANES, LANES), lambda i, d: (i, 0)),
                      pl.BlockSpec(memory_space=pl.ANY)],
            out_specs=pl.BlockSpec(memory_space=pl.ANY),
            scratch_shapes=[pltpu.SemaphoreType.DMA(())],
        ),
        out_shape=jax.ShapeDtypeStruct((n_slots * SUBLANES, LANES), BF16),
        input_output_aliases={2: 0},
        compiler_params=_cparams(("arbitrary",), 32),
        name="moe_dispatch",
    )(dest, h_tiles, zeros)


def _expert_kernel(be_ref, nu_ref, xs_ref, wg_ref, wu_ref, wd_ref, y_ref, wgb_ref, wub_ref, wdb_ref,
                   stage_ref):
    i = pl.program_id(0)
    used = i < nu_ref[0]
    new_expert = (i == 0) | (be_ref[i] != be_ref[jnp.maximum(i - 1, 0)])

    @pl.when(used & new_expert)
    def _():
        wgb_ref[...] = wg_ref[...].astype(BF16)
        wub_ref[...] = wu_ref[...].astype(BF16)
        wdb_ref[...] = wd_ref[...].astype(BF16)

    @pl.when(used)
    def _():
        stage_ref[...] = xs_ref[...].astype(F32)
        xb = _load_row_tiles(stage_ref, MOE_BLOCK).astype(BF16)
        gate = _dot(xb, wgb_ref[...])
        up = _dot(xb, wub_ref[...])
        act = (gate * jax.nn.sigmoid(gate) * up).astype(BF16)
        _store_row_tiles(stage_ref, _dot(act, wdb_ref[...]))
        y_ref[...] = stage_ref[...].astype(BF16)

    @pl.when(jnp.logical_not(used))
    def _():
        y_ref[...] = jnp.zeros_like(y_ref)


def _expert_ffn(block_expert, n_used, xs_tiles, layer, wg, wu, wd):
    D, FF = wg.shape[2], wg.shape[3]
    n_blocks = xs_tiles.shape[0] // (MOE_BLOCK * SUBLANES)
    rows = MOE_BLOCK * SUBLANES
    blk = lambda i, be, nu: (jnp.minimum(i, nu[0] - 1), 0)
    wsel = lambda i, be, nu: (layer, be[jnp.minimum(i, nu[0] - 1)], 0, 0)
    return pl.pallas_call(
        _expert_kernel,
        grid_spec=pltpu.PrefetchScalarGridSpec(
            num_scalar_prefetch=2,
            grid=(n_blocks,),
            in_specs=[pl.BlockSpec((rows, LANES), blk),
                      pl.BlockSpec((None, None, D, FF), wsel),
                      pl.BlockSpec((None, None, D, FF), wsel),
                      pl.BlockSpec((None, None, FF, D), wsel)],
            out_specs=pl.BlockSpec((rows, LANES), lambda i, be, nu: (i, 0)),
            scratch_shapes=[pltpu.VMEM((D, FF), BF16), pltpu.VMEM((D, FF), BF16), pltpu.VMEM((FF, D), BF16),
                            pltpu.VMEM((rows, LANES), F32)],
        ),
        out_shape=jax.ShapeDtypeStruct(xs_tiles.shape, BF16),
        compiler_params=_cparams(("arbitrary",), 48),
        name="moe_experts",
    )(block_expert, n_used, xs_tiles, wg, wu, wd)


def _combine_kernel(dest_ref, x_ref, wt_ref, yb_ref, o_ref, buf_ref, sem, stage_ref):
    tm = x_ref.shape[0]
    base = pl.program_id(0) * (tm * TOP_K)

    def issue(r, carry):
        for kk in range(TOP_K):
            _row_copy(yb_ref, dest_ref[base + r * TOP_K + kk], buf_ref.at[kk], r, sem).start(priority=kk)
        return carry

    lax.fori_loop(0, tm, issue, 0, unroll=DMA_UNROLL)

    def drain(r, carry):
        for kk in range(TOP_K):
            _row_copy(yb_ref, dest_ref[base + r * TOP_K + kk], buf_ref.at[kk], r, sem).wait()
        return carry

    lax.fori_loop(0, tm, drain, 0, unroll=DMA_UNROLL)
    wt = wt_ref[...]
    w1 = wt[:, 2:3]
    w2 = wt[:, 3:4]
    stage_ref[...] = buf_ref[...].astype(F32)
    for j in range(SUBLANES):
        cols = slice(j * LANES, (j + 1) * LANES)
        y1 = stage_ref[0, pl.ds(j, tm, stride=SUBLANES), :]
        y2 = stage_ref[1, pl.ds(j, tm, stride=SUBLANES), :]
        o_ref[:, cols] = x_ref[:, cols] + (y1 * w1 + y2 * w2)


def _combine(dest, x2d, rt2d, yb_tiles):
    T, D = x2d.shape
    tm = MOE_ROWS
    return pl.pallas_call(
        _combine_kernel,
        grid_spec=pltpu.PrefetchScalarGridSpec(
            num_scalar_prefetch=1,
            grid=(T // tm,),
            in_specs=[pl.BlockSpec((tm, D), lambda i, d: (i, 0)),
                      pl.BlockSpec((tm, ROUTE_LANES), lambda i, d: (i, 0)),
                      pl.BlockSpec(memory_space=pl.ANY)],
            out_specs=pl.BlockSpec((tm, D), lambda i, d: (i, 0)),
            scratch_shapes=[pltpu.VMEM((TOP_K, tm * SUBLANES, LANES), BF16), pltpu.SemaphoreType.DMA(()),
                            pltpu.VMEM((TOP_K, tm * SUBLANES, LANES), F32)],
        ),
        out_shape=jax.ShapeDtypeStruct((T, D), F32),
        compiler_params=_cparams(("arbitrary",), 48),
        name="moe_combine",
    )(dest, x2d, rt2d, yb_tiles)


def _moe_plan(rt2d, counts_lanes):
    T = rt2d.shape[0]
    A = T * TOP_K
    e = rt2d[:, 0:TOP_K].astype(jnp.int32).reshape(A)
    rank = rt2d[:, 4:4 + TOP_K].astype(jnp.int32).reshape(A)
    counts = counts_lanes[0, N_GROUPS:N_GROUPS + N_EXPERTS].astype(jnp.int32)
    padded = ((counts + MOE_BLOCK - 1) // MOE_BLOCK) * MOE_BLOCK
    pends = jnp.cumsum(padded)
    pstarts = pends - padded
    onehot = e[:, None] == jnp.arange(N_EXPERTS, dtype=jnp.int32)[None, :]
    dest = jnp.sum(jnp.where(onehot, pstarts[None, :], 0), axis=1) + rank
    n_blocks = (A + N_EXPERTS * (MOE_BLOCK - 1) + MOE_BLOCK - 1) // MOE_BLOCK
    block_start = jnp.arange(n_blocks, dtype=jnp.int32) * MOE_BLOCK
    block_expert = jnp.minimum(jnp.sum(pends[None, :] <= block_start[:, None], axis=1), N_EXPERTS - 1)
    n_used = (pends[-1] // MOE_BLOCK).astype(jnp.int32).reshape(1)
    return dest.astype(jnp.int32), block_expert.astype(jnp.int32), n_used, n_blocks * MOE_BLOCK


def _rel_bias_pairs(table):
    n_ext = BAND + CHUNK - 1
    ext = jnp.concatenate([table, jnp.broadcast_to(table[:, -1:], (table.shape[0], n_ext - table.shape[1]))], axis=1)
    rev = ext[:, ::-1]
    bias = jnp.stack([rev[:, CHUNK - 1 - i:CHUNK - 1 - i + BAND] for i in range(CHUNK)], axis=1)
    return (bias.astype(F32) * LOG2E).reshape(-1, ATT_HEADS // 2, 2 * CHUNK, BAND)


def _pad_rows(w, rows, offset=0):
    out = jnp.zeros((rows, w.shape[1]), w.dtype)
    return out.at[offset:offset + w.shape[0]].set(w)


def kernel(x, mem, norm_mix, w_in, shift_mu, decay_w0, decay_w2, iclr_a0, iclr_a2, gate_g2, rw_k_k, rw_k_a, rw_r_k, lnx_w, lnx_b, vres_v0, vres_v1, vres_v2, att_q_norm, att_k_norm, att_rel_bias, att_out_norm, w_out, norm_xa, norm_mem, xa_w_q, xa_w_kv, xa_w_o, xa_q_norm, xa_k_norm, norm_ffn, moe_w_group, moe_b_group, moe_w_route, moe_b_route, moe_w_gate, moe_w_up, moe_w_down):
    B, S, D = x.shape
    T = B * S
    depth = w_in.shape[0]
    M = mem.shape[1]
    row = lambda p: p.reshape(1, -1)
    ones_bd = jnp.kron(jnp.eye(ATT_HEADS, dtype=F32), jnp.ones((HEAD_DIM, HEAD_DIM), F32)).astype(BF16)

    bias_pairs = _rel_bias_pairs(att_rel_bias.reshape(depth * ATT_HEADS, -1))

    v_first = None
    for l in range(depth):
        u, qh, kh, vh = _inproj(x.reshape(T, D), row(norm_mix[l]), w_in[l].astype(BF16),
                                row(jnp.tile(att_q_norm[l], ATT_HEADS)),
                                row(jnp.tile(att_k_norm[l], ATT_HEADS)))
        w2p = _pad_rows(decay_w2[l], LANES, 0).astype(BF16)
        a2p = _pad_rows(iclr_a2[l], LANES, DECAY_LORA).astype(BF16)
        vres = None
        if l > 0:
            v1p = jnp.zeros((RWKV_WIDTH, LANES), F32).at[:, :MV_LORA].set(vres_v1[l - 1]).astype(BF16)
            v2p = _pad_rows(vres_v2[l - 1], LANES, 0).astype(BF16)
            vres = (v_first, row(vres_v0[l - 1]), v1p, v2p)
        prep = _rwkv_prep(u.reshape(B, S, RWKV_IN), row(shift_mu[l]), row(decay_w0[l]),
                          w2p, row(iclr_a0[l]), a2p, gate_g2[l].astype(BF16),
                          row(rw_k_k[l]), row(rw_k_a[l]), row(rw_r_k[l]), ones_bd, vres)
        at3, rt3, bt3, kt3, vb3, ge4, bonus3, g3 = prep[:8]
        if l == 0:
            v_first = prep[8]
        yr3 = _wkv_chunked(at3, rt3, bt3, kt3, vb3, ge4, bonus3, row(lnx_w[l]), row(lnx_b[l]))
        pad = ((0, 0), (BAND_PAD, 0), (0, 0))
        ya3 = _band_attn(qh.reshape(B, S, ATT_WIDTH), jnp.pad(kh.reshape(B, S, ATT_WIDTH), pad),
                         jnp.pad(vh.reshape(B, S, ATT_WIDTH), pad), bias_pairs[l])
        km, vm = _mem_kv(mem.reshape(B * M, D), row(norm_mem[l]), xa_w_kv[l].astype(BF16), row(xa_k_norm[l]))
        wrt = jnp.zeros((D, ROUTE_LANES), F32)
        wrt = wrt.at[:, :N_GROUPS].set(moe_w_group[l]).at[:, N_GROUPS:N_GROUPS + N_EXPERTS].set(moe_w_route[l])
        brt = jnp.zeros((ROUTE_LANES,), F32)
        brt = brt.at[:N_GROUPS].set(moe_b_group[l]).at[N_GROUPS:N_GROUPS + N_EXPERTS].set(moe_b_route[l])
        x3, h_tiles, rt3, counts = _mix_xa_router(
            x, yr3, g3, ya3, row(att_out_norm[l]), w_out[l].astype(BF16), row(norm_xa[l]),
            xa_w_q[l].astype(BF16), row(xa_q_norm[l]), km.reshape(B, M, XA_WIDTH), vm.reshape(B, M, XA_WIDTH),
            xa_w_o[l].astype(BF16), row(norm_ffn[l]), wrt.astype(BF16), row(brt))
        rt2d = rt3.reshape(T, ROUTE_LANES)
        dest, block_expert, n_used, n_slots = _moe_plan(rt2d, counts)
        xs = _dispatch(dest, h_tiles, n_slots)
        yb = _expert_ffn(block_expert, n_used, xs, l, moe_w_gate, moe_w_up, moe_w_down)
        x = _combine(dest, x3.reshape(T, D), rt2d, yb).reshape(B, S, D)
    return x
```

```python
import functools

import jax
import jax.numpy as jnp
from jax import lax
from jax.experimental import pallas as pl
from jax.experimental.pallas import tpu as pltpu

F32 = jnp.float32
BF16 = jnp.bfloat16

CHUNK = 64
RWKV_HEADS = 8
HEAD_DIM = 64
RWKV_WIDTH = 512
ATT_HEADS = 8
ATT_WIDTH = 512
DECAY_LORA = 64
AAA_LORA = 64
MV_LORA = 32
GATE_LORA = 128
RWKV_IN = 3 * RWKV_WIDTH + DECAY_LORA + AAA_LORA + GATE_LORA
LORA_OFF = 3 * RWKV_WIDTH
GATE_OFF = LORA_OFF + DECAY_LORA + AAA_LORA
IN_WIDTH = RWKV_IN + 3 * ATT_WIDTH
PREV_CHUNKS = 8
BAND = (PREV_CHUNKS + 1) * CHUNK
BAND_PAD = PREV_CHUNKS * CHUNK
REL_MAX = 256
XA_HEADS = 4
XA_HEAD_DIM = 128
XA_WIDTH = 512
N_GROUPS = 4
EXPERTS_PER_GROUP = 8
N_EXPERTS = 32
TOP_K = 2
EXPERT_FF = 512
MOE_BLOCK = 512
NORM_EPS = 1e-6
LNX_EPS = 64e-5
NEG_INF = -1e30

LANES = 128
SUBLANES = 8
ROW_TILE = 256
INPROJ_ROWS = 512
MIX_ROWS = 512
MIX_GROUPS = 2
LOG2E = 1.4426950408889634
WKV_ROWS = 256
WKV_GROUP_CHUNKS = 2
WKV_SEQS = 2
NEUMANN_SQUARINGS =CHUNK.bit_length() - 2
ATT_ROWS = 256
ATT_GROUP = 2
MOE_ROWS = 512
ROUTE_LANES = 128
DMA_UNROLL = 8


def _cparams(semantics, vmem_mib):
    return pltpu.CompilerParams(dimension_semantics=semantics,
                                vmem_limit_bytes=vmem_mib * 1024 * 1024)


def _dot(a, b):
    return jnp.dot(a, b, preferred_element_type=F32)


def _dot_nt(a, b):
    return lax.dot_general(a, b, (((1,), (1,)), ((), ())), preferred_element_type=F32)


def _rms(x, g):
    ms = jnp.mean(x * x, axis=-1, keepdims=True)
    return x * lax.rsqrt(ms + NORM_EPS) * g


def _group_sum(x, ones_bd):
    hi = x.astype(BF16)
    lo = (x - hi.astype(F32)).astype(BF16)
    return _dot(hi, ones_bd) + _dot(lo, ones_bd)


def _group_sumsq(x, ones_bd):
    return _group_sum(x * x, ones_bd)


def _inproj_kernel(x_ref, g_ref, w_ref, gq_ref, gk_ref, u_ref, q_ref, k_ref, v_ref):
    h = _rms(x_ref[...], g_ref[...]).astype(BF16)
    for j in range(0, RWKV_IN, 256):
        u_ref[:, j:j + 256] = _dot(h, w_ref[:, j:j + 256])
    even = lax.broadcasted_iota(jnp.int32, (x_ref.shape[0], LANES), 1) < HEAD_DIM

    def head_rms(x, gain_ref, scale):
        for j in range(0, ATT_WIDTH, LANES):
            xs = x[:, j:j + LANES]
            sq = xs * xs
            ss = jnp.where(even, jnp.sum(jnp.where(even, sq, 0.0), axis=-1, keepdims=True),
                           jnp.sum(jnp.where(even, 0.0, sq), axis=-1, keepdims=True))
            yield j, xs * lax.rsqrt(ss * (1.0 / HEAD_DIM) + NORM_EPS) * (gain_ref[:, j:j + LANES] * scale)

    q = _dot(h, w_ref[:, RWKV_IN:RWKV_IN + ATT_WIDTH])
    for j, qn in head_rms(q, gq_ref, HEAD_DIM ** -0.5 * LOG2E):
        q_ref[:, j:j + LANES] = qn.astype(BF16)
    k = _dot(h, w_ref[:, RWKV_IN + ATT_WIDTH:RWKV_IN + 2 * ATT_WIDTH])
    for j, kn in head_rms(k, gk_ref, 1.0):
        k_ref[:, j:j + LANES] = kn.astype(BF16)
    v_ref[...] = _dot(h, w_ref[:, RWKV_IN + 2 * ATT_WIDTH:]).astype(BF16)


def _inproj(x2d, g, w, gq, gk):
    T, D = x2d.shape
    tm = INPROJ_ROWS
    full = lambda i: (0, 0)
    row = lambda i: (i, 0)
    return pl.pallas_call(
        _inproj_kernel,
        grid=(T // tm,),
        in_specs=[pl.BlockSpec((tm, D), row), pl.BlockSpec((1, D), full),
                  pl.BlockSpec((D, IN_WIDTH), full), pl.BlockSpec((1, ATT_WIDTH), full),
                  pl.BlockSpec((1, ATT_WIDTH), full)],
        out_specs=[pl.BlockSpec((tm, RWKV_IN), row), pl.BlockSpec((tm, ATT_WIDTH), row),
                   pl.BlockSpec((tm, ATT_WIDTH), row), pl.BlockSpec((tm, ATT_WIDTH), row)],
        out_shape=[jax.ShapeDtypeStruct((T, RWKV_IN), F32), jax.ShapeDtypeStruct((T, ATT_WIDTH), BF16),
                   jax.ShapeDtypeStruct((T, ATT_WIDTH), BF16), jax.ShapeDtypeStruct((T, ATT_WIDTH), BF16)],
        compiler_params=_cparams(("parallel",), 48),
        name="inproj",
    )(x2d, g, w, gq, gk)


def _rwkv_prep_kernel(has_vres, *refs):
    if has_vres:
        (u_ref, up_ref, mu_ref, w0_ref, w2_ref, a0_ref, a2_ref, g2_ref, kkp_ref, kap_ref, rkp_ref,
         bd_ref, tri_ref, vf_ref, v0_ref, v1_ref, v2_ref,
         at_ref, rt_ref, bt_ref, kt_ref, vb_ref, ge_ref, bonus_ref, g_ref) = refs
    else:
        (u_ref, up_ref, mu_ref, w0_ref, w2_ref, a0_ref, a2_ref, g2_ref, kkp_ref, kap_ref, rkp_ref,
         bd_ref, tri_ref,
         at_ref, rt_ref, bt_ref, kt_ref, vb_ref, ge_ref, bonus_ref, g_ref, v_ref) = refs
    i = pl.program_id(1)
    u = u_ref[...]
    ts = u.shape[0]
    prev_row = jnp.where(i > 0, up_ref[SUBLANES - 1:SUBLANES, :], 0.0)
    rolled = pltpu.roll(u, 1, 0)
    row_id = lax.broadcasted_iota(jnp.int32, u.shape, 0)
    shifted = jnp.where(row_id == 0, jnp.broadcast_to(prev_row, u.shape), rolled)
    u = u + mu_ref[...] * (shifted - u)

    r = u[:, 0:RWKV_WIDTH]
    k = u[:, RWKV_WIDTH:2 * RWKV_WIDTH]
    v = u[:, 2 * RWKV_WIDTH:3 * RWKV_WIDTH]
    lora = u[:, LORA_OFF:GATE_OFF]
    gl = u[:, GATE_OFF:RWKV_IN]

    w = w0_ref[...] + _dot(jnp.tanh(lora).astype(BF16), w2_ref[...])
    z = -w
    softplus = jnp.maximum(z, 0.0) + jnp.log(1.0 + jnp.exp(-jnp.abs(z)))
    w = -softplus - 0.5
    log_decay = -jnp.exp(w)
    a = jax.nn.sigmoid(a0_ref[...] + _dot(lora.astype(BF16), a2_ref[...]))
    g_ref[...] = _dot(jax.nn.sigmoid(gl).astype(BF16), g2_ref[...])
    if has_vres:
        mix = _dot(_dot(v.astype(BF16), v1_ref[...]).astype(BF16), v2_ref[...])
        v = v + (vf_ref[...] - v) * jax.nn.sigmoid(v0_ref[...] + mix)
    else:
        v_ref[...] = v

    bd = bd_ref[...]
    kk = k * kkp_ref[...]
    kk = kk * lax.rsqrt(jnp.maximum(_group_sumsq(kk, bd), 1e-24))
    k2 = k * (1.0 + (a - 1.0) * kap_ref[...])
    bonus_ref[...] = _group_sum(r * k2 * rkp_ref[...], bd) * v

    hi = log_decay.astype(BF16)
    lo = (log_decay - hi.astype(F32)).astype(BF16)
    tri = tri_ref[...]
    cum = _dot(tri, hi) + _dot(tri, lo)
    inv_gamma = jnp.exp(-cum)
    at_ref[...] = (-kk * jnp.exp(cum - log_decay)).astype(BF16)
    rt_ref[...] = (r * jnp.exp(cum)).astype(BF16)
    bt_ref[...] = (kk * a * inv_gamma).astype(BF16)
    kt_ref[...] = (k2 * inv_gamma).astype(BF16)
    vb_ref[...] = v.astype(BF16)
    for c in range(ts // CHUNK):
        ge_ref[c] = jnp.exp(jnp.sum(log_decay[c * CHUNK:(c + 1) * CHUNK, :], axis=0, keepdims=True))


def _rwkv_prep(u3, mu, w0, w2p, a0, a2p, g2, kkp, kap, rkp, ones_bd, vres):
    B, S, _ = u3.shape
    ts = ROW_TILE
    W = RWKV_WIDTH
    n_c = ts // CHUNK
    full2 = lambda b, i: (0, 0)
    tile = lambda b, i: (b, i, 0)
    prev = lambda b, i: (b, jnp.maximum(i * (ts // SUBLANES) - 1, 0), 0)
    tri = (jnp.tril(jnp.ones((ts, ts), F32))
           * jnp.kron(jnp.eye(n_c, dtype=F32), jnp.ones((CHUNK, CHUNK), F32))).astype(BF16)
    in_specs = [pl.BlockSpec((None, ts, RWKV_IN), tile),
                pl.BlockSpec((None, SUBLANES, RWKV_IN), prev),
                pl.BlockSpec((1, RWKV_IN), full2), pl.BlockSpec((1, W), full2),
                pl.BlockSpec((LANES, W), full2), pl.BlockSpec((1, W), full2),
                pl.BlockSpec((LANES, W), full2), pl.BlockSpec((GATE_LORA, W), full2),
                pl.BlockSpec((1, W), full2), pl.BlockSpec((1, W), full2), pl.BlockSpec((1, W), full2),
                pl.BlockSpec((W, W), full2), pl.BlockSpec((ts, ts), full2)]
    args = [u3, u3, mu, w0, w2p, a0, a2p, g2, kkp, kap, rkp, ones_bd, tri]
    if vres is not None:
        v_first, v0, v1p, v2p = vres
        in_specs += [pl.BlockSpec((None, ts, W), tile), pl.BlockSpec((1, W), full2),
                     pl.BlockSpec((W, LANES), full2), pl.BlockSpec((LANES, W), full2)]
        args += [v_first, v0, v1p, v2p]
    tok = pl.BlockSpec((None, ts, W), tile)
    out_specs = [tok] * 5 + [pl.BlockSpec((None, n_c, 1, W), lambda b, i: (b, i, 0, 0)), tok, tok]
    out_shape = ([jax.ShapeDtypeStruct((B, S, W), BF16)] * 5
                 + [jax.ShapeDtypeStruct((B, S // CHUNK, 1, W), F32)]
                 + [jax.ShapeDtypeStruct((B, S, W), F32)] * 2)
    if vres is None:
        out_specs.append(tok)
        out_shape.append(jax.ShapeDtypeStruct((B, S, W), F32))
    return pl.pallas_call(
        functools.partial(_rwkv_prep_kernel, vres is not None),
        grid=(B, S // ts),
        in_specs=in_specs,
        out_specs=out_specs,
        out_shape=out_shape,
        compiler_params=_cparams(("parallel", "parallel"), 48),
        name="rwkv_prep",
    )(*args)


def _wkv_chunk_kernel(at_ref, rt_ref, bt_ref, kt_ref, vb_ref, ge_ref, bonus_ref, lw_ref, lb_ref,
                      y_ref, n_ref):
    C = CHUNK
    n_pairs = RWKV_HEADS // 2
    n_seqs = at_ref.shape[0]
    n_chunks = at_ref.shape[1] // C

    @pl.when(pl.program_id(1) == 0)
    def _():
        n_ref[...] = jnp.zeros_like(n_ref)

    row = lax.broadcasted_iota(jnp.int32, (2 * C, LANES), 0)
    lane = lax.broadcasted_iota(jnp.int32, (2 * C, LANES), 1)
    top = row < C
    left = lane < HEAD_DIM
    same = top == left
    t_row = jnp.where(top, row, row - C)
    s_col = jnp.where(left, lane, lane - HEAD_DIM)
    strict = s_col < t_row
    incl = s_col <= t_row
    even = lax.broadcasted_iota(jnp.int32, (C, LANES), 1) < HEAD_DIM
    zeros_c = jnp.zeros((C, LANES), BF16)

    unit_groups = [[(c, b, p) for c in range(c0, c0 + WKV_GROUP_CHUNKS) for b in range(n_seqs)
                    for p in range(n_pairs)] for c0 in range(0, n_chunks, WKV_GROUP_CHUNKS)]

    def tile(ref, unit):
        c, b, p = unit
        return ref[b, c * C:(c + 1) * C, p * LANES:(p + 1) * LANES]

    def stack_heads(x):
        z = jnp.zeros_like(x)
        return jnp.concatenate([jnp.where(even, x, z), jnp.where(even, z, x)], axis=0)

    def half_sum(x):
        lo = jnp.sum(jnp.where(even, x, 0.0), axis=-1, keepdims=True)
        hi = jnp.sum(jnp.where(even, 0.0, x), axis=-1, keepdims=True)
        return jnp.where(even, lo, hi)

    for units in unit_groups:
        at2 = [stack_heads(tile(at_ref, un)) for un in units]
        bk = [jnp.concatenate([tile(bt_ref, un), tile(kt_ref, un)], axis=0) for un in units]
        g4 = [_dot_nt(jnp.concatenate([a, stack_heads(tile(rt_ref, un))], axis=0), b)
              for un, a, b in zip(units, at2, bk)]
        ga = [jnp.where(strict, g[:2 * C], 0.0) for g in g4]
        gr = [jnp.where(incl, g[2 * C:], 0.0).astype(BF16) for g in g4]
        zs = [jnp.where(same, _dot(g.astype(BF16), jnp.concatenate([zeros_c, tile(vb_ref, un)], axis=0)), 0.0)
              for un, g in zip(units, ga)]
        pw = [jnp.where(same, jnp.where(top, g, pltpu.roll(g, HEAD_DIM, 1)), 0.0).astype(BF16) for g in ga]
        w = [jnp.concatenate([a.astype(F32), z], axis=1) for a, z in zip(at2, zs)]
        w = [x + _dot(m, x.astype(BF16)) for m, x in zip(pw, w)]
        for _ in range(NEUMANN_SQUARINGS):
            pw = [_dot(m, m).astype(BF16) for m in pw]
            w = [x + _dot(m, x.astype(BF16)) for m, x in zip(pw, w)]
        pm = [(x[:C, :LANES] + x[C:, :LANES]).astype(BF16) for x in w]
        q = [x[:C, LANES:] + x[C:, LANES:] for x in w]

        for i, un in enumerate(units):
            c, b, p = un
            cols = slice(p * LANES, (p + 1) * LANES)
            rows = slice(c * C, (c + 1) * C)
            n0 = n_ref[b, p]
            pr = _dot_nt(jnp.concatenate([pm[i], tile(rt_ref, un)], axis=0), n0.astype(BF16))
            u = pr[:C] + q[i]
            uv = jnp.concatenate([u.astype(BF16), tile(vb_ref, un)], axis=0)
            yp = _dot(gr[i], uv)
            y = pr[C:] + jnp.where(even, yp[:C], yp[C:])
            g_slab = ge_ref[b, c, :, cols]
            bkg = (bk[i].astype(F32) * g_slab).astype(BF16)
            dn = lax.dot_general(uv, bkg, (((0,), (0,)), ((), ())), preferred_element_type=F32)
            n_ref[b, p] = g_slab * n0 + jnp.where(same, dn, 0.0)
            mean = half_sum(y) * (1.0 / HEAD_DIM)
            yc = y - mean
            var = half_sum(yc * yc) * (1.0 / HEAD_DIM)
            yn = yc * lax.rsqrt(var + LNX_EPS) * lw_ref[:, cols] + lb_ref[:, cols]
            y_ref[b, rows, cols] = yn + bonus_ref[b, rows, cols]


def _wkv_chunked(at, rt, bt, kt, vb, ge, bonus, lnw, lnb):
    B, S, W = at.shape
    ts = WKV_ROWS
    nb = WKV_SEQS if B % WKV_SEQS == 0 else 1
    n_c = ts // CHUNK
    tile = lambda b, i: (b, i, 0)
    full = lambda b, i: (0, 0)
    tok = pl.BlockSpec((nb, ts, W), tile)
    return pl.pallas_call(
        _wkv_chunk_kernel,
        grid=(B // nb, S // ts),
        in_specs=[tok] * 5 + [pl.BlockSpec((nb, n_c, 1, W), lambda b, i: (b, i, 0, 0)), tok,
                              pl.BlockSpec((1, W), full), pl.BlockSpec((1, W), full)],
        out_specs=tok,
        out_shape=jax.ShapeDtypeStruct((B, S, W), F32),
        scratch_shapes=[pltpu.VMEM((nb, RWKV_HEADS // 2, 2 * HEAD_DIM, LANES), F32)],
        compiler_params=_cparams(("parallel", "arbitrary"), 32),
        name="wkv_chunked",
    )(at, rt, bt, kt, vb, ge, bonus, lnw, lnb)


def _band_attn_kernel(q_ref, k_ref, v_ref, bias_ref, o_ref, s_ref, e_ref):
    i = pl.program_id(1)
    n_chunks = q_ref.shape[0] // CHUNK
    n_pairs = ATT_HEADS // 2
    lane = lax.broadcasted_iota(jnp.int32, (CHUNK, LANES), 1)
    even = lane < HEAD_DIM
    slot = lax.broadcasted_iota(jnp.int32, (1, 1, BAND), 2)

    group = s_ref.shape[0]

    def chunk_body(masked, cc, carry):
        cgs = [i * n_chunks + cc * group + j for j in range(group)]
        starts = [pl.multiple_of(cg * CHUNK, CHUNK) for cg in cgs]
        rows = [pl.multiple_of((cc * group + j) * CHUNK, CHUNK) for j in range(group)]
        for j in range(group):
            for p in range(n_pairs):
                cols = slice(p * LANES, (p + 1) * LANES)
                q2 = q_ref[pl.ds(rows[j], CHUNK), cols]
                zero = jnp.zeros_like(q2)
                qs = jnp.concatenate([jnp.where(even, q2, zero), jnp.where(even, zero, q2)], axis=0)
                s_ref[j, p] = _dot_nt(qs, k_ref[pl.ds(starts[j], BAND), cols]) + bias_ref[p]
        inv = []
        for j in range(group):
            s = s_ref[j]
            if masked:
                s = jnp.where(slot >= (BAND_PAD - cgs[j] * CHUNK), s, NEG_INF)
            e = jnp.exp2(s - jnp.max(s, axis=-1, keepdims=True))
            inv.append(1.0 / jnp.sum(e, axis=-1, keepdims=True))
            e_ref[j] = e.astype(BF16)
        for j in range(group):
            for p in range(n_pairs):
                cols = slice(p * LANES, (p + 1) * LANES)
                o = _dot(e_ref[j, p], v_ref[pl.ds(starts[j], BAND), cols]) * inv[j][p]
                o_ref[pl.ds(rows[j], CHUNK), cols] = jnp.where(even, o[:CHUNK], o[CHUNK:])
        return carry

    @pl.when(i * n_chunks < PREV_CHUNKS)
    def _():
        lax.fori_loop(0, n_chunks // group, functools.partial(chunk_body, True), 0)

    @pl.when(i * n_chunks >= PREV_CHUNKS)
    def _():
        lax.fori_loop(0, n_chunks // group, functools.partial(chunk_body, False), 0)


def _band_attn(q3, kp3, vp3, bias_pairs):
    B, S, W = q3.shape
    SP = kp3.shape[1]
    tq = ATT_ROWS
    assert PREV_CHUNKS % (tq // CHUNK) == 0
    n_pairs = ATT_HEADS // 2
    return pl.pallas_call(
        _band_attn_kernel,
        grid=(B, S // tq),
        in_specs=[pl.BlockSpec((None, tq, W), lambda b, i: (b, i, 0)),
                  pl.BlockSpec((None, SP, W), lambda b, i: (b, 0, 0)),
                  pl.BlockSpec((None, SP, W), lambda b, i: (b, 0, 0)),
                  pl.BlockSpec((ATT_HEADS // 2, 2 * CHUNK, BAND), lambda b, i: (0, 0, 0))],
        out_specs=pl.BlockSpec((None, tq, W), lambda b, i: (b, i, 0)),
        out_shape=jax.ShapeDtypeStruct((B, S, W), F32),
        scratch_shapes=[pltpu.VMEM((ATT_GROUP, n_pairs, 2 * CHUNK, BAND), F32),
                        pltpu.VMEM((ATT_GROUP, n_pairs, 2 * CHUNK, BAND), BF16)],
        compiler_params=_cparams(("parallel", "arbitrary"), 48),
        name="band_attn",
    )(q3, kp3, vp3, bias_pairs)


def _mem_kv_kernel(m_ref, g_ref, w_ref, kn_ref, k_ref, v_ref):
    h = _rms(m_ref[...], g_ref[...]).astype(BF16)
    kv = _dot(h, w_ref[...])
    for hd in range(XA_HEADS):
        cols = slice(hd * XA_HEAD_DIM, (hd + 1) * XA_HEAD_DIM)
        k_ref[:, cols] = _rms(kv[:, cols], kn_ref[...]).astype(BF16)
    v_ref[...] = kv[:, XA_WIDTH:].astype(BF16)


def _mem_kv(mem2d, g, w_kv, k_norm):
    R, D = mem2d.shape
    tm = ROW_TILE
    full = lambda i: (0, 0)
    row = lambda i: (i, 0)
    return pl.pallas_call(
        _mem_kv_kernel,
        grid=(R // tm,),
        in_specs=[pl.BlockSpec((tm, D), row), pl.BlockSpec((1, D), full),
                  pl.BlockSpec((D, 2 * XA_WIDTH), full), pl.BlockSpec((1, XA_HEAD_DIM), full)],
        out_specs=[pl.BlockSpec((tm, XA_WIDTH), row)] * 2,
        out_shape=[jax.ShapeDtypeStruct((R, XA_WIDTH), BF16)] * 2,
        compiler_params=_cparams(("parallel",), 32),
        name="mem_kv",
    )(mem2d, g, w_kv, k_norm)


def _mix_xa_router_kernel(x_ref, yr_ref, g_ref, ya_ref, gao_ref, wout_ref, gxa_ref, wq_ref, qn_ref,
                          km_ref, vm_ref, wo_ref, gff_ref, wrt_ref, brt_ref,
                          tri_ref, xo_ref, h_ref, rt_ref, cnt_ref, run_ref, stage_ref):
    first = (pl.program_id(0) == 0) & (pl.program_id(1) == 0)

    @pl.when(first)
    def _():
        run_ref[...] = jnp.zeros_like(run_ref)

    n_rows = x_ref.shape[0]
    groups = [slice(r, r + n_rows // MIX_GROUPS) for r in range(0, n_rows, n_rows // MIX_GROUPS)]
    head_cols = [slice(hd * XA_HEAD_DIM, (hd + 1) * XA_HEAD_DIM) for hd in range(XA_HEADS)]

    yr = [(yr_ref[rows, :] * g_ref[rows, :]).astype(BF16) for rows in groups]
    ya = [_rms(ya_ref[rows, :], gao_ref[...]).astype(BF16) for rows in groups]
    x1 = [x_ref[rows, :] + _dot(a, wout_ref[0:RWKV_WIDTH, :]) + _dot(b, wout_ref[RWKV_WIDTH:, :])
          for rows, a, b in zip(groups, yr, ya)]

    q = [_dot(_rms(xg, gxa_ref[...]).astype(BF16), wq_ref[...]) for xg in x1]
    qh = [[_rms(qg[:, cols], qn_ref[...]).astype(BF16) for cols in head_cols] for qg in q]
    s = [[_dot_nt(qg[hd], km_ref[:, cols]) * (XA_HEAD_DIM ** -0.5) for hd, cols in enumerate(head_cols)]
         for qg in qh]
    e = [[jnp.exp(sh - jnp.max(sh, axis=-1, keepdims=True)) for sh in sg] for sg in s]
    pr = [[(eh / jnp.sum(eh, axis=-1, keepdims=True)).astype(BF16) for eh in eg] for eg in e]
    o = [jnp.concatenate([_dot(pg[hd], vm_ref[:, cols]) for hd, cols in enumerate(head_cols)],
                         axis=-1).astype(BF16) for pg in pr]
    x = jnp.concatenate([xg + _dot(og, wo_ref[...]) for xg, og in zip(x1, o)], axis=0)
    xo_ref[...] = x

    h = _rms(x, gff_ref[...])
    tm = h.shape[0]
    _store_row_tiles(stage_ref, h)
    h_ref[...] = stage_ref[...].astype(BF16)
    logits =_dot(h.astype(BF16), wrt_ref[...]) + brt_ref[...]
    lane_i = lax.broadcasted_iota(jnp.int32, logits.shape, 1)
    lane = lane_i.astype(F32)
    big = float(ROUTE_LANES)
    gmask = lane < N_GROUPS
    gmax = jnp.max(jnp.where(gmask, logits, -jnp.inf), axis=-1, keepdims=True)
    ge = jnp.where(gmask, jnp.exp(logits - gmax), 0.0)
    gp = ge / jnp.sum(ge, axis=-1, keepdims=True)
    g_gate = jnp.max(gp, axis=-1, keepdims=True)
    g_top = jnp.min(jnp.where(gmask & (gp == g_gate), lane, big), axis=-1, keepdims=True)
    lo = N_GROUPS + g_top * EXPERTS_PER_GROUP
    rmask = (lane >= lo) & (lane < lo + EXPERTS_PER_GROUP)
    rmax = jnp.max(jnp.where(rmask, logits, -jnp.inf), axis=-1, keepdims=True)
    re = jnp.where(rmask, jnp.exp(logits - rmax), 0.0)
    p_in = re / jnp.sum(re, axis=-1, keepdims=True)
    p1 = jnp.max(jnp.where(rmask, p_in, -1.0), axis=-1, keepdims=True)
    j1 = jnp.min(jnp.where(rmask & (p_in == p1), lane, big), axis=-1, keepdims=True)
    rmask2 = rmask & (lane != j1)
    p2 = jnp.max(jnp.where(rmask2, p_in, -1.0), axis=-1, keepdims=True)
    j2 = jnp.min(jnp.where(rmask2 & (p_in == p2), lane, big), axis=-1, keepdims=True)
    denom = p1 + p2
    w1 = g_gate * (p1 / denom)
    w2 = g_gate * (p2 / denom)
    e1 = j1 - N_GROUPS
    e2 = j2 - N_GROUPS
    hit1 = lane == j1
    hit2 = lane == j2
    hits = jnp.where(hit1 | hit2, 1.0, 0.0)
    before = _dot(tri_ref[...], hits.astype(BF16)) + run_ref[...]
    rank1 = jnp.sum(jnp.where(hit1, before, 0.0), axis=-1, keepdims=True)
    rank2 = jnp.sum(jnp.where(hit2, before, 0.0), axis=-1, keepdims=True)
    run = run_ref[...] + jnp.sum(hits, axis=0, keepdims=True)
    run_ref[...] = run
    cnt_ref[...] = jnp.broadcast_to(run, cnt_ref.shape)
    rt_ref[...] = jnp.where(lane_i == 0, e1, jnp.where(lane_i == 1, e2,
                            jnp.where(lane_i == 2, w1, jnp.where(lane_i == 3, w2,
                                      jnp.where(lane_i == 4, rank1, jnp.where(lane_i == 5, rank2, 0.0))))))


def _mix_xa_router(x3, yr3, g3, ya3, gao, wout, gxa, wq, qn, km3, vm3, wo, gff, wrt, brt):
    B, S, D = x3.shape
    M = km3.shape[1]
    tm = MIX_ROWS
    tile = lambda b, i: (b, i, 0)
    full = lambda b, i: (0, 0)
    memb = lambda b, i: (b, 0, 0)
    W = RWKV_WIDTH
    n_i = S // tm
    tri = jnp.tril(jnp.ones((tm, tm), F32), -1).astype(BF16)
    return pl.pallas_call(
        _mix_xa_router_kernel,
        grid=(B, S // tm),
        in_specs=[pl.BlockSpec((None, tm, D), tile), pl.BlockSpec((None, tm, W), tile),
                  pl.BlockSpec((None, tm, W), tile), pl.BlockSpec((None, tm, W), tile),
                  pl.BlockSpec((1, W), full), pl.BlockSpec((2 * W, D), full),
                  pl.BlockSpec((1, D), full), pl.BlockSpec((D, XA_WIDTH), full),
                  pl.BlockSpec((1, XA_HEAD_DIM), full),
                  pl.BlockSpec((None, M, XA_WIDTH), memb), pl.BlockSpec((None, M, XA_WIDTH), memb),
                  pl.BlockSpec((XA_WIDTH, D), full), pl.BlockSpec((1, D), full),
                  pl.BlockSpec((D, ROUTE_LANES), full), pl.BlockSpec((1, ROUTE_LANES), full),
                  pl.BlockSpec((tm, tm), full)],
        out_specs=[pl.BlockSpec((None, tm, D), tile),
                   pl.BlockSpec((tm * SUBLANES, LANES), lambda b, i: (b * n_i + i, 0)),
                   pl.BlockSpec((None, tm, ROUTE_LANES), tile),
                   pl.BlockSpec((SUBLANES, ROUTE_LANES), full)],
        out_shape=[jax.ShapeDtypeStruct((B, S, D), F32),
                   jax.ShapeDtypeStruct((B * S * SUBLANES, LANES), BF16),
                   jax.ShapeDtypeStruct((B, S, ROUTE_LANES), F32),
                   jax.ShapeDtypeStruct((SUBLANES, ROUTE_LANES), F32)],
        scratch_shapes=[pltpu.VMEM((1, ROUTE_LANES), F32), pltpu.VMEM((tm * SUBLANES, LANES), F32)],
        compiler_params=_cparams(("arbitrary", "arbitrary"), 48),
        name="mix_xa_router",
    )(x3, yr3, g3, ya3, gao, wout, gxa, wq, qn, km3, vm3, wo, gff, wrt, brt, tri)


def _row_copy(src_ref, src_off, dst_ref, dst_off, sem):
    src = src_ref.at[pl.ds(pl.multiple_of(src_off, SUBLANES), SUBLANES)]
    dst = dst_ref.at[pl.ds(pl.multiple_of(dst_off, SUBLANES), SUBLANES)]
    return pltpu.make_async_copy(src, dst, sem)


def _load_row_tiles(stage_ref, rows):
    return jnp.concatenate([stage_ref[pl.ds(j, rows, stride=SUBLANES), :] for j in range(SUBLANES)], axis=-1)


def _store_row_tiles(stage_ref, x):
    for j in range(SUBLANES):
        stage_ref[pl.ds(j, x.shape[0], stride=SUBLANES), :] = x[:, j * LANES:(j + 1) * LANES]


def _dispatch_kernel(dest_ref, h_ref, zeros_ref, xs_ref, sem):
    del zeros_ref
    tm = h_ref.shape[0] // SUBLANES
    base = pl.program_id(0) * (tm * TOP_K)

    def issue(r, carry):
        for kk in range(TOP_K):
            _row_copy(h_ref, r * SUBLANES, xs_ref, dest_ref[base + r * TOP_K + kk], sem).start(priority=kk)
        return carry

    lax.fori_loop(0, tm, issue, 0, unroll=DMA_UNROLL)

    def drain(r, carry):
        for kk in range(TOP_K):
            _row_copy(h_ref, r * SUBLANES, xs_ref, dest_ref[base + r * TOP_K + kk], sem).wait()
        return carry

    lax.fori_loop(0, tm, drain, 0, unroll=DMA_UNROLL)


def _dispatch(dest, h_tiles, n_slots):
    tm = MOE_ROWS
    T = h_tiles.shape[0] // SUBLANES
    zeros = jnp.zeros((n_slots * SUBLANES, LANES), BF16)
    return pl.pallas_call(
        _dispatch_kernel,
        grid_spec=pltpu.PrefetchScalarGridSpec(
            num_scalar_prefetch=1,
            grid=(T // tm,),
            in_specs=[pl.BlockSpec((tm * SUBLANES, LANES), lambda i, d: (i, 0)),
                      pl.BlockSpec(memory_space=pl.ANY)],
            out_specs=pl.BlockSpec(memory_space=pl.ANY),
            scratch_shapes=[pltpu.SemaphoreType.DMA(())],
        ),
        out_shape=jax.ShapeDtypeStruct((n_slots * SUBLANES, LANES), BF16),
        input_output_aliases={2: 0},
        compiler_params=_cparams(("arbitrary",), 32),
        name="moe_dispatch",
    )(dest, h_tiles, zeros)


def _expert_kernel(be_ref, nu_ref, xs_ref, wg_ref, wu_ref, wd_ref, y_ref, wgb_ref, wub_ref, wdb_ref,
                   stage_ref):
    i = pl.program_id(0)
    used = i < nu_ref[0]
    new_expert = (i == 0) | (be_ref[i] != be_ref[jnp.maximum(i - 1, 0)])

    @pl.when(used & new_expert)
    def _():
        wgb_ref[...] = wg_ref[...].astype(BF16)
        wub_ref[...] = wu_ref[...].astype(BF16)
        wdb_ref[...] = wd_ref[...].astype(BF16)

    @pl.when(used)
    def _():
        stage_ref[...] = xs_ref[...].astype(F32)
        xb = _load_row_tiles(stage_ref, MOE_BLOCK).astype(BF16)
        gate = _dot(xb, wgb_ref[...])
        up = _dot(xb, wub_ref[...])
        act = (gate * jax.nn.sigmoid(gate) * up).astype(BF16)
        _store_row_tiles(stage_ref, _dot(act, wdb_ref[...]))
        y_ref[...] = stage_ref[...].astype(BF16)

    @pl.when(jnp.logical_not(used))
    def _():
        y_ref[...] = jnp.zeros_like(y_ref)


def _expert_ffn(block_expert, n_used, xs_tiles, layer, wg, wu, wd):
    D, FF = wg.shape[2], wg.shape[3]
    n_blocks = xs_tiles.shape[0] // (MOE_BLOCK * SUBLANES)
    rows = MOE_BLOCK * SUBLANES
    blk = lambda i, be, nu: (jnp.minimum(i, nu[0] - 1), 0)
    wsel = lambda i, be, nu: (layer, be[jnp.minimum(i, nu[0] - 1)], 0, 0)
    return pl.pallas_call(
        _expert_kernel,
        grid_spec=pltpu.PrefetchScalarGridSpec(
            num_scalar_prefetch=2,
            grid=(n_blocks,),
            in_specs=[pl.BlockSpec((rows, LANES), blk),
                      pl.BlockSpec((None, None, D, FF), wsel),
                      pl.BlockSpec((None, None, D, FF), wsel),
                      pl.BlockSpec((None, None, FF, D), wsel)],
            out_specs=pl.BlockSpec((rows, LANES), lambda i, be, nu: (i, 0)),
            scratch_shapes=[pltpu.VMEM((D, FF), BF16), pltpu.VMEM((D, FF), BF16), pltpu.VMEM((FF, D), BF16),
                            pltpu.VMEM((rows, LANES), F32)],
        ),
        out_shape=jax.ShapeDtypeStruct(xs_tiles.shape, BF16),
        compiler_params=_cparams(("arbitrary",), 48),
        name="moe_experts",
    )(block_expert, n_used, xs_tiles, wg, wu, wd)


def _combine_kernel(dest_ref, x_ref, wt_ref, yb_ref, o_ref, buf_ref, sem, stage_ref):
    tm = x_ref.shape[0]
    base = pl.program_id(0) * (tm * TOP_K)

    def issue(r, carry):
        for kk in range(TOP_K):
            _row_copy(yb_ref, dest_ref[base + r * TOP_K + kk], buf_ref.at[kk], r * SUBLANES, sem).start(priority=kk)
        return carry

    lax.fori_loop(0, tm, issue, 0, unroll=DMA_UNROLL)

    def drain(r, carry):
        for kk in range(TOP_K):
            _row_copy(yb_ref, dest_ref[base + r * TOP_K + kk], buf_ref.at[kk], r * SUBLANES, sem).wait()
        return carry

    lax.fori_loop(0, tm, drain, 0, unroll=DMA_UNROLL)
    wt = wt_ref[...]
    w1 = wt[:, 2:3]
    w2 = wt[:, 3:4]
    stage_ref[...] = buf_ref[...].astype(F32)
    for j in range(SUBLANES):
        cols = slice(j * LANES, (j + 1) * LANES)
        y1 = stage_ref[0, pl.ds(j, tm, stride=SUBLANES), :]
        y2 = stage_ref[1, pl.ds(j, tm, stride=SUBLANES), :]
        o_ref[:, cols] = x_ref[:, cols] + (y1 * w1 + y2 * w2)


def _combine(dest, x2d, rt2d, yb_tiles):
    T, D = x2d.shape
    tm = MOE_ROWS
    return pl.pallas_call(
        _combine_kernel,
        grid_spec=pltpu.PrefetchScalarGridSpec(
            num_scalar_prefetch=1,
            grid=(T // tm,),
            in_specs=[pl.BlockSpec((tm, D), lambda i, d: (i, 0)),
                      pl.BlockSpec((tm, ROUTE_LANES), lambda i, d: (i, 0)),
                      pl.BlockSpec(memory_space=pl.ANY)],
            out_specs=pl.BlockSpec((tm, D), lambda i, d: (i, 0)),
            scratch_shapes=[pltpu.VMEM((TOP_K, tm * SUBLANES, LANES), BF16), pltpu.SemaphoreType.DMA(()),
                            pltpu.VMEM((TOP_K, tm * SUBLANES, LANES), F32)],
        ),
        out_shape=jax.ShapeDtypeStruct((T, D), F32),
        compiler_params=_cparams(("arbitrary",), 48),
        name="moe_combine",
    )(dest, x2d, rt2d, yb_tiles)


def _moe_plan(rt2d, counts_lanes):
    T = rt2d.shape[0]
    A = T * TOP_K
    e = rt2d[:, 0:TOP_K].astype(jnp.int32).reshape(A)
    rank = rt2d[:, 4:4 + TOP_K].astype(jnp.int32).reshape(A)
    counts = counts_lanes[0, N_GROUPS:N_GROUPS + N_EXPERTS].astype(jnp.int32)
    padded = ((counts + MOE_BLOCK - 1) // MOE_BLOCK) * MOE_BLOCK
    pends = jnp.cumsum(padded)
    pstarts = pends - padded
    onehot = e[:, None] == jnp.arange(N_EXPERTS, dtype=jnp.int32)[None, :]
    dest = jnp.sum(jnp.where(onehot, pstarts[None, :], 0), axis=1) + rank
    n_blocks = (A + N_EXPERTS * (MOE_BLOCK - 1) + MOE_BLOCK - 1) // MOE_BLOCK
    block_start = jnp.arange(n_blocks, dtype=jnp.int32) * MOE_BLOCK
    block_expert = jnp.minimum(jnp.sum(pends[None, :] <= block_start[:, None], axis=1), N_EXPERTS - 1)
    n_used = (pends[-1] // MOE_BLOCK).astype(jnp.int32).reshape(1)
    dest_off = (dest * SUBLANES).astype(jnp.int32)
    return dest_off, block_expert.astype(jnp.int32), n_used, n_blocks * MOE_BLOCK


def _rel_bias_pairs(table):
    n_ext = BAND + CHUNK - 1
    ext = jnp.concatenate([table, jnp.broadcast_to(table[:, -1:], (table.shape[0], n_ext - table.shape[1]))], axis=1)
    rev = ext[:, ::-1]
    bias = jnp.stack([rev[:, CHUNK - 1 - i:CHUNK - 1 - i + BAND] for i in range(CHUNK)], axis=1)
    return (bias.astype(F32) * LOG2E).reshape(-1, ATT_HEADS // 2, 2 * CHUNK, BAND)


def _pad_rows(w, rows, offset=0):
    out = jnp.zeros((rows, w.shape[1]), w.dtype)
    return out.at[offset:offset + w.shape[0]].set(w)


def kernel(x, mem, norm_mix, w_in, shift_mu, decay_w0, decay_w2, iclr_a0, iclr_a2, gate_g2, rw_k_k, rw_k_a, rw_r_k, lnx_w, lnx_b, vres_v0, vres_v1, vres_v2, att_q_norm, att_k_norm, att_rel_bias, att_out_norm, w_out, norm_xa, norm_mem, xa_w_q, xa_w_kv, xa_w_o, xa_q_norm, xa_k_norm, norm_ffn, moe_w_group, moe_b_group, moe_w_route, moe_b_route, moe_w_gate, moe_w_up, moe_w_down):
    B, S, D = x.shape
    T = B * S
    depth = w_in.shape[0]
    M = mem.shape[1]
    row = lambda p: p.reshape(1, -1)
    ones_bd = jnp.kron(jnp.eye(ATT_HEADS, dtype=F32), jnp.ones((HEAD_DIM, HEAD_DIM), F32)).astype(BF16)

    bias_pairs = _rel_bias_pairs(att_rel_bias.reshape(depth * ATT_HEADS, -1))

    v_first = None
    for l in range(depth):
        u, qh, kh, vh = _inproj(x.reshape(T, D), row(norm_mix[l]), w_in[l].astype(BF16),
                                row(jnp.tile(att_q_norm[l], ATT_HEADS)),
                                row(jnp.tile(att_k_norm[l], ATT_HEADS)))
        w2p = _pad_rows(decay_w2[l], LANES, 0).astype(BF16)
        a2p = _pad_rows(iclr_a2[l], LANES, DECAY_LORA).astype(BF16)
        vres = None
        if l > 0:
            v1p = jnp.zeros((RWKV_WIDTH, LANES), F32).at[:, :MV_LORA].set(vres_v1[l - 1]).astype(BF16)
            v2p = _pad_rows(vres_v2[l - 1], LANES, 0).astype(BF16)
            vres = (v_first, row(vres_v0[l - 1]), v1p, v2p)
        prep = _rwkv_prep(u.reshape(B, S, RWKV_IN), row(shift_mu[l]), row(decay_w0[l]),
                          w2p, row(iclr_a0[l]), a2p, gate_g2[l].astype(BF16),
                          row(rw_k_k[l]), row(rw_k_a[l]), row(rw_r_k[l]), ones_bd, vres)
        at3, rt3, bt3, kt3, vb3, ge4, bonus3, g3 = prep[:8]
        if l == 0:
            v_first = prep[8]
        yr3 = _wkv_chunked(at3, rt3, bt3, kt3, vb3, ge4, bonus3, row(lnx_w[l]), row(lnx_b[l]))
        pad = ((0, 0), (BAND_PAD, 0), (0, 0))
        ya3 = _band_attn(qh.reshape(B, S, ATT_WIDTH), jnp.pad(kh.reshape(B, S, ATT_WIDTH), pad),
                         jnp.pad(vh.reshape(B, S, ATT_WIDTH), pad), bias_pairs[l])
        km, vm = _mem_kv(mem.reshape(B * M, D), row(norm_mem[l]), xa_w_kv[l].astype(BF16), row(xa_k_norm[l]))
        wrt = jnp.zeros((D, ROUTE_LANES), F32)
        wrt = wrt.at[:, :N_GROUPS].set(moe_w_group[l]).at[:, N_GROUPS:N_GROUPS + N_EXPERTS].set(moe_w_route[l])
        brt = jnp.zeros((ROUTE_LANES,), F32)
        brt = brt.at[:N_GROUPS].set(moe_b_group[l]).at[N_GROUPS:N_GROUPS + N_EXPERTS].set(moe_b_route[l])
        x3, h_tiles, rt3, counts = _mix_xa_router(
            x, yr3, g3, ya3, row(att_out_norm[l]), w_out[l].astype(BF16), row(norm_xa[l]),
            xa_w_q[l].astype(BF16), row(xa_q_norm[l]), km.reshape(B, M, XA_WIDTH), vm.reshape(B, M, XA_WIDTH),
            xa_w_o[l].astype(BF16), row(norm_ffn[l]), wrt.astype(BF16), row(brt))
        rt2d = rt3.reshape(T, ROUTE_LANES)
        dest, block_expert, n_used, n_slots = _moe_plan(rt2d, counts)
        xs = _dispatch(dest, h_tiles, n_slots)
        yb = _expert_ffn(block_expert, n_used, xs, l, moe_w_gate, moe_w_up, moe_w_down)
        x = _combine(dest, x3.reshape(T, D), rt2d, yb).reshape(B, S, D)
    return x
```

```python
import functools

import jax
import jax.numpy as jnp
from jax import lax
from jax.experimental import pallas as pl
from jax.experimental.pallas import tpu as pltpu

F32 = jnp.float32
BF16 = jnp.bfloat16

CHUNK = 64
RWKV_HEADS = 8
HEAD_DIM = 64
RWKV_WIDTH = 512
ATT_HEADS = 8
ATT_WIDTH = 512
DECAY_LORA = 64
AAA_LORA = 64
MV_LORA = 32
GATE_LORA = 128
RWKV_IN = 3 * RWKV_WIDTH + DECAY_LORA + AAA_LORA + GATE_LORA
LORA_OFF = 3 * RWKV_WIDTH
GATE_OFF = LORA_OFF + DECAY_LORA + AAA_LORA
IN_WIDTH = RWKV_IN + 3 * ATT_WIDTH
PREV_CHUNKS = 8
BAND = (PREV_CHUNKS + 1) * CHUNK
BAND_PAD = PREV_CHUNKS * CHUNK
REL_MAX = 256
XA_HEADS = 4
XA_HEAD_DIM = 128
XA_WIDTH = 512
N_GROUPS = 4
EXPERTS_PER_GROUP = 8
N_EXPERTS = 32
TOP_K = 2
EXPERT_FF = 512
MOE_BLOCK = 512
NORM_EPS = 1e-6
LNX_EPS = 64e-5
NEG_INF = -1e30

LANES = 128
SUBLANES = 8
ROW_TILE = 256
INPROJ_ROWS = 512
MIX_ROWS = 512
MIX_GROUPS = 2
LOG2E = 1.4426950408889634
WKV_ROWS = 256
WKV_GROUP_CHUNKS = 2
WKV_SEQS = 2
NEUMANN_SQUARINGS =CHUNK.bit_length() - 2
ATT_ROWS = 256
ATT_GROUP = 2
MOE_ROWS = 512
ROUTE_LANES = 128
DMA_UNROLL = 8


def _cparams(semantics, vmem_mib):
    return pltpu.CompilerParams(dimension_semantics=semantics,
                                vmem_limit_bytes=vmem_mib * 1024 * 1024)


def _dot(a, b):
    return jnp.dot(a, b, preferred_element_type=F32)


def _dot_nt(a, b):
    return lax.dot_general(a, b, (((1,), (1,)), ((), ())), preferred_element_type=F32)


def _rms(x, g):
    ms = jnp.mean(x * x, axis=-1, keepdims=True)
    return x * lax.rsqrt(ms + NORM_EPS) * g


def _group_sum(x, ones_bd):
    hi = x.astype(BF16)
    lo = (x - hi.astype(F32)).astype(BF16)
    return _dot(hi, ones_bd) + _dot(lo, ones_bd)


def _group_sumsq(x, ones_bd):
    return _group_sum(x * x, ones_bd)


def _inproj_kernel(x_ref, g_ref, w_ref, gq_ref, gk_ref, u_ref, q_ref, k_ref, v_ref):
    h = _rms(x_ref[...], g_ref[...]).astype(BF16)
    for j in range(0, RWKV_IN, 256):
        u_ref[:, j:j + 256] = _dot(h, w_ref[:, j:j + 256])
    even = lax.broadcasted_iota(jnp.int32, (x_ref.shape[0], LANES), 1) < HEAD_DIM

    def head_rms(x, gain_ref, scale):
        for j in range(0, ATT_WIDTH, LANES):
            xs = x[:, j:j + LANES]
            sq = xs * xs
            ss = jnp.where(even, jnp.sum(jnp.where(even, sq, 0.0), axis=-1, keepdims=True),
                           jnp.sum(jnp.where(even, 0.0, sq), axis=-1, keepdims=True))
            yield j, xs * lax.rsqrt(ss * (1.0 / HEAD_DIM) + NORM_EPS) * (gain_ref[:, j:j + LANES] * scale)

    q = _dot(h, w_ref[:, RWKV_IN:RWKV_IN + ATT_WIDTH])
    for j, qn in head_rms(q, gq_ref, HEAD_DIM ** -0.5 * LOG2E):
        q_ref[:, j:j + LANES] = qn.astype(BF16)
    k = _dot(h, w_ref[:, RWKV_IN + ATT_WIDTH:RWKV_IN + 2 * ATT_WIDTH])
    for j, kn in head_rms(k, gk_ref, 1.0):
        k_ref[:, j:j + LANES] = kn.astype(BF16)
    v_ref[...] = _dot(h, w_ref[:, RWKV_IN + 2 * ATT_WIDTH:]).astype(BF16)


def _inproj(x2d, g, w, gq, gk):
    T, D = x2d.shape
    tm = INPROJ_ROWS
    full = lambda i: (0, 0)
    row = lambda i: (i, 0)
    return pl.pallas_call(
        _inproj_kernel,
        grid=(T // tm,),
        in_specs=[pl.BlockSpec((tm, D), row), pl.BlockSpec((1, D), full),
                  pl.BlockSpec((D, IN_WIDTH), full), pl.BlockSpec((1, ATT_WIDTH), full),
                  pl.BlockSpec((1, ATT_WIDTH), full)],
        out_specs=[pl.BlockSpec((tm, RWKV_IN), row), pl.BlockSpec((tm, ATT_WIDTH), row),
                   pl.BlockSpec((tm, ATT_WIDTH), row), pl.BlockSpec((tm, ATT_WIDTH), row)],
        out_shape=[jax.ShapeDtypeStruct((T, RWKV_IN), F32), jax.ShapeDtypeStruct((T, ATT_WIDTH), BF16),
                   jax.ShapeDtypeStruct((T, ATT_WIDTH), BF16), jax.ShapeDtypeStruct((T, ATT_WIDTH), BF16)],
        compiler_params=_cparams(("parallel",), 48),
        name="inproj",
    )(x2d, g, w, gq, gk)


def _rwkv_prep_kernel(has_vres, *refs):
    if has_vres:
        (u_ref, up_ref, mu_ref, w0_ref, w2_ref, a0_ref, a2_ref, g2_ref, kkp_ref, kap_ref, rkp_ref,
         bd_ref, tri_ref, vf_ref, v0_ref, v1_ref, v2_ref,
         at_ref, rt_ref, bt_ref, kt_ref, vb_ref, ge_ref, bonus_ref, g_ref) = refs
    else:
        (u_ref, up_ref, mu_ref, w0_ref, w2_ref, a0_ref, a2_ref, g2_ref, kkp_ref, kap_ref, rkp_ref,
         bd_ref, tri_ref,
         at_ref, rt_ref, bt_ref, kt_ref, vb_ref, ge_ref, bonus_ref, g_ref, v_ref) = refs
    i = pl.program_id(1)
    u = u_ref[...]
    ts = u.shape[0]
    prev_row = jnp.where(i > 0, up_ref[SUBLANES - 1:SUBLANES, :], 0.0)
    rolled = pltpu.roll(u, 1, 0)
    row_id = lax.broadcasted_iota(jnp.int32, u.shape, 0)
    shifted = jnp.where(row_id == 0, jnp.broadcast_to(prev_row, u.shape), rolled)
    u = u + mu_ref[...] * (shifted - u)

    r = u[:, 0:RWKV_WIDTH]
    k = u[:, RWKV_WIDTH:2 * RWKV_WIDTH]
    v = u[:, 2 * RWKV_WIDTH:3 * RWKV_WIDTH]
    lora = u[:, LORA_OFF:GATE_OFF]
    gl = u[:, GATE_OFF:RWKV_IN]

    w = w0_ref[...] + _dot(jnp.tanh(lora).astype(BF16), w2_ref[...])
    z = -w
    softplus = jnp.maximum(z, 0.0) + jnp.log(1.0 + jnp.exp(-jnp.abs(z)))
    w = -softplus - 0.5
    log_decay = -jnp.exp(w)
    a = jax.nn.sigmoid(a0_ref[...] + _dot(lora.astype(BF16), a2_ref[...]))
    g_ref[...] = _dot(jax.nn.sigmoid(gl).astype(BF16), g2_ref[...])
    if has_vres:
        mix = _dot(_dot(v.astype(BF16), v1_ref[...]).astype(BF16), v2_ref[...])
        v = v + (vf_ref[...] - v) * jax.nn.sigmoid(v0_ref[...] + mix)
    else:
        v_ref[...] = v

    bd = bd_ref[...]
    kk = k * kkp_ref[...]
    kk = kk * lax.rsqrt(jnp.maximum(_group_sumsq(kk, bd), 1e-24))
    k2 = k * (1.0 + (a - 1.0) * kap_ref[...])
    bonus_ref[...] = _group_sum(r * k2 * rkp_ref[...], bd) * v

    hi = log_decay.astype(BF16)
    lo = (log_decay - hi.astype(F32)).astype(BF16)
    tri = tri_ref[...]
    cum = _dot(tri, hi) + _dot(tri, lo)
    inv_gamma = jnp.exp(-cum)
    at_ref[...] = (-kk * jnp.exp(cum - log_decay)).astype(BF16)
    rt_ref[...] = (r * jnp.exp(cum)).astype(BF16)
    bt_ref[...] = (kk * a * inv_gamma).astype(BF16)
    kt_ref[...] = (k2 * inv_gamma).astype(BF16)
    vb_ref[...] = v.astype(BF16)
    for c in range(ts // CHUNK):
        ge_ref[c] = jnp.exp(jnp.sum(log_decay[c * CHUNK:(c + 1) * CHUNK, :], axis=0, keepdims=True))


def _rwkv_prep(u3, mu, w0, w2p, a0, a2p, g2, kkp, kap, rkp, ones_bd, vres):
    B, S, _ = u3.shape
    ts = ROW_TILE
    W = RWKV_WIDTH
    n_c = ts // CHUNK
    full2 = lambda b, i: (0, 0)
    tile = lambda b, i: (b, i, 0)
    prev = lambda b, i: (b, jnp.maximum(i * (ts // SUBLANES) - 1, 0), 0)
    tri = (jnp.tril(jnp.ones((ts, ts), F32))
           * jnp.kron(jnp.eye(n_c, dtype=F32), jnp.ones((CHUNK, CHUNK), F32))).astype(BF16)
    in_specs = [pl.BlockSpec((None, ts, RWKV_IN), tile),
                pl.BlockSpec((None, SUBLANES, RWKV_IN), prev),
                pl.BlockSpec((1, RWKV_IN), full2), pl.BlockSpec((1, W), full2),
                pl.BlockSpec((LANES, W), full2), pl.BlockSpec((1, W), full2),
                pl.BlockSpec((LANES, W), full2), pl.BlockSpec((GATE_LORA, W), full2),
                pl.BlockSpec((1, W), full2), pl.BlockSpec((1, W), full2), pl.BlockSpec((1, W), full2),
                pl.BlockSpec((W, W), full2), pl.BlockSpec((ts, ts), full2)]
    args = [u3, u3, mu, w0, w2p, a0, a2p, g2, kkp, kap, rkp, ones_bd, tri]
    if vres is not None:
        v_first, v0, v1p, v2p = vres
        in_specs += [pl.BlockSpec((None, ts, W), tile), pl.BlockSpec((1, W), full2),
                     pl.BlockSpec((W, LANES), full2), pl.BlockSpec((LANES, W), full2)]
        args += [v_first, v0, v1p, v2p]
    tok = pl.BlockSpec((None, ts, W), tile)
    out_specs = [tok] * 5 + [pl.BlockSpec((None, n_c, 1, W), lambda b, i: (b, i, 0, 0)), tok, tok]
    out_shape = ([jax.ShapeDtypeStruct((B, S, W), BF16)] * 5
                 + [jax.ShapeDtypeStruct((B, S // CHUNK, 1, W), F32)]
                 + [jax.ShapeDtypeStruct((B, S, W), F32)] * 2)
    if vres is None:
        out_specs.append(tok)
        out_shape.append(jax.ShapeDtypeStruct((B, S, W), F32))
    return pl.pallas_call(
        functools.partial(_rwkv_prep_kernel, vres is not None),
        grid=(B, S // ts),
        in_specs=in_specs,
        out_specs=out_specs,
        out_shape=out_shape,
        compiler_params=_cparams(("parallel", "parallel"), 48),
        name="rwkv_prep",
    )(*args)


def _wkv_chunk_kernel(at_ref, rt_ref, bt_ref, kt_ref, vb_ref, ge_ref, bonus_ref, lw_ref, lb_ref,
                      y_ref, n_ref):
    C = CHUNK
    n_pairs = RWKV_HEADS // 2
    n_seqs = at_ref.shape[0]
    n_chunks = at_ref.shape[1] // C

    @pl.when(pl.program_id(1) == 0)
    def _():
        n_ref[...] = jnp.zeros_like(n_ref)

    row = lax.broadcasted_iota(jnp.int32, (2 * C, LANES), 0)
    lane = lax.broadcasted_iota(jnp.int32, (2 * C, LANES), 1)
    top = row < C
    left = lane < HEAD_DIM
    same = top == left
    t_row = jnp.where(top, row, row - C)
    s_col = jnp.where(left, lane, lane - HEAD_DIM)
    strict = s_col < t_row
    incl = s_col <= t_row
    even = lax.broadcasted_iota(jnp.int32, (C, LANES), 1) < HEAD_DIM
    zeros_c = jnp.zeros((C, LANES), BF16)

    unit_groups = [[(c, b, p) for c in range(c0, c0 + WKV_GROUP_CHUNKS) for b in range(n_seqs)
                    for p in range(n_pairs)] for c0 in range(0, n_chunks, WKV_GROUP_CHUNKS)]

    def tile(ref, unit):
        c, b, p = unit
        return ref[b, c * C:(c + 1) * C, p * LANES:(p + 1) * LANES]

    def stack_heads(x):
        z = jnp.zeros_like(x)
        return jnp.concatenate([jnp.where(even, x, z), jnp.where(even, z, x)], axis=0)

    def half_sum(x):
        lo = jnp.sum(jnp.where(even, x, 0.0), axis=-1, keepdims=True)
        hi = jnp.sum(jnp.where(even, 0.0, x), axis=-1, keepdims=True)
        return jnp.where(even, lo, hi)

    for units in unit_groups:
        at2 = [stack_heads(tile(at_ref, un)) for un in units]
        bk = [jnp.concatenate([tile(bt_ref, un), tile(kt_ref, un)], axis=0) for un in units]
        g4 = [_dot_nt(jnp.concatenate([a, stack_heads(tile(rt_ref, un))], axis=0), b)
              for un, a, b in zip(units, at2, bk)]
        ga = [jnp.where(strict, g[:2 * C], 0.0) for g in g4]
        gr = [jnp.where(incl, g[2 * C:], 0.0).astype(BF16) for g in g4]
        zs = [jnp.where(same, _dot(g.astype(BF16), jnp.concatenate([zeros_c, tile(vb_ref, un)], axis=0)), 0.0)
              for un, g in zip(units, ga)]
        pw = [jnp.where(same, jnp.where(top, g, pltpu.roll(g, HEAD_DIM, 1)), 0.0).astype(BF16) for g in ga]
        w = [jnp.concatenate([a.astype(F32), z], axis=1) for a, z in zip(at2, zs)]
        w = [x + _dot(m, x.astype(BF16)) for m, x in zip(pw, w)]
        for _ in range(NEUMANN_SQUARINGS):
            pw = [_dot(m, m).astype(BF16) for m in pw]
            w = [x + _dot(m, x.astype(BF16)) for m, x in zip(pw, w)]
        pm = [(x[:C, :LANES] + x[C:, :LANES]).astype(BF16) for x in w]
        q = [x[:C, LANES:] + x[C:, LANES:] for x in w]

        for i, un in enumerate(units):
            c, b, p = un
            cols = slice(p * LANES, (p + 1) * LANES)
            rows = slice(c * C, (c + 1) * C)
            n0 = n_ref[b, p]
            pr = _dot_nt(jnp.concatenate([pm[i], tile(rt_ref, un)], axis=0), n0.astype(BF16))
            u = pr[:C] + q[i]
            uv = jnp.concatenate([u.astype(BF16), tile(vb_ref, un)], axis=0)
            yp = _dot(gr[i], uv)
            y = pr[C:] + jnp.where(even, yp[:C], yp[C:])
            g_slab = ge_ref[b, c, :, cols]
            bkg = (bk[i].astype(F32) * g_slab).astype(BF16)
            dn = lax.dot_general(uv, bkg, (((0,), (0,)), ((), ())), preferred_element_type=F32)
            n_ref[b, p] = g_slab * n0 + jnp.where(same, dn, 0.0)
            mean = half_sum(y) * (1.0 / HEAD_DIM)
            yc = y - mean
            var = half_sum(yc * yc) * (1.0 / HEAD_DIM)
            yn = yc * lax.rsqrt(var + LNX_EPS) * lw_ref[:, cols] + lb_ref[:, cols]
            y_ref[b, rows, cols] = yn + bonus_ref[b, rows, cols]


def _wkv_chunked(at, rt, bt, kt, vb, ge, bonus, lnw, lnb):
    B, S, W = at.shape
    ts = WKV_ROWS
    nb = WKV_SEQS if B % WKV_SEQS == 0 else 1
    n_c = ts // CHUNK
    tile = lambda b, i: (b, i, 0)
    full = lambda b, i: (0, 0)
    tok = pl.BlockSpec((nb, ts, W), tile)
    return pl.pallas_call(
        _wkv_chunk_kernel,
        grid=(B // nb, S // ts),
        in_specs=[tok] * 5 + [pl.BlockSpec((nb, n_c, 1, W), lambda b, i: (b, i, 0, 0)), tok,
                              pl.BlockSpec((1, W), full), pl.BlockSpec((1, W), full)],
        out_specs=tok,
        out_shape=jax.ShapeDtypeStruct((B, S, W), F32),
        scratch_shapes=[pltpu.VMEM((nb, RWKV_HEADS // 2, 2 * HEAD_DIM, LANES), F32)],
        compiler_params=_cparams(("parallel", "arbitrary"), 32),
        name="wkv_chunked",
    )(at, rt, bt, kt, vb, ge, bonus, lnw, lnb)


def _band_attn_kernel(q_ref, k_ref, v_ref, bias_ref, o_ref, s_ref, e_ref, kz_ref, vz_ref):
    i = pl.program_id(1)
    head_rows = kz_ref.shape[0] - BAND_PAD

    @pl.when(i == 0)
    def _():
        kz_ref[0:BAND_PAD, :] = jnp.zeros((BAND_PAD, kz_ref.shape[1]), kz_ref.dtype)
        vz_ref[0:BAND_PAD, :] = jnp.zeros((BAND_PAD, vz_ref.shape[1]), vz_ref.dtype)
        kz_ref[BAND_PAD:, :] = k_ref[0:head_rows, :]
        vz_ref[BAND_PAD:, :] = v_ref[0:head_rows, :]

    n_chunks = q_ref.shape[0] // CHUNK
    n_pairs = ATT_HEADS // 2
    lane = lax.broadcasted_iota(jnp.int32, (CHUNK, LANES), 1)
    even = lane < HEAD_DIM
    slot = lax.broadcasted_iota(jnp.int32, (1, 1, BAND), 2)

    group = s_ref.shape[0]

    def chunk_body(masked, cc, carry):
        cgs = [i * n_chunks + cc * group + j for j in range(group)]
        if masked:
            kv_k, kv_v = kz_ref, vz_ref
            starts = [pl.multiple_of(cg * CHUNK, CHUNK) for cg in cgs]
        else:
            kv_k, kv_v = k_ref, v_ref
            starts = [pl.multiple_of((cg - PREV_CHUNKS) * CHUNK, CHUNK) for cg in cgs]
        rows = [pl.multiple_of((cc * group + j) * CHUNK, CHUNK) for j in range(group)]
        for j in range(group):
            for p in range(n_pairs):
                cols = slice(p * LANES, (p + 1) * LANES)
                q2 = q_ref[pl.ds(rows[j], CHUNK), cols]
                zero = jnp.zeros_like(q2)
                qs = jnp.concatenate([jnp.where(even, q2, zero), jnp.where(even, zero, q2)], axis=0)
                s_ref[j, p] = _dot_nt(qs, kv_k[pl.ds(starts[j], BAND), cols]) + bias_ref[p]
        inv = []
        for j in range(group):
            s = s_ref[j]
            if masked:
                s = jnp.where(slot >= (BAND_PAD - cgs[j] * CHUNK), s, NEG_INF)
            e = jnp.exp2(s - jnp.max(s, axis=-1, keepdims=True))
            inv.append(1.0 / jnp.sum(e, axis=-1, keepdims=True))
            e_ref[j] = e.astype(BF16)
        for j in range(group):
            for p in range(n_pairs):
                cols = slice(p * LANES, (p + 1) * LANES)
                o = _dot(e_ref[j, p], kv_v[pl.ds(starts[j], BAND), cols]) * inv[j][p]
                o_ref[pl.ds(rows[j], CHUNK), cols] = jnp.where(even, o[:CHUNK], o[CHUNK:])
        return carry

    @pl.when(i * n_chunks < PREV_CHUNKS)
    def _():
        lax.fori_loop(0, n_chunks // group, functools.partial(chunk_body, True), 0)

    @pl.when(i * n_chunks >= PREV_CHUNKS)
    def _():
        lax.fori_loop(0, n_chunks // group, functools.partial(chunk_body, False), 0)


def _band_attn(q3, k3, v3, bias_pairs):
    B, S, W = q3.shape
    SP = S
    tq = ATT_ROWS
    assert PREV_CHUNKS % (tq // CHUNK) == 0
    n_pairs = ATT_HEADS // 2
    return pl.pallas_call(
        _band_attn_kernel,
        grid=(B, S // tq),
        in_specs=[pl.BlockSpec((None, tq, W), lambda b, i: (b, i, 0)),
                  pl.BlockSpec((None, SP, W), lambda b, i: (b, 0, 0)),
                  pl.BlockSpec((None, SP, W), lambda b, i: (b, 0, 0)),
                  pl.BlockSpec((ATT_HEADS // 2, 2 * CHUNK, BAND), lambda b, i: (0, 0, 0))],
        out_specs=pl.BlockSpec((None, tq, W), lambda b, i: (b, i, 0)),
        out_shape=jax.ShapeDtypeStruct((B, S, W), F32),
        scratch_shapes=[pltpu.VMEM((ATT_GROUP, n_pairs, 2 * CHUNK, BAND), F32),
                        pltpu.VMEM((ATT_GROUP, n_pairs, 2 * CHUNK, BAND), BF16),
                        pltpu.VMEM((BAND_PAD + PREV_CHUNKS * CHUNK, W), BF16),
                        pltpu.VMEM((BAND_PAD + PREV_CHUNKS * CHUNK, W), BF16)],
        compiler_params=_cparams(("parallel", "arbitrary"), 48),
        name="band_attn",
    )(q3, k3, v3, bias_pairs)


def _mem_kv_kernel(m_ref, g_ref, w_ref, kn_ref, k_ref, v_ref):
    h = _rms(m_ref[...], g_ref[...]).astype(BF16)
    kv = _dot(h, w_ref[...])
    for hd in range(XA_HEADS):
        cols = slice(hd * XA_HEAD_DIM, (hd + 1) * XA_HEAD_DIM)
        k_ref[:, cols] = _rms(kv[:, cols], kn_ref[...]).astype(BF16)
    v_ref[...] = kv[:, XA_WIDTH:].astype(BF16)


def _mem_kv(mem2d, g, w_kv, k_norm):
    R, D = mem2d.shape
    tm = ROW_TILE
    full = lambda i: (0, 0)
    row = lambda i: (i, 0)
    return pl.pallas_call(
        _mem_kv_kernel,
        grid=(R // tm,),
        in_specs=[pl.BlockSpec((tm, D), row), pl.BlockSpec((1, D), full),
                  pl.BlockSpec((D, 2 * XA_WIDTH), full), pl.BlockSpec((1, XA_HEAD_DIM), full)],
        out_specs=[pl.BlockSpec((tm, XA_WIDTH), row)] * 2,
        out_shape=[jax.ShapeDtypeStruct((R, XA_WIDTH), BF16)] * 2,
        compiler_params=_cparams(("parallel",), 32),
        name="mem_kv",
    )(mem2d, g, w_kv, k_norm)


def _mix_xa_router_kernel(x_ref, yr_ref, g_ref, ya_ref, gao_ref, wout_ref, gxa_ref, wq_ref, qn_ref,
                          km_ref, vm_ref, wo_ref, gff_ref, wrt_ref, brt_ref,
                          tri_ref, xo_ref, h_ref, rt_ref, cnt_ref, run_ref, stage_ref):
    first = (pl.program_id(0) == 0) & (pl.program_id(1) == 0)

    @pl.when(first)
    def _():
        run_ref[...] = jnp.zeros_like(run_ref)

    n_rows = x_ref.shape[0]
    groups = [slice(r, r + n_rows // MIX_GROUPS) for r in range(0, n_rows, n_rows // MIX_GROUPS)]
    head_cols = [slice(hd * XA_HEAD_DIM, (hd + 1) * XA_HEAD_DIM) for hd in range(XA_HEADS)]

    yr = [(yr_ref[rows, :] * g_ref[rows, :]).astype(BF16) for rows in groups]
    ya = [_rms(ya_ref[rows, :], gao_ref[...]).astype(BF16) for rows in groups]
    x1 = [x_ref[rows, :] + _dot(a, wout_ref[0:RWKV_WIDTH, :]) + _dot(b, wout_ref[RWKV_WIDTH:, :])
          for rows, a, b in zip(groups, yr, ya)]

    q = [_dot(_rms(xg, gxa_ref[...]).astype(BF16), wq_ref[...]) for xg in x1]
    qscale = XA_HEAD_DIM ** -0.5 * LOG2E
    qh = [[(_rms(qg[:, cols], qn_ref[...]) * qscale).astype(BF16) for cols in head_cols] for qg in q]
    s = [[_dot_nt(qg[hd], km_ref[:, cols]) for hd, cols in enumerate(head_cols)] for qg in qh]
    e = [[jnp.exp2(sh - jnp.max(sh, axis=-1, keepdims=True)) for sh in sg] for sg in s]
    inv = [[1.0 / jnp.sum(eh, axis=-1, keepdims=True) for eh in eg] for eg in e]
    o = [jnp.concatenate([_dot(eg[hd].astype(BF16), vm_ref[:, cols]) * ig[hd]
                          for hd, cols in enumerate(head_cols)], axis=-1).astype(BF16)
         for eg, ig in zip(e, inv)]
    x = jnp.concatenate([xg + _dot(og, wo_ref[...]) for xg, og in zip(x1, o)], axis=0)
    xo_ref[...] = x

    h = _rms(x, gff_ref[...])
    tm = h.shape[0]
    _store_row_tiles(stage_ref, h)
    h_ref[...] = stage_ref[...].astype(BF16)
    logits =_dot(h.astype(BF16), wrt_ref[...]) + brt_ref[...]
    lane_i = lax.broadcasted_iota(jnp.int32, logits.shape, 1)
    lane = lane_i.astype(F32)
    big = float(ROUTE_LANES)
    gmask = lane < N_GROUPS
    gmax = jnp.max(jnp.where(gmask, logits, -jnp.inf), axis=-1, keepdims=True)
    ge = jnp.where(gmask, jnp.exp(logits - gmax), 0.0)
    gp = ge / jnp.sum(ge, axis=-1, keepdims=True)
    g_gate = jnp.max(gp, axis=-1, keepdims=True)
    g_top = jnp.min(jnp.where(gmask & (gp == g_gate), lane, big), axis=-1, keepdims=True)
    lo = N_GROUPS + g_top * EXPERTS_PER_GROUP
    rmask = (lane >= lo) & (lane < lo + EXPERTS_PER_GROUP)
    rmax = jnp.max(jnp.where(rmask, logits, -jnp.inf), axis=-1, keepdims=True)
    re = jnp.where(rmask, jnp.exp(logits - rmax), 0.0)
    p_in = re / jnp.sum(re, axis=-1, keepdims=True)
    p1 = jnp.max(jnp.where(rmask, p_in, -1.0), axis=-1, keepdims=True)
    j1 = jnp.min(jnp.where(rmask & (p_in == p1), lane, big), axis=-1, keepdims=True)
    rmask2 = rmask & (lane != j1)
    p2 = jnp.max(jnp.where(rmask2, p_in, -1.0), axis=-1, keepdims=True)
    j2 = jnp.min(jnp.where(rmask2 & (p_in == p2), lane, big), axis=-1, keepdims=True)
    denom = p1 + p2
    w1 = g_gate * (p1 / denom)
    w2 = g_gate * (p2 / denom)
    e1 = j1 - N_GROUPS
    e2 = j2 - N_GROUPS
    hit1 = lane == j1
    hit2 = lane == j2
    hits = jnp.where(hit1 | hit2, 1.0, 0.0)
    before = _dot(tri_ref[...], hits.astype(BF16)) + run_ref[...]
    rank1 = jnp.sum(jnp.where(hit1, before, 0.0), axis=-1, keepdims=True)
    rank2 = jnp.sum(jnp.where(hit2, before, 0.0), axis=-1, keepdims=True)
    run = run_ref[...] + jnp.sum(hits, axis=0, keepdims=True)
    run_ref[...] = run
    cnt_ref[...] = jnp.broadcast_to(run, cnt_ref.shape)
    rt_ref[...] = jnp.where(lane_i == 0, e1, jnp.where(lane_i == 1, e2,
                            jnp.where(lane_i == 2, w1, jnp.where(lane_i == 3, w2,
                                      jnp.where(lane_i == 4, rank1, jnp.where(lane_i == 5, rank2, 0.0))))))


def _mix_xa_router(x3, yr3, g3, ya3, gao, wout, gxa, wq, qn, km3, vm3, wo, gff, wrt, brt):
    B, S, D = x3.shape
    M = km3.shape[1]
    tm = MIX_ROWS
    tile = lambda b, i: (b, i, 0)
    full = lambda b, i: (0, 0)
    memb = lambda b, i: (b, 0, 0)
    W = RWKV_WIDTH
    n_i = S // tm
    tri = jnp.tril(jnp.ones((tm, tm), F32), -1).astype(BF16)
    return pl.pallas_call(
        _mix_xa_router_kernel,
        grid=(B, S // tm),
        in_specs=[pl.BlockSpec((None, tm, D), tile), pl.BlockSpec((None, tm, W), tile),
                  pl.BlockSpec((None, tm, W), tile), pl.BlockSpec((None, tm, W), tile),
                  pl.BlockSpec((1, W), full), pl.BlockSpec((2 * W, D), full),
                  pl.BlockSpec((1, D), full), pl.BlockSpec((D, XA_WIDTH), full),
                  pl.BlockSpec((1, XA_HEAD_DIM), full),
                  pl.BlockSpec((None, M, XA_WIDTH), memb), pl.BlockSpec((None, M, XA_WIDTH), memb),
                  pl.BlockSpec((XA_WIDTH, D), full), pl.BlockSpec((1, D), full),
                  pl.BlockSpec((D, ROUTE_LANES), full), pl.BlockSpec((1, ROUTE_LANES), full),
                  pl.BlockSpec((tm, tm), full)],
        out_specs=[pl.BlockSpec((None, tm, D), tile),
                   pl.BlockSpec((tm * SUBLANES, LANES), lambda b, i: (b * n_i + i, 0)),
                   pl.BlockSpec((None, tm, ROUTE_LANES), tile),
                   pl.BlockSpec((SUBLANES, ROUTE_LANES), full)],
        out_shape=[jax.ShapeDtypeStruct((B, S, D), F32),
                   jax.ShapeDtypeStruct((B * S * SUBLANES, LANES), BF16),
                   jax.ShapeDtypeStruct((B, S, ROUTE_LANES), F32),
                   jax.ShapeDtypeStruct((SUBLANES, ROUTE_LANES), F32)],
        scratch_shapes=[pltpu.VMEM((1, ROUTE_LANES), F32), pltpu.VMEM((tm * SUBLANES, LANES), F32)],
        compiler_params=_cparams(("arbitrary", "arbitrary"), 48),
        name="mix_xa_router",
    )(x3, yr3, g3, ya3, gao, wout, gxa, wq, qn, km3, vm3, wo, gff, wrt, brt, tri)


def _row_copy(src_ref, src_off, dst_ref, dst_off, sem):
    src = src_ref.at[pl.ds(pl.multiple_of(src_off, SUBLANES), SUBLANES)]
    dst = dst_ref.at[pl.ds(pl.multiple_of(dst_off, SUBLANES), SUBLANES)]
    return pltpu.make_async_copy(src, dst, sem)


def _load_row_tiles(stage_ref, rows):
    return jnp.concatenate([stage_ref[pl.ds(j, rows, stride=SUBLANES), :] for j in range(SUBLANES)], axis=-1)


def _store_row_tiles(stage_ref, x):
    for j in range(SUBLANES):
        stage_ref[pl.ds(j, x.shape[0], stride=SUBLANES), :] = x[:, j * LANES:(j + 1) * LANES]


def _dispatch_kernel(dest_ref, h_ref, zeros_ref, xs_ref, sem):
    del zeros_ref
    tm = h_ref.shape[0] // SUBLANES
    base = pl.program_id(0) * (tm * TOP_K)

    def issue(r, carry):
        for kk in range(TOP_K):
            _row_copy(h_ref, r * SUBLANES, xs_ref, dest_ref[base + r * TOP_K + kk], sem).start(priority=kk)
        return carry

    lax.fori_loop(0, tm, issue, 0, unroll=DMA_UNROLL)

    def drain(r, carry):
        for kk in range(TOP_K):
            _row_copy(h_ref, r * SUBLANES, xs_ref, dest_ref[base + r * TOP_K + kk], sem).wait()
        return carry

    lax.fori_loop(0, tm, drain, 0, unroll=DMA_UNROLL)


def _dispatch(dest, h_tiles, n_slots):
    tm = MOE_ROWS
    T = h_tiles.shape[0] // SUBLANES
    zeros = jnp.zeros((n_slots * SUBLANES, LANES), BF16)
    return pl.pallas_call(
        _dispatch_kernel,
        grid_spec=pltpu.PrefetchScalarGridSpec(
            num_scalar_prefetch=1,
            grid=(T // tm,),
            in_specs=[pl.BlockSpec((tm * SUBLANES, LANES), lambda i, d: (i, 0)),
                      pl.BlockSpec(memory_space=pl.ANY)],
            out_specs=pl.BlockSpec(memory_space=pl.ANY),
            scratch_shapes=[pltpu.SemaphoreType.DMA(())],
        ),
        out_shape=jax.ShapeDtypeStruct((n_slots * SUBLANES, LANES), BF16),
        input_output_aliases={2: 0},
        compiler_params=_cparams(("arbitrary",), 32),
        name="moe_dispatch",
    )(dest, h_tiles, zeros)


def _expert_kernel(be_ref, nu_ref, xs_ref, wg_ref, wu_ref, wd_ref, y_ref, wgb_ref, wub_ref, wdb_ref,
                   stage_ref):
    i = pl.program_id(0)
    used = i < nu_ref[0]
    new_expert = (i == 0) | (be_ref[i] != be_ref[jnp.maximum(i - 1, 0)])

    @pl.when(used & new_expert)
    def _():
        wgb_ref[...] = wg_ref[...].astype(BF16)
        wub_ref[...] = wu_ref[...].astype(BF16)
        wdb_ref[...] = wd_ref[...].astype(BF16)

    @pl.when(used)
    def _():
        stage_ref[...] = xs_ref[...].astype(F32)
        xb = _load_row_tiles(stage_ref, MOE_BLOCK).astype(BF16)
        gate = _dot(xb, wgb_ref[...])
        up = _dot(xb, wub_ref[...])
        act = (gate * jax.nn.sigmoid(gate) * up).astype(BF16)
        _store_row_tiles(stage_ref, _dot(act, wdb_ref[...]))
        y_ref[...] = stage_ref[...].astype(BF16)

    @pl.when(jnp.logical_not(used))
    def _():
        y_ref[...] = jnp.zeros_like(y_ref)


def _expert_ffn(block_expert, n_used, xs_tiles, layer, wg, wu, wd):
    D, FF = wg.shape[2], wg.shape[3]
    n_blocks = xs_tiles.shape[0] // (MOE_BLOCK * SUBLANES)
    rows = MOE_BLOCK * SUBLANES
    blk = lambda i, be, nu: (jnp.minimum(i, nu[0] - 1), 0)
    wsel = lambda i, be, nu: (layer, be[jnp.minimum(i, nu[0] - 1)], 0, 0)
    return pl.pallas_call(
        _expert_kernel,
        grid_spec=pltpu.PrefetchScalarGridSpec(
            num_scalar_prefetch=2,
            grid=(n_blocks,),
            in_specs=[pl.BlockSpec((rows, LANES), blk),
                      pl.BlockSpec((None, None, D, FF), wsel),
                      pl.BlockSpec((None, None, D, FF), wsel),
                      pl.BlockSpec((None, None, FF, D), wsel)],
            out_specs=pl.BlockSpec((rows, LANES), lambda i, be, nu: (i, 0)),
            scratch_shapes=[pltpu.VMEM((D, FF), BF16), pltpu.VMEM((D, FF), BF16), pltpu.VMEM((FF, D), BF16),
                            pltpu.VMEM((rows, LANES), F32)],
        ),
        out_shape=jax.ShapeDtypeStruct(xs_tiles.shape, BF16),
        compiler_params=_cparams(("arbitrary",), 48),
        name="moe_experts",
    )(block_expert, n_used, xs_tiles, wg, wu, wd)


def _combine_kernel(dest_ref, x_ref, wt_ref, yb_ref, o_ref, buf_ref, sem, stage_ref):
    tm = x_ref.shape[0]
    base = pl.program_id(0) * (tm * TOP_K)

    def issue(r, carry):
        for kk in range(TOP_K):
            _row_copy(yb_ref, dest_ref[base + r * TOP_K + kk], buf_ref.at[kk], r * SUBLANES, sem).start(priority=kk)
        return carry

    lax.fori_loop(0, tm, issue, 0, unroll=DMA_UNROLL)

    def drain(r, carry):
        for kk in range(TOP_K):
            _row_copy(yb_ref, dest_ref[base + r * TOP_K + kk], buf_ref.at[kk], r * SUBLANES, sem).wait()
        return carry

    lax.fori_loop(0, tm, drain, 0, unroll=DMA_UNROLL)
    wt = wt_ref[...]
    w1 = wt[:, 2:3]
    w2 = wt[:, 3:4]
    stage_ref[...] = buf_ref[...].astype(F32)
    for j in range(SUBLANES):
        cols = slice(j * LANES, (j + 1) * LANES)
        y1 = stage_ref[0, pl.ds(j, tm, stride=SUBLANES), :]
        y2 = stage_ref[1, pl.ds(j, tm, stride=SUBLANES), :]
        o_ref[:, cols] = x_ref[:, cols] + (y1 * w1 + y2 * w2)


def _combine(dest, x2d, rt2d, yb_tiles):
    T, D = x2d.shape
    tm = MOE_ROWS
    return pl.pallas_call(
        _combine_kernel,
        grid_spec=pltpu.PrefetchScalarGridSpec(
            num_scalar_prefetch=1,
            grid=(T // tm,),
            in_specs=[pl.BlockSpec((tm, D), lambda i, d: (i, 0)),
                      pl.BlockSpec((tm, ROUTE_LANES), lambda i, d: (i, 0)),
                      pl.BlockSpec(memory_space=pl.ANY)],
            out_specs=pl.BlockSpec((tm, D), lambda i, d: (i, 0)),
            scratch_shapes=[pltpu.VMEM((TOP_K, tm * SUBLANES, LANES), BF16), pltpu.SemaphoreType.DMA(()),
                            pltpu.VMEM((TOP_K, tm * SUBLANES, LANES), F32)],
        ),
        out_shape=jax.ShapeDtypeStruct((T, D), F32),
        compiler_params=_cparams(("arbitrary",), 48),
        name="moe_combine",
    )(dest, x2d, rt2d, yb_tiles)


def _moe_plan(rt2d, counts_lanes):
    T = rt2d.shape[0]
    A = T * TOP_K
    e = rt2d[:, 0:TOP_K].astype(jnp.int32).reshape(A)
    rank = rt2d[:, 4:4 + TOP_K].astype(jnp.int32).reshape(A)
    counts = counts_lanes[0, N_GROUPS:N_GROUPS + N_EXPERTS].astype(jnp.int32)
    padded = ((counts + MOE_BLOCK - 1) // MOE_BLOCK) * MOE_BLOCK
    pends = jnp.cumsum(padded)
    pstarts = pends - padded
    onehot = e[:, None] == jnp.arange(N_EXPERTS, dtype=jnp.int32)[None, :]
    dest = jnp.sum(jnp.where(onehot, pstarts[None, :], 0), axis=1) + rank
    n_blocks = (A + N_EXPERTS * (MOE_BLOCK - 1) + MOE_BLOCK - 1) // MOE_BLOCK
    block_start = jnp.arange(n_blocks, dtype=jnp.int32) * MOE_BLOCK
    block_expert = jnp.minimum(jnp.sum(pends[None, :] <= block_start[:, None], axis=1), N_EXPERTS - 1)
    n_used = (pends[-1] // MOE_BLOCK).astype(jnp.int32).reshape(1)
    dest_off = (dest * SUBLANES).astype(jnp.int32)
    return dest_off, block_expert.astype(jnp.int32), n_used, n_blocks * MOE_BLOCK


def _rel_bias_pairs(table):
    n_ext = BAND + CHUNK - 1
    ext = jnp.concatenate([table, jnp.broadcast_to(table[:, -1:], (table.shape[0], n_ext - table.shape[1]))], axis=1)
    rev = ext[:, ::-1]
    bias = jnp.stack([rev[:, CHUNK - 1 - i:CHUNK - 1 - i + BAND] for i in range(CHUNK)], axis=1)
    return (bias.astype(F32) * LOG2E).reshape(-1, ATT_HEADS // 2, 2 * CHUNK, BAND)


def _pad_rows(w, rows, offset=0):
    out = jnp.zeros((rows, w.shape[1]), w.dtype)
    return out.at[offset:offset + w.shape[0]].set(w)


def kernel(x, mem, norm_mix, w_in, shift_mu, decay_w0, decay_w2, iclr_a0, iclr_a2, gate_g2, rw_k_k, rw_k_a, rw_r_k, lnx_w, lnx_b, vres_v0, vres_v1, vres_v2, att_q_norm, att_k_norm, att_rel_bias, att_out_norm, w_out, norm_xa, norm_mem, xa_w_q, xa_w_kv, xa_w_o, xa_q_norm, xa_k_norm, norm_ffn, moe_w_group, moe_b_group, moe_w_route, moe_b_route, moe_w_gate, moe_w_up, moe_w_down):
    B, S, D = x.shape
    T = B * S
    depth = w_in.shape[0]
    M = mem.shape[1]
    row = lambda p: p.reshape(1, -1)
    ones_bd = jnp.kron(jnp.eye(ATT_HEADS, dtype=F32), jnp.ones((HEAD_DIM, HEAD_DIM), F32)).astype(BF16)

    bias_pairs = _rel_bias_pairs(att_rel_bias.reshape(depth * ATT_HEADS, -1))

    v_first = None
    for l in range(depth):
        u, qh, kh, vh = _inproj(x.reshape(T, D), row(norm_mix[l]), w_in[l].astype(BF16),
                                row(jnp.tile(att_q_norm[l], ATT_HEADS)),
                                row(jnp.tile(att_k_norm[l], ATT_HEADS)))
        w2p = _pad_rows(decay_w2[l], LANES, 0).astype(BF16)
        a2p = _pad_rows(iclr_a2[l], LANES, DECAY_LORA).astype(BF16)
        vres = None
        if l > 0:
            v1p = jnp.zeros((RWKV_WIDTH, LANES), F32).at[:, :MV_LORA].set(vres_v1[l - 1]).astype(BF16)
            v2p = _pad_rows(vres_v2[l - 1], LANES, 0).astype(BF16)
            vres = (v_first, row(vres_v0[l - 1]), v1p, v2p)
        prep = _rwkv_prep(u.reshape(B, S, RWKV_IN), row(shift_mu[l]), row(decay_w0[l]),
                          w2p, row(iclr_a0[l]), a2p, gate_g2[l].astype(BF16),
                          row(rw_k_k[l]), row(rw_k_a[l]), row(rw_r_k[l]), ones_bd, vres)
        at3, rt3, bt3, kt3, vb3, ge4, bonus3, g3 = prep[:8]
        if l == 0:
            v_first = prep[8]
        yr3 = _wkv_chunked(at3, rt3, bt3, kt3, vb3, ge4, bonus3, row(lnx_w[l]), row(lnx_b[l]))
        ya3 = _band_attn(qh.reshape(B, S, ATT_WIDTH), kh.reshape(B, S, ATT_WIDTH),
                         vh.reshape(B, S, ATT_WIDTH), bias_pairs[l])
        km, vm = _mem_kv(mem.reshape(B * M, D), row(norm_mem[l]), xa_w_kv[l].astype(BF16), row(xa_k_norm[l]))
        wrt = jnp.zeros((D, ROUTE_LANES), F32)
        wrt = wrt.at[:, :N_GROUPS].set(moe_w_group[l]).at[:, N_GROUPS:N_GROUPS + N_EXPERTS].set(moe_w_route[l])
        brt = jnp.zeros((ROUTE_LANES,), F32)
        brt = brt.at[:N_GROUPS].set(moe_b_group[l]).at[N_GROUPS:N_GROUPS + N_EXPERTS].set(moe_b_route[l])
        x3, h_tiles, rt3, counts = _mix_xa_router(
            x, yr3, g3, ya3, row(att_out_norm[l]), w_out[l].astype(BF16), row(norm_xa[l]),
            xa_w_q[l].astype(BF16), row(xa_q_norm[l]), km.reshape(B, M, XA_WIDTH), vm.reshape(B, M, XA_WIDTH),
            xa_w_o[l].astype(BF16), row(norm_ffn[l]), wrt.astype(BF16), row(brt))
        rt2d = rt3.reshape(T, ROUTE_LANES)
        dest, block_expert, n_used, n_slots = _moe_plan(rt2d, counts)
        xs = _dispatch(dest, h_tiles, n_slots)
        yb = _expert_ffn(block_expert, n_used, xs, l, moe_w_gate, moe_w_up, moe_w_down)
        x = _combine(dest, x3.reshape(T, D), rt2d, yb).reshape(B, S, D)
    return x
```

```python
import functools

import jax
import jax.numpy as jnp
from jax import lax
from jax.experimental import pallas as pl
from jax.experimental.pallas import tpu as pltpu

F32 = jnp.float32
BF16 = jnp.bfloat16

CHUNK = 64
RWKV_HEADS = 8
HEAD_DIM = 64
RWKV_WIDTH = 512
ATT_HEADS = 8
ATT_WIDTH = 512
DECAY_LORA = 64
AAA_LORA = 64
MV_LORA = 32
GATE_LORA = 128
RWKV_IN = 3 * RWKV_WIDTH + DECAY_LORA + AAA_LORA + GATE_LORA
LORA_OFF = 3 * RWKV_WIDTH
GATE_OFF = LORA_OFF + DECAY_LORA + AAA_LORA
IN_WIDTH = RWKV_IN + 3 * ATT_WIDTH
PREV_CHUNKS = 8
BAND = (PREV_CHUNKS + 1) * CHUNK
BAND_PAD = PREV_CHUNKS * CHUNK
REL_MAX = 256
XA_HEADS = 4
XA_HEAD_DIM = 128
XA_WIDTH = 512
N_GROUPS = 4
EXPERTS_PER_GROUP = 8
N_EXPERTS = 32
TOP_K = 2
EXPERT_FF = 512
MOE_BLOCK = 512
NORM_EPS = 1e-6
LNX_EPS = 64e-5
NEG_INF = -1e30

LANES = 128
SUBLANES = 8
ROW_TILE = 256
INPROJ_ROWS = 512
PREP_ROWS = 512
MIX_ROWS = 512
MIX_GROUPS = 2
LOG2E = 1.4426950408889634
WKV_ROWS = 256
WKV_GROUP_CHUNKS = 2
WKV_SEQS = 2
NEUMANN_SQUARINGS =CHUNK.bit_length() - 2
ATT_ROWS = 256
ATT_GROUP = 2
MOE_ROWS = 1024
ROUTE_LANES = 128
DMA_UNROLL = 8


def _cparams(semantics, vmem_mib):
    return pltpu.CompilerParams(dimension_semantics=semantics,
                                vmem_limit_bytes=vmem_mib * 1024 * 1024)


def _dot(a, b):
    return jnp.dot(a, b, preferred_element_type=F32)


def _dot_nt(a, b):
    return lax.dot_general(a, b, (((1,), (1,)), ((), ())), preferred_element_type=F32)


def _rms(x, g):
    ms = jnp.mean(x * x, axis=-1, keepdims=True)
    return x * lax.rsqrt(ms + NORM_EPS) * g


def _group_sum(x, ones_bd):
    hi = x.astype(BF16)
    lo = (x - hi.astype(F32)).astype(BF16)
    return _dot(hi, ones_bd) + _dot(lo, ones_bd)


def _group_sumsq(x, ones_bd):
    return _group_sum(x * x, ones_bd)


def _inproj_kernel(x_ref, g_ref, w_ref, gq_ref, gk_ref, u_ref, q_ref, k_ref, v_ref):
    h = _rms(x_ref[...], g_ref[...]).astype(BF16)
    for j in range(0, RWKV_IN, 256):
        u_ref[:, j:j + 256] = _dot(h, w_ref[:, j:j + 256])
    even = lax.broadcasted_iota(jnp.int32, (x_ref.shape[0], LANES), 1) < HEAD_DIM

    def head_rms(x, gain_ref, scale):
        for j in range(0, ATT_WIDTH, LANES):
            xs = x[:, j:j + LANES]
            sq = xs * xs
            ss = jnp.where(even, jnp.sum(jnp.where(even, sq, 0.0), axis=-1, keepdims=True),
                           jnp.sum(jnp.where(even, 0.0, sq), axis=-1, keepdims=True))
            yield j, xs * lax.rsqrt(ss * (1.0 / HEAD_DIM) + NORM_EPS) * (gain_ref[:, j:j + LANES] * scale)

    q = _dot(h, w_ref[:, RWKV_IN:RWKV_IN + ATT_WIDTH])
    for j, qn in head_rms(q, gq_ref, HEAD_DIM ** -0.5 * LOG2E):
        q_ref[:, j:j + LANES] = qn.astype(BF16)
    k = _dot(h, w_ref[:, RWKV_IN + ATT_WIDTH:RWKV_IN + 2 * ATT_WIDTH])
    for j, kn in head_rms(k, gk_ref, 1.0):
        k_ref[:, j:j + LANES] = kn.astype(BF16)
    v_ref[...] = _dot(h, w_ref[:, RWKV_IN + 2 * ATT_WIDTH:]).astype(BF16)


def _inproj(x2d, g, w, gq, gk):
    T, D = x2d.shape
    tm = INPROJ_ROWS
    full = lambda i: (0, 0)
    row = lambda i: (i, 0)
    return pl.pallas_call(
        _inproj_kernel,
        grid=(T // tm,),
        in_specs=[pl.BlockSpec((tm, D), row), pl.BlockSpec((1, D), full),
                  pl.BlockSpec((D, IN_WIDTH), full), pl.BlockSpec((1, ATT_WIDTH), full),
                  pl.BlockSpec((1, ATT_WIDTH), full)],
        out_specs=[pl.BlockSpec((tm, RWKV_IN), row), pl.BlockSpec((tm, ATT_WIDTH), row),
                   pl.BlockSpec((tm, ATT_WIDTH), row), pl.BlockSpec((tm, ATT_WIDTH), row)],
        out_shape=[jax.ShapeDtypeStruct((T, RWKV_IN), F32), jax.ShapeDtypeStruct((T, ATT_WIDTH), BF16),
                   jax.ShapeDtypeStruct((T, ATT_WIDTH), BF16), jax.ShapeDtypeStruct((T, ATT_WIDTH), BF16)],
        compiler_params=_cparams(("parallel",), 48),
        name="inproj",
    )(x2d, g, w, gq, gk)


def _rwkv_prep_kernel(has_vres, *refs):
    if has_vres:
        (u_ref, up_ref, mu_ref, w0_ref, w2_ref, a0_ref, a2_ref, g2_ref, kkp_ref, kap_ref, rkp_ref,
         bd_ref, tri_ref, vf_ref, v0_ref, v1_ref, v2_ref,
         at_ref, rt_ref, bt_ref, kt_ref, vb_ref, ge_ref, bonus_ref, g_ref) = refs
    else:
        (u_ref, up_ref, mu_ref, w0_ref, w2_ref, a0_ref, a2_ref, g2_ref, kkp_ref, kap_ref, rkp_ref,
         bd_ref, tri_ref,
         at_ref, rt_ref, bt_ref, kt_ref, vb_ref, ge_ref, bonus_ref, g_ref, v_ref) = refs
    i = pl.program_id(1)
    u = u_ref[...]
    ts = u.shape[0]
    prev_row = jnp.where(i > 0, up_ref[SUBLANES - 1:SUBLANES, :], 0.0)
    rolled = pltpu.roll(u, 1, 0)
    row_id = lax.broadcasted_iota(jnp.int32, u.shape, 0)
    shifted = jnp.where(row_id == 0, jnp.broadcast_to(prev_row, u.shape), rolled)
    u = u + mu_ref[...] * (shifted - u)

    r = u[:, 0:RWKV_WIDTH]
    k = u[:, RWKV_WIDTH:2 * RWKV_WIDTH]
    v = u[:, 2 * RWKV_WIDTH:3 * RWKV_WIDTH]
    lora = u[:, LORA_OFF:GATE_OFF]
    gl = u[:, GATE_OFF:RWKV_IN]

    w = w0_ref[...] + _dot(jnp.tanh(lora).astype(BF16), w2_ref[...])
    z = -w
    softplus = jnp.maximum(z, 0.0) + jnp.log(1.0 + jnp.exp(-jnp.abs(z)))
    w = -softplus - 0.5
    log_decay = -jnp.exp(w)
    a = jax.nn.sigmoid(a0_ref[...] + _dot(lora.astype(BF16), a2_ref[...]))
    g_ref[...] = _dot(jax.nn.sigmoid(gl).astype(BF16), g2_ref[...])
    if has_vres:
        mix = _dot(_dot(v.astype(BF16), v1_ref[...]).astype(BF16), v2_ref[...])
        v = v + (vf_ref[...] - v) * jax.nn.sigmoid(v0_ref[...] + mix)
    else:
        v_ref[...] = v

    bd = bd_ref[...]
    kk = k * kkp_ref[...]
    kk = kk * lax.rsqrt(jnp.maximum(_group_sumsq(kk, bd), 1e-24))
    k2 = k * (1.0 + (a - 1.0) * kap_ref[...])
    bonus_ref[...] = _group_sum(r * k2 * rkp_ref[...], bd) * v

    hi = log_decay.astype(BF16)
    lo = (log_decay - hi.astype(F32)).astype(BF16)
    tri = tri_ref[...]
    cum = _dot(tri, hi) + _dot(tri, lo)
    inv_gamma = jnp.exp(-cum)
    at_ref[...] = (-kk * jnp.exp(cum - log_decay)).astype(BF16)
    rt_ref[...] = (r * jnp.exp(cum)).astype(BF16)
    bt_ref[...] = (kk * a * inv_gamma).astype(BF16)
    kt_ref[...] = (k2 * inv_gamma).astype(BF16)
    vb_ref[...] = v.astype(BF16)
    for c in range(ts // CHUNK):
        ge_ref[c] = jnp.exp(jnp.sum(log_decay[c * CHUNK:(c + 1) * CHUNK, :], axis=0, keepdims=True))


def _rwkv_prep(u3, mu, w0, w2p, a0, a2p, g2, kkp, kap, rkp, ones_bd, vres):
    B, S, _ = u3.shape
    ts = PREP_ROWS
    W = RWKV_WIDTH
    n_c = ts // CHUNK
    full2 = lambda b, i: (0, 0)
    tile = lambda b, i: (b, i, 0)
    prev = lambda b, i: (b, jnp.maximum(i * (ts // SUBLANES) - 1, 0), 0)
    tri = (jnp.tril(jnp.ones((ts, ts), F32))
           * jnp.kron(jnp.eye(n_c, dtype=F32), jnp.ones((CHUNK, CHUNK), F32))).astype(BF16)
    in_specs = [pl.BlockSpec((None, ts, RWKV_IN), tile),
                pl.BlockSpec((None, SUBLANES, RWKV_IN), prev),
                pl.BlockSpec((1, RWKV_IN), full2), pl.BlockSpec((1, W), full2),
                pl.BlockSpec((LANES, W), full2), pl.BlockSpec((1, W), full2),
                pl.BlockSpec((LANES, W), full2), pl.BlockSpec((GATE_LORA, W), full2),
                pl.BlockSpec((1, W), full2), pl.BlockSpec((1, W), full2), pl.BlockSpec((1, W), full2),
                pl.BlockSpec((W, W), full2), pl.BlockSpec((ts, ts), full2)]
    args = [u3, u3, mu, w0, w2p, a0, a2p, g2, kkp, kap, rkp, ones_bd, tri]
    if vres is not None:
        v_first, v0, v1p, v2p = vres
        in_specs += [pl.BlockSpec((None, ts, W), tile), pl.BlockSpec((1, W), full2),
                     pl.BlockSpec((W, LANES), full2), pl.BlockSpec((LANES, W), full2)]
        args += [v_first, v0, v1p, v2p]
    tok = pl.BlockSpec((None, ts, W), tile)
    out_specs = [tok] * 5 + [pl.BlockSpec((None, n_c, 1, W), lambda b, i: (b, i, 0, 0)), tok, tok]
    out_shape = ([jax.ShapeDtypeStruct((B, S, W), BF16)] * 5
                 + [jax.ShapeDtypeStruct((B, S // CHUNK, 1, W), F32)]
                 + [jax.ShapeDtypeStruct((B, S, W), F32)] * 2)
    if vres is None:
        out_specs.append(tok)
        out_shape.append(jax.ShapeDtypeStruct((B, S, W), F32))
    return pl.pallas_call(
        functools.partial(_rwkv_prep_kernel, vres is not None),
        grid=(B, S // ts),
        in_specs=in_specs,
        out_specs=out_specs,
        out_shape=out_shape,
        compiler_params=_cparams(("parallel", "parallel"), 48),
        name="rwkv_prep",
    )(*args)


def _wkv_chunk_kernel(at_ref, rt_ref, bt_ref, kt_ref, vb_ref, ge_ref, bonus_ref, lw_ref, lb_ref,
                      y_ref, n_ref):
    C = CHUNK
    n_pairs = RWKV_HEADS // 2
    n_seqs = at_ref.shape[0]
    n_chunks = at_ref.shape[1] // C

    @pl.when(pl.program_id(1) == 0)
    def _():
        n_ref[...] = jnp.zeros_like(n_ref)

    row = lax.broadcasted_iota(jnp.int32, (2 * C, LANES), 0)
    lane = lax.broadcasted_iota(jnp.int32, (2 * C, LANES), 1)
    top = row < C
    left = lane < HEAD_DIM
    same = top == left
    t_row = jnp.where(top, row, row - C)
    s_col = jnp.where(left, lane, lane - HEAD_DIM)
    strict = s_col < t_row
    incl = s_col <= t_row
    even = lax.broadcasted_iota(jnp.int32, (C, LANES), 1) < HEAD_DIM
    zeros_c = jnp.zeros((C, LANES), BF16)

    unit_groups = [[(c, b, p) for c in range(c0, c0 + WKV_GROUP_CHUNKS) for b in range(n_seqs)
                    for p in range(n_pairs)] for c0 in range(0, n_chunks, WKV_GROUP_CHUNKS)]

    def tile(ref, unit):
        c, b, p = unit
        return ref[b, c * C:(c + 1) * C, p * LANES:(p + 1) * LANES]

    def stack_heads(x):
        z = jnp.zeros_like(x)
        return jnp.concatenate([jnp.where(even, x, z), jnp.where(even, z, x)], axis=0)

    def half_sum(x):
        lo = jnp.sum(jnp.where(even, x, 0.0), axis=-1, keepdims=True)
        hi = jnp.sum(jnp.where(even, 0.0, x), axis=-1, keepdims=True)
        return jnp.where(even, lo, hi)

    for units in unit_groups:
        at2 = [stack_heads(tile(at_ref, un)) for un in units]
        bk = [jnp.concatenate([tile(bt_ref, un), tile(kt_ref, un)], axis=0) for un in units]
        g4 = [_dot_nt(jnp.concatenate([a, stack_heads(tile(rt_ref, un))], axis=0), b)
              for un, a, b in zip(units, at2, bk)]
        ga = [jnp.where(strict, g[:2 * C], 0.0) for g in g4]
        gr = [jnp.where(incl, g[2 * C:], 0.0).astype(BF16) for g in g4]
        zs = [jnp.where(same, _dot(g.astype(BF16), jnp.concatenate([zeros_c, tile(vb_ref, un)], axis=0)), 0.0)
              for un, g in zip(units, ga)]
        pw = [jnp.where(same, jnp.where(top, g, pltpu.roll(g, HEAD_DIM, 1)), 0.0).astype(BF16) for g in ga]
        w = [jnp.concatenate([a.astype(F32), z], axis=1) for a, z in zip(at2, zs)]
        w = [x + _dot(m, x.astype(BF16)) for m, x in zip(pw, w)]
        for _ in range(NEUMANN_SQUARINGS):
            pw = [_dot(m, m).astype(BF16) for m in pw]
            w = [x + _dot(m, x.astype(BF16)) for m, x in zip(pw, w)]
        pm = [(x[:C, :LANES] + x[C:, :LANES]).astype(BF16) for x in w]
        q = [x[:C, LANES:] + x[C:, LANES:] for x in w]

        for i, un in enumerate(units):
            c, b, p = un
            cols = slice(p * LANES, (p + 1) * LANES)
            rows = slice(c * C, (c + 1) * C)
            n0 = n_ref[b, p]
            pr = _dot_nt(jnp.concatenate([pm[i], tile(rt_ref, un)], axis=0), n0.astype(BF16))
            u = pr[:C] + q[i]
            uv = jnp.concatenate([u.astype(BF16), tile(vb_ref, un)], axis=0)
            yp = _dot(gr[i], uv)
            y = pr[C:] + jnp.where(even, yp[:C], yp[C:])
            g_slab = ge_ref[b, c, :, cols]
            bkg = (bk[i].astype(F32) * g_slab).astype(BF16)
            dn = lax.dot_general(uv, bkg, (((0,), (0,)), ((), ())), preferred_element_type=F32)
            n_ref[b, p] = g_slab * n0 + jnp.where(same, dn, 0.0)
            mean = half_sum(y) * (1.0 / HEAD_DIM)
            yc = y - mean
            var = half_sum(yc * yc) * (1.0 / HEAD_DIM)
            yn = yc * lax.rsqrt(var + LNX_EPS) * lw_ref[:, cols] + lb_ref[:, cols]
            y_ref[b, rows, cols] = yn + bonus_ref[b, rows, cols]


def _wkv_chunked(at, rt, bt, kt, vb, ge, bonus, lnw, lnb):
    B, S, W = at.shape
    ts = WKV_ROWS
    nb = WKV_SEQS if B % WKV_SEQS == 0 else 1
    n_c = ts // CHUNK
    tile = lambda b, i: (b, i, 0)
    full = lambda b, i: (0, 0)
    tok = pl.BlockSpec((nb, ts, W), tile)
    return pl.pallas_call(
        _wkv_chunk_kernel,
        grid=(B // nb, S // ts),
        in_specs=[tok] * 5 + [pl.BlockSpec((nb, n_c, 1, W), lambda b, i: (b, i, 0, 0)), tok,
                              pl.BlockSpec((1, W), full), pl.BlockSpec((1, W), full)],
        out_specs=tok,
        out_shape=jax.ShapeDtypeStruct((B, S, W), F32),
        scratch_shapes=[pltpu.VMEM((nb, RWKV_HEADS // 2, 2 * HEAD_DIM, LANES), F32)],
        compiler_params=_cparams(("parallel", "arbitrary"), 32),
        name="wkv_chunked",
    )(at, rt, bt, kt, vb, ge, bonus, lnw, lnb)


def _band_attn_kernel(q_ref, k_ref, v_ref, bias_ref, o_ref, s_ref, e_ref, kz_ref, vz_ref):
    i = pl.program_id(1)
    head_rows = kz_ref.shape[0] - BAND_PAD

    @pl.when(i == 0)
    def _():
        kz_ref[0:BAND_PAD, :] = jnp.zeros((BAND_PAD, kz_ref.shape[1]), kz_ref.dtype)
        vz_ref[0:BAND_PAD, :] = jnp.zeros((BAND_PAD, vz_ref.shape[1]), vz_ref.dtype)
        kz_ref[BAND_PAD:, :] = k_ref[0:head_rows, :]
        vz_ref[BAND_PAD:, :] = v_ref[0:head_rows, :]

    n_chunks = q_ref.shape[0] // CHUNK
    n_pairs = ATT_HEADS // 2
    lane = lax.broadcasted_iota(jnp.int32, (CHUNK, LANES), 1)
    even = lane < HEAD_DIM
    slot = lax.broadcasted_iota(jnp.int32, (1, 1, BAND), 2)

    group = s_ref.shape[0]

    def chunk_body(masked, cc, carry):
        cgs = [i * n_chunks + cc * group + j for j in range(group)]
        if masked:
            kv_k, kv_v = kz_ref, vz_ref
            starts = [pl.multiple_of(cg * CHUNK, CHUNK) for cg in cgs]
        else:
            kv_k, kv_v = k_ref, v_ref
            starts = [pl.multiple_of((cg - PREV_CHUNKS) * CHUNK, CHUNK) for cg in cgs]
        rows = [pl.multiple_of((cc * group + j) * CHUNK, CHUNK) for j in range(group)]
        for j in range(group):
            for p in range(n_pairs):
                cols = slice(p * LANES, (p + 1) * LANES)
                q2 = q_ref[pl.ds(rows[j], CHUNK), cols]
                zero = jnp.zeros_like(q2)
                qs = jnp.concatenate([jnp.where(even, q2, zero), jnp.where(even, zero, q2)], axis=0)
                s_ref[j, p] = _dot_nt(qs, kv_k[pl.ds(starts[j], BAND), cols]) + bias_ref[p]
        inv = []
        for j in range(group):
            s = s_ref[j]
            if masked:
                s = jnp.where(slot >= (BAND_PAD - cgs[j] * CHUNK), s, NEG_INF)
            e = jnp.exp2(s - jnp.max(s, axis=-1, keepdims=True))
            inv.append(1.0 / jnp.sum(e, axis=-1, keepdims=True))
            e_ref[j] = e.astype(BF16)
        for j in range(group):
            for p in range(n_pairs):
                cols = slice(p * LANES, (p + 1) * LANES)
                o = _dot(e_ref[j, p], kv_v[pl.ds(starts[j], BAND), cols]) * inv[j][p]
                o_ref[pl.ds(rows[j], CHUNK), cols] = jnp.where(even, o[:CHUNK], o[CHUNK:])
        return carry

    @pl.when(i * n_chunks < PREV_CHUNKS)
    def _():
        lax.fori_loop(0, n_chunks // group, functools.partial(chunk_body, True), 0)

    @pl.when(i * n_chunks >= PREV_CHUNKS)
    def _():
        lax.fori_loop(0, n_chunks // group, functools.partial(chunk_body, False), 0)


def _band_attn(q3, k3, v3, bias_pairs):
    B, S, W = q3.shape
    SP = S
    tq = ATT_ROWS
    assert PREV_CHUNKS % (tq // CHUNK) == 0
    n_pairs = ATT_HEADS // 2
    return pl.pallas_call(
        _band_attn_kernel,
        grid=(B, S // tq),
        in_specs=[pl.BlockSpec((None, tq, W), lambda b, i: (b, i, 0)),
                  pl.BlockSpec((None, SP, W), lambda b, i: (b, 0, 0)),
                  pl.BlockSpec((None, SP, W), lambda b, i: (b, 0, 0)),
                  pl.BlockSpec((ATT_HEADS // 2, 2 * CHUNK, BAND), lambda b, i: (0, 0, 0))],
        out_specs=pl.BlockSpec((None, tq, W), lambda b, i: (b, i, 0)),
        out_shape=jax.ShapeDtypeStruct((B, S, W), F32),
        scratch_shapes=[pltpu.VMEM((ATT_GROUP, n_pairs, 2 * CHUNK, BAND), F32),
                        pltpu.VMEM((ATT_GROUP, n_pairs, 2 * CHUNK, BAND), BF16),
                        pltpu.VMEM((BAND_PAD + PREV_CHUNKS * CHUNK, W), BF16),
                        pltpu.VMEM((BAND_PAD + PREV_CHUNKS * CHUNK, W), BF16)],
        compiler_params=_cparams(("parallel", "arbitrary"), 48),
        name="band_attn",
    )(q3, k3, v3, bias_pairs)


def _mem_kv_kernel(m_ref, g_ref, w_ref, kn_ref, k_ref, v_ref):
    h = _rms(m_ref[...], g_ref[...]).astype(BF16)
    kv = _dot(h, w_ref[...])
    for hd in range(XA_HEADS):
        cols = slice(hd * XA_HEAD_DIM, (hd + 1) * XA_HEAD_DIM)
        k_ref[:, cols] = _rms(kv[:, cols], kn_ref[...]).astype(BF16)
    v_ref[...] = kv[:, XA_WIDTH:].astype(BF16)


def _mem_kv(mem2d, g, w_kv, k_norm):
    R, D = mem2d.shape
    tm = ROW_TILE
    full = lambda i: (0, 0)
    row = lambda i: (i, 0)
    return pl.pallas_call(
        _mem_kv_kernel,
        grid=(R // tm,),
        in_specs=[pl.BlockSpec((tm, D), row), pl.BlockSpec((1, D), full),
                  pl.BlockSpec((D, 2 * XA_WIDTH), full), pl.BlockSpec((1, XA_HEAD_DIM), full)],
        out_specs=[pl.BlockSpec((tm, XA_WIDTH), row)] * 2,
        out_shape=[jax.ShapeDtypeStruct((R, XA_WIDTH), BF16)] * 2,
        compiler_params=_cparams(("parallel",), 32),
        name="mem_kv",
    )(mem2d, g, w_kv, k_norm)


def _mix_xa_router_kernel(x_ref, yr_ref, g_ref, ya_ref, gao_ref, wout_ref, gxa_ref, wq_ref, qn_ref,
                          km_ref, vm_ref, wo_ref, gff_ref, wrt_ref, brt_ref,
                          tri_ref, xo_ref, h_ref, rt_ref, cnt_ref, run_ref, stage_ref):
    first = (pl.program_id(0) == 0) & (pl.program_id(1) == 0)

    @pl.when(first)
    def _():
        run_ref[...] = jnp.zeros_like(run_ref)

    n_rows = x_ref.shape[0]
    groups = [slice(r, r + n_rows // MIX_GROUPS) for r in range(0, n_rows, n_rows // MIX_GROUPS)]
    head_cols = [slice(hd * XA_HEAD_DIM, (hd + 1) * XA_HEAD_DIM) for hd in range(XA_HEADS)]

    yr = [(yr_ref[rows, :] * g_ref[rows, :]).astype(BF16) for rows in groups]
    ya = [_rms(ya_ref[rows, :], gao_ref[...]).astype(BF16) for rows in groups]
    x1 = [x_ref[rows, :] + _dot(a, wout_ref[0:RWKV_WIDTH, :]) + _dot(b, wout_ref[RWKV_WIDTH:, :])
          for rows, a, b in zip(groups, yr, ya)]

    q = [_dot(_rms(xg, gxa_ref[...]).astype(BF16), wq_ref[...]) for xg in x1]
    qscale = XA_HEAD_DIM ** -0.5 * LOG2E
    qh = [[(_rms(qg[:, cols], qn_ref[...]) * qscale).astype(BF16) for cols in head_cols] for qg in q]
    s = [[_dot_nt(qg[hd], km_ref[:, cols]) for hd, cols in enumerate(head_cols)] for qg in qh]
    e = [[jnp.exp2(sh - jnp.max(sh, axis=-1, keepdims=True)) for sh in sg] for sg in s]
    inv = [[1.0 / jnp.sum(eh, axis=-1, keepdims=True) for eh in eg] for eg in e]
    o = [jnp.concatenate([_dot(eg[hd].astype(BF16), vm_ref[:, cols]) * ig[hd]
                          for hd, cols in enumerate(head_cols)], axis=-1).astype(BF16)
         for eg, ig in zip(e, inv)]
    x = jnp.concatenate([xg + _dot(og, wo_ref[...]) for xg, og in zip(x1, o)], axis=0)
    xo_ref[...] = x

    h = _rms(x, gff_ref[...])
    tm = h.shape[0]
    _store_row_tiles(stage_ref, h)
    h_ref[...] = stage_ref[...].astype(BF16)
    logits =_dot(h.astype(BF16), wrt_ref[...]) + brt_ref[...]
    lane_i = lax.broadcasted_iota(jnp.int32, logits.shape, 1)
    lane = lane_i.astype(F32)
    big = float(ROUTE_LANES)
    gmask = lane < N_GROUPS
    gmax = jnp.max(jnp.where(gmask, logits, -jnp.inf), axis=-1, keepdims=True)
    ge = jnp.where(gmask, jnp.exp(logits - gmax), 0.0)
    gp = ge / jnp.sum(ge, axis=-1, keepdims=True)
    g_gate = jnp.max(gp, axis=-1, keepdims=True)
    g_top = jnp.min(jnp.where(gmask & (gp == g_gate), lane, big), axis=-1, keepdims=True)
    lo = N_GROUPS + g_top * EXPERTS_PER_GROUP
    rmask = (lane >= lo) & (lane < lo + EXPERTS_PER_GROUP)
    rmax = jnp.max(jnp.where(rmask, logits, -jnp.inf), axis=-1, keepdims=True)
    re = jnp.where(rmask, jnp.exp(logits - rmax), 0.0)
    p_in = re / jnp.sum(re, axis=-1, keepdims=True)
    p1 = jnp.max(jnp.where(rmask, p_in, -1.0), axis=-1, keepdims=True)
    j1 = jnp.min(jnp.where(rmask & (p_in == p1), lane, big), axis=-1, keepdims=True)
    rmask2 = rmask & (lane != j1)
    p2 = jnp.max(jnp.where(rmask2, p_in, -1.0), axis=-1, keepdims=True)
    j2 = jnp.min(jnp.where(rmask2 & (p_in == p2), lane, big), axis=-1, keepdims=True)
    denom = p1 + p2
    w1 = g_gate * (p1 / denom)
    w2 = g_gate * (p2 / denom)
    e1 = j1 - N_GROUPS
    e2 = j2 - N_GROUPS
    hit1 = lane == j1
    hit2 = lane == j2
    hits = jnp.where(hit1 | hit2, 1.0, 0.0)
    before = _dot(tri_ref[...], hits.astype(BF16)) + run_ref[...]
    rank1 = jnp.sum(jnp.where(hit1, before, 0.0), axis=-1, keepdims=True)
    rank2 = jnp.sum(jnp.where(hit2, before, 0.0), axis=-1, keepdims=True)
    run = run_ref[...] + jnp.sum(hits, axis=0, keepdims=True)
    run_ref[...] = run
    cnt_ref[...] = jnp.broadcast_to(run, cnt_ref.shape)
    rt_ref[...] = jnp.where(lane_i == 0, e1, jnp.where(lane_i == 1, e2,
                            jnp.where(lane_i == 2, w1, jnp.where(lane_i == 3, w2,
                                      jnp.where(lane_i == 4, rank1, jnp.where(lane_i == 5, rank2, 0.0))))))


def _mix_xa_router(x3, yr3, g3, ya3, gao, wout, gxa, wq, qn, km3, vm3, wo, gff, wrt, brt):
    B, S, D = x3.shape
    M = km3.shape[1]
    tm = MIX_ROWS
    tile = lambda b, i: (b, i, 0)
    full = lambda b, i: (0, 0)
    memb = lambda b, i: (b, 0, 0)
    W = RWKV_WIDTH
    n_i = S // tm
    tri = jnp.tril(jnp.ones((tm, tm), F32), -1).astype(BF16)
    return pl.pallas_call(
        _mix_xa_router_kernel,
        grid=(B, S // tm),
        in_specs=[pl.BlockSpec((None, tm, D), tile), pl.BlockSpec((None, tm, W), tile),
                  pl.BlockSpec((None, tm, W), tile), pl.BlockSpec((None, tm, W), tile),
                  pl.BlockSpec((1, W), full), pl.BlockSpec((2 * W, D), full),
                  pl.BlockSpec((1, D), full), pl.BlockSpec((D, XA_WIDTH), full),
                  pl.BlockSpec((1, XA_HEAD_DIM), full),
                  pl.BlockSpec((None, M, XA_WIDTH), memb), pl.BlockSpec((None, M, XA_WIDTH), memb),
                  pl.BlockSpec((XA_WIDTH, D), full), pl.BlockSpec((1, D), full),
                  pl.BlockSpec((D, ROUTE_LANES), full), pl.BlockSpec((1, ROUTE_LANES), full),
                  pl.BlockSpec((tm, tm), full)],
        out_specs=[pl.BlockSpec((None, tm, D), tile),
                   pl.BlockSpec((tm * SUBLANES, LANES), lambda b, i: (b * n_i + i, 0)),
                   pl.BlockSpec((None, tm, ROUTE_LANES), tile),
                   pl.BlockSpec((SUBLANES, ROUTE_LANES), full)],
        out_shape=[jax.ShapeDtypeStruct((B, S, D), F32),
                   jax.ShapeDtypeStruct((B * S * SUBLANES, LANES), BF16),
                   jax.ShapeDtypeStruct((B, S, ROUTE_LANES), F32),
                   jax.ShapeDtypeStruct((SUBLANES, ROUTE_LANES), F32)],
        scratch_shapes=[pltpu.VMEM((1, ROUTE_LANES), F32), pltpu.VMEM((tm * SUBLANES, LANES), F32)],
        compiler_params=_cparams(("arbitrary", "arbitrary"), 48),
        name="mix_xa_router",
    )(x3, yr3, g3, ya3, gao, wout, gxa, wq, qn, km3, vm3, wo, gff, wrt, brt, tri)


def _row_copy(src_ref, src_off, dst_ref, dst_off, sem):
    src = src_ref.at[pl.ds(pl.multiple_of(src_off, SUBLANES), SUBLANES)]
    dst = dst_ref.at[pl.ds(pl.multiple_of(dst_off, SUBLANES), SUBLANES)]
    return pltpu.make_async_copy(src, dst, sem)


def _load_row_tiles(stage_ref, rows):
    return jnp.concatenate([stage_ref[pl.ds(j, rows, stride=SUBLANES), :] for j in range(SUBLANES)], axis=-1)


def _store_row_tiles(stage_ref, x):
    for j in range(SUBLANES):
        stage_ref[pl.ds(j, x.shape[0], stride=SUBLANES), :] = x[:, j * LANES:(j + 1) * LANES]


def _dispatch_kernel(dest_ref, h_ref, zeros_ref, xs_ref, sem):
    del zeros_ref
    tm = h_ref.shape[0] // SUBLANES
    base = pl.program_id(0) * (tm * TOP_K)

    def issue(r, carry):
        for kk in range(TOP_K):
            _row_copy(h_ref, r * SUBLANES, xs_ref, dest_ref[base + r * TOP_K + kk], sem).start(priority=kk)
        return carry

    lax.fori_loop(0, tm, issue, 0, unroll=DMA_UNROLL)

    def drain(r, carry):
        for kk in range(TOP_K):
            _row_copy(h_ref, r * SUBLANES, xs_ref, dest_ref[base + r * TOP_K + kk], sem).wait()
        return carry

    lax.fori_loop(0, tm, drain, 0, unroll=DMA_UNROLL)


def _dispatch(dest, h_tiles, n_slots):
    tm = MOE_ROWS
    T = h_tiles.shape[0] // SUBLANES
    zeros = jnp.zeros((n_slots * SUBLANES, LANES), BF16)
    return pl.pallas_call(
        _dispatch_kernel,
        grid_spec=pltpu.PrefetchScalarGridSpec(
            num_scalar_prefetch=1,
            grid=(T // tm,),
            in_specs=[pl.BlockSpec((tm * SUBLANES, LANES), lambda i, d: (i, 0)),
                      pl.BlockSpec(memory_space=pl.ANY)],
            out_specs=pl.BlockSpec(memory_space=pl.ANY),
            scratch_shapes=[pltpu.SemaphoreType.DMA(())],
        ),
        out_shape=jax.ShapeDtypeStruct((n_slots * SUBLANES, LANES), BF16),
        input_output_aliases={2: 0},
        compiler_params=_cparams(("arbitrary",), 32),
        name="moe_dispatch",
    )(dest, h_tiles, zeros)


def _expert_kernel(be_ref, nu_ref, xs_ref, wg_ref, wu_ref, wd_ref, y_ref, wgb_ref, wub_ref, wdb_ref,
                   stage_ref):
    i = pl.program_id(0)
    used = i < nu_ref[0]
    new_expert = (i == 0) | (be_ref[i] != be_ref[jnp.maximum(i - 1, 0)])

    @pl.when(used & new_expert)
    def _():
        wgb_ref[...] = wg_ref[...].astype(BF16)
        wub_ref[...] = wu_ref[...].astype(BF16)
        wdb_ref[...] = wd_ref[...].astype(BF16)

    @pl.when(used)
    def _():
        stage_ref[...] = xs_ref[...].astype(F32)
        xb = _load_row_tiles(stage_ref, MOE_BLOCK).astype(BF16)
        gate = _dot(xb, wgb_ref[...])
        up = _dot(xb, wub_ref[...])
        act = (gate * jax.nn.sigmoid(gate) * up).astype(BF16)
        _store_row_tiles(stage_ref, _dot(act, wdb_ref[...]))
        y_ref[...] = stage_ref[...].astype(BF16)

    @pl.when(jnp.logical_not(used))
    def _():
        y_ref[...] = jnp.zeros_like(y_ref)


def _expert_ffn(block_expert, n_used, xs_tiles, layer, wg, wu, wd):
    D, FF = wg.shape[2], wg.shape[3]
    n_blocks = xs_tiles.shape[0] // (MOE_BLOCK * SUBLANES)
    rows = MOE_BLOCK * SUBLANES
    blk = lambda i, be, nu: (jnp.minimum(i, nu[0] - 1), 0)
    wsel = lambda i, be, nu: (layer, be[jnp.minimum(i, nu[0] - 1)], 0, 0)
    return pl.pallas_call(
        _expert_kernel,
        grid_spec=pltpu.PrefetchScalarGridSpec(
            num_scalar_prefetch=2,
            grid=(n_blocks,),
            in_specs=[pl.BlockSpec((rows, LANES), blk),
                      pl.BlockSpec((None, None, D, FF), wsel),
                      pl.BlockSpec((None, None, D, FF), wsel),
                      pl.BlockSpec((None, None, FF, D), wsel)],
            out_specs=pl.BlockSpec((rows, LANES), lambda i, be, nu: (i, 0)),
            scratch_shapes=[pltpu.VMEM((D, FF), BF16), pltpu.VMEM((D, FF), BF16), pltpu.VMEM((FF, D), BF16),
                            pltpu.VMEM((rows, LANES), F32)],
        ),
        out_shape=jax.ShapeDtypeStruct(xs_tiles.shape, BF16),
        compiler_params=_cparams(("arbitrary",), 48),
        name="moe_experts",
    )(block_expert, n_used, xs_tiles, wg, wu, wd)


def _combine_kernel(dest_ref, x_ref, wt_ref, yb_ref, o_ref, buf_ref, sem, stage_ref):
    tm = x_ref.shape[0]
    base = pl.program_id(0) * (tm * TOP_K)

    def issue(r, carry):
        for kk in range(TOP_K):
            _row_copy(yb_ref, dest_ref[base + r * TOP_K + kk], buf_ref.at[kk], r * SUBLANES, sem).start(priority=kk)
        return carry

    lax.fori_loop(0, tm, issue, 0, unroll=DMA_UNROLL)

    def drain(r, carry):
        for kk in range(TOP_K):
            _row_copy(yb_ref, dest_ref[base + r * TOP_K + kk], buf_ref.at[kk], r * SUBLANES, sem).wait()
        return carry

    lax.fori_loop(0, tm, drain, 0, unroll=DMA_UNROLL)
    wt = wt_ref[...]
    w1 = wt[:, 2:3]
    w2 = wt[:, 3:4]
    stage_ref[...] = buf_ref[...].astype(F32)
    for j in range(SUBLANES):
        cols = slice(j * LANES, (j + 1) * LANES)
        y1 = stage_ref[0, pl.ds(j, tm, stride=SUBLANES), :]
        y2 = stage_ref[1, pl.ds(j, tm, stride=SUBLANES), :]
        o_ref[:, cols] = x_ref[:, cols] + (y1 * w1 + y2 * w2)


def _combine(dest, x2d, rt2d, yb_tiles):
    T, D = x2d.shape
    tm = MOE_ROWS
    return pl.pallas_call(
        _combine_kernel,
        grid_spec=pltpu.PrefetchScalarGridSpec(
            num_scalar_prefetch=1,
            grid=(T // tm,),
            in_specs=[pl.BlockSpec((tm, D), lambda i, d: (i, 0)),
                      pl.BlockSpec((tm, ROUTE_LANES), lambda i, d: (i, 0)),
                      pl.BlockSpec(memory_space=pl.ANY)],
            out_specs=pl.BlockSpec((tm, D), lambda i, d: (i, 0)),
            scratch_shapes=[pltpu.VMEM((TOP_K, tm * SUBLANES, LANES), BF16), pltpu.SemaphoreType.DMA(()),
                            pltpu.VMEM((TOP_K, tm * SUBLANES, LANES), F32)],
        ),
        out_shape=jax.ShapeDtypeStruct((T, D), F32),
        compiler_params=_cparams(("arbitrary",), 48),
        name="moe_combine",
    )(dest, x2d, rt2d, yb_tiles)


def _moe_plan(rt2d, counts_lanes):
    T = rt2d.shape[0]
    A = T * TOP_K
    e = rt2d[:, 0:TOP_K].astype(jnp.int32).reshape(A)
    rank = rt2d[:, 4:4 + TOP_K].astype(jnp.int32).reshape(A)
    counts = counts_lanes[0, N_GROUPS:N_GROUPS + N_EXPERTS].astype(jnp.int32)
    padded = ((counts + MOE_BLOCK - 1) // MOE_BLOCK) * MOE_BLOCK
    pends = jnp.cumsum(padded)
    pstarts = pends - padded
    onehot = e[:, None] == jnp.arange(N_EXPERTS, dtype=jnp.int32)[None, :]
    dest = jnp.sum(jnp.where(onehot, pstarts[None, :], 0), axis=1) + rank
    n_blocks = (A + N_EXPERTS * (MOE_BLOCK - 1) + MOE_BLOCK - 1) // MOE_BLOCK
    block_start = jnp.arange(n_blocks, dtype=jnp.int32) * MOE_BLOCK
    block_expert = jnp.minimum(jnp.sum(pends[None, :] <= block_start[:, None], axis=1), N_EXPERTS - 1)
    n_used = (pends[-1] // MOE_BLOCK).astype(jnp.int32).reshape(1)
    dest_off = (dest * SUBLANES).astype(jnp.int32)
    return dest_off, block_expert.astype(jnp.int32), n_used, n_blocks * MOE_BLOCK


def _rel_bias_pairs(table):
    n_ext = BAND + CHUNK - 1
    ext = jnp.concatenate([table, jnp.broadcast_to(table[:, -1:], (table.shape[0], n_ext - table.shape[1]))], axis=1)
    rev = ext[:, ::-1]
    bias = jnp.stack([rev[:, CHUNK - 1 - i:CHUNK - 1 - i + BAND] for i in range(CHUNK)], axis=1)
    return (bias.astype(F32) * LOG2E).reshape(-1, ATT_HEADS // 2, 2 * CHUNK, BAND)


def _pad_rows(w, rows, offset=0):
    out = jnp.zeros((rows, w.shape[1]), w.dtype)
    return out.at[offset:offset + w.shape[0]].set(w)


def kernel(x, mem, norm_mix, w_in, shift_mu, decay_w0, decay_w2, iclr_a0, iclr_a2, gate_g2, rw_k_k, rw_k_a, rw_r_k, lnx_w, lnx_b, vres_v0, vres_v1, vres_v2, att_q_norm, att_k_norm, att_rel_bias, att_out_norm, w_out, norm_xa, norm_mem, xa_w_q, xa_w_kv, xa_w_o, xa_q_norm, xa_k_norm, norm_ffn, moe_w_group, moe_b_group, moe_w_route, moe_b_route, moe_w_gate, moe_w_up, moe_w_down):
    B, S, D = x.shape
    T = B * S
    depth = w_in.shape[0]
    M = mem.shape[1]
    row = lambda p: p.reshape(1, -1)
    ones_bd = jnp.kron(jnp.eye(ATT_HEADS, dtype=F32), jnp.ones((HEAD_DIM, HEAD_DIM), F32)).astype(BF16)

    bias_pairs = _rel_bias_pairs(att_rel_bias.reshape(depth * ATT_HEADS, -1))

    v_first = None
    for l in range(depth):
        u, qh, kh, vh = _inproj(x.reshape(T, D), row(norm_mix[l]), w_in[l].astype(BF16),
                                row(jnp.tile(att_q_norm[l], ATT_HEADS)),
                                row(jnp.tile(att_k_norm[l], ATT_HEADS)))
        w2p = _pad_rows(decay_w2[l], LANES, 0).astype(BF16)
        a2p = _pad_rows(iclr_a2[l], LANES, DECAY_LORA).astype(BF16)
        vres = None
        if l > 0:
            v1p = jnp.zeros((RWKV_WIDTH, LANES), F32).at[:, :MV_LORA].set(vres_v1[l - 1]).astype(BF16)
            v2p = _pad_rows(vres_v2[l - 1], LANES, 0).astype(BF16)
            vres = (v_first, row(vres_v0[l - 1]), v1p, v2p)
        prep = _rwkv_prep(u.reshape(B, S, RWKV_IN), row(shift_mu[l]), row(decay_w0[l]),
                          w2p, row(iclr_a0[l]), a2p, gate_g2[l].astype(BF16),
                          row(rw_k_k[l]), row(rw_k_a[l]), row(rw_r_k[l]), ones_bd, vres)
        at3, rt3, bt3, kt3, vb3, ge4, bonus3, g3 = prep[:8]
        if l == 0:
            v_first = prep[8]
        yr3 = _wkv_chunked(at3, rt3, bt3, kt3, vb3, ge4, bonus3, row(lnx_w[l]), row(lnx_b[l]))
        ya3 = _band_attn(qh.reshape(B, S, ATT_WIDTH), kh.reshape(B, S, ATT_WIDTH),
                         vh.reshape(B, S, ATT_WIDTH), bias_pairs[l])
        km, vm = _mem_kv(mem.reshape(B * M, D), row(norm_mem[l]), xa_w_kv[l].astype(BF16), row(xa_k_norm[l]))
        wrt = jnp.zeros((D, ROUTE_LANES), F32)
        wrt = wrt.at[:, :N_GROUPS].set(moe_w_group[l]).at[:, N_GROUPS:N_GROUPS + N_EXPERTS].set(moe_w_route[l])
        brt = jnp.zeros((ROUTE_LANES,), F32)
        brt = brt.at[:N_GROUPS].set(moe_b_group[l]).at[N_GROUPS:N_GROUPS + N_EXPERTS].set(moe_b_route[l])
        x3, h_tiles, rt3, counts = _mix_xa_router(
            x, yr3, g3, ya3, row(att_out_norm[l]), w_out[l].astype(BF16), row(norm_xa[l]),
            xa_w_q[l].astype(BF16), row(xa_q_norm[l]), km.reshape(B, M, XA_WIDTH), vm.reshape(B, M, XA_WIDTH),
            xa_w_o[l].astype(BF16), row(norm_ffn[l]), wrt.astype(BF16), row(brt))
        rt2d = rt3.reshape(T, ROUTE_LANES)
        dest, block_expert, n_used, n_slots = _moe_plan(rt2d, counts)
        xs = _dispatch(dest, h_tiles, n_slots)
        yb = _expert_ffn(block_expert, n_used, xs, l, moe_w_gate, moe_w_up, moe_w_down)
        x = _combine(dest, x3.reshape(T, D), rt2d, yb).reshape(B, S, D)
    return x
```

```python
import functools

import jax
import jax.numpy as jnp
from jax import lax
from jax.experimental import pallas as pl
from jax.experimental.pallas import tpu as pltpu

F32 = jnp.float32
BF16 = jnp.bfloat16

CHUNK = 64
RWKV_HEADS = 8
HEAD_DIM = 64
RWKV_WIDTH = 512
ATT_HEADS = 8
ATT_WIDTH = 512
DECAY_LORA = 64
AAA_LORA = 64
MV_LORA = 32
GATE_LORA = 128
RWKV_IN = 3 * RWKV_WIDTH + DECAY_LORA + AAA_LORA + GATE_LORA
LORA_OFF = 3 * RWKV_WIDTH
GATE_OFF = LORA_OFF + DECAY_LORA + AAA_LORA
IN_WIDTH = RWKV_IN + 3 * ATT_WIDTH
PREV_CHUNKS = 8
BAND = (PREV_CHUNKS + 1) * CHUNK
BAND_PAD = PREV_CHUNKS * CHUNK
REL_MAX = 256
XA_HEADS = 4
XA_HEAD_DIM = 128
XA_WIDTH = 512
N_GROUPS = 4
EXPERTS_PER_GROUP = 8
N_EXPERTS = 32
TOP_K = 2
EXPERT_FF = 512
MOE_BLOCK = 512
NORM_EPS = 1e-6
LNX_EPS = 64e-5
NEG_INF = -1e30

LANES = 128
SUBLANES = 8
ROW_TILE = 256
INPROJ_ROWS = 512
PREP_ROWS = 512
MIX_ROWS = 512
MIX_GROUPS = 2
LOG2E = 1.4426950408889634
WKV_ROWS = 256
WKV_GROUP_CHUNKS = 2
WKV_SEQS = 2
NEUMANN_SQUARINGS =CHUNK.bit_length() - 2
ATT_ROWS = 512
ATT_GROUP = 2
MOE_ROWS = 1024
ROUTE_LANES = 128
DMA_UNROLL = 8


def _cparams(semantics, vmem_mib):
    return pltpu.CompilerParams(dimension_semantics=semantics,
                                vmem_limit_bytes=vmem_mib * 1024 * 1024)


def _dot(a, b):
    return jnp.dot(a, b, preferred_element_type=F32)


def _dot_nt(a, b):
    return lax.dot_general(a, b, (((1,), (1,)), ((), ())), preferred_element_type=F32)


def _rms(x, g):
    ms = jnp.mean(x * x, axis=-1, keepdims=True)
    return x * lax.rsqrt(ms + NORM_EPS) * g


def _group_sum(x, ones_bd):
    hi = x.astype(BF16)
    lo = (x - hi.astype(F32)).astype(BF16)
    return _dot(hi, ones_bd) + _dot(lo, ones_bd)


def _group_sumsq(x, ones_bd):
    return _group_sum(x * x, ones_bd)


def _inproj_kernel(x_ref, g_ref, w_ref, gq_ref, gk_ref, u_ref, q_ref, k_ref, v_ref):
    h = _rms(x_ref[...], g_ref[...]).astype(BF16)
    for j in range(0, RWKV_IN, 256):
        u_ref[:, j:j + 256] = _dot(h, w_ref[:, j:j + 256])
    even = lax.broadcasted_iota(jnp.int32, (x_ref.shape[0], LANES), 1) < HEAD_DIM

    def head_rms(x, gain_ref, scale):
        for j in range(0, ATT_WIDTH, LANES):
            xs = x[:, j:j + LANES]
            sq = xs * xs
            ss = jnp.where(even, jnp.sum(jnp.where(even, sq, 0.0), axis=-1, keepdims=True),
                           jnp.sum(jnp.where(even, 0.0, sq), axis=-1, keepdims=True))
            yield j, xs * lax.rsqrt(ss * (1.0 / HEAD_DIM) + NORM_EPS) * (gain_ref[:, j:j + LANES] * scale)

    q = _dot(h, w_ref[:, RWKV_IN:RWKV_IN + ATT_WIDTH])
    for j, qn in head_rms(q, gq_ref, HEAD_DIM ** -0.5 * LOG2E):
        q_ref[:, j:j + LANES] = qn.astype(BF16)
    k = _dot(h, w_ref[:, RWKV_IN + ATT_WIDTH:RWKV_IN + 2 * ATT_WIDTH])
    for j, kn in head_rms(k, gk_ref, 1.0):
        k_ref[:, j:j + LANES] = kn.astype(BF16)
    v_ref[...] = _dot(h, w_ref[:, RWKV_IN + 2 * ATT_WIDTH:]).astype(BF16)


def _inproj(x2d, g, w, gq, gk):
    T, D = x2d.shape
    tm = INPROJ_ROWS
    full = lambda i: (0, 0)
    row = lambda i: (i, 0)
    return pl.pallas_call(
        _inproj_kernel,
        grid=(T // tm,),
        in_specs=[pl.BlockSpec((tm, D), row), pl.BlockSpec((1, D), full),
                  pl.BlockSpec((D, IN_WIDTH), full), pl.BlockSpec((1, ATT_WIDTH), full),
                  pl.BlockSpec((1, ATT_WIDTH), full)],
        out_specs=[pl.BlockSpec((tm, RWKV_IN), row), pl.BlockSpec((tm, ATT_WIDTH), row),
                   pl.BlockSpec((tm, ATT_WIDTH), row), pl.BlockSpec((tm, ATT_WIDTH), row)],
        out_shape=[jax.ShapeDtypeStruct((T, RWKV_IN), F32), jax.ShapeDtypeStruct((T, ATT_WIDTH), BF16),
                   jax.ShapeDtypeStruct((T, ATT_WIDTH), BF16), jax.ShapeDtypeStruct((T, ATT_WIDTH), BF16)],
        compiler_params=_cparams(("parallel",), 48),
        name="inproj",
    )(x2d, g, w, gq, gk)


def _rwkv_prep_kernel(has_vres, *refs):
    if has_vres:
        (u_ref, up_ref, mu_ref, w0_ref, w2_ref, a0_ref, a2_ref, g2_ref, kkp_ref, kap_ref, rkp_ref,
         bd_ref, tri_ref, vf_ref, v0_ref, v1_ref, v2_ref,
         at_ref, rt_ref, bt_ref, kt_ref, vb_ref, ge_ref, bonus_ref, g_ref) = refs
    else:
        (u_ref, up_ref, mu_ref, w0_ref, w2_ref, a0_ref, a2_ref, g2_ref, kkp_ref, kap_ref, rkp_ref,
         bd_ref, tri_ref,
         at_ref, rt_ref, bt_ref, kt_ref, vb_ref, ge_ref, bonus_ref, g_ref, v_ref) = refs
    i = pl.program_id(1)
    u = u_ref[...]
    ts = u.shape[0]
    prev_row = jnp.where(i > 0, up_ref[SUBLANES - 1:SUBLANES, :], 0.0)
    rolled = pltpu.roll(u, 1, 0)
    row_id = lax.broadcasted_iota(jnp.int32, u.shape, 0)
    shifted = jnp.where(row_id == 0, jnp.broadcast_to(prev_row, u.shape), rolled)
    u = u + mu_ref[...] * (shifted - u)

    r = u[:, 0:RWKV_WIDTH]
    k = u[:, RWKV_WIDTH:2 * RWKV_WIDTH]
    v = u[:, 2 * RWKV_WIDTH:3 * RWKV_WIDTH]
    lora = u[:, LORA_OFF:GATE_OFF]
    gl = u[:, GATE_OFF:RWKV_IN]

    w = w0_ref[...] + _dot(jnp.tanh(lora).astype(BF16), w2_ref[...])
    z = -w
    softplus = jnp.maximum(z, 0.0) + jnp.log(1.0 + jnp.exp(-jnp.abs(z)))
    w = -softplus - 0.5
    log_decay = -jnp.exp(w)
    a = jax.nn.sigmoid(a0_ref[...] + _dot(lora.astype(BF16), a2_ref[...]))
    g_ref[...] = _dot(jax.nn.sigmoid(gl).astype(BF16), g2_ref[...])
    if has_vres:
        mix = _dot(_dot(v.astype(BF16), v1_ref[...]).astype(BF16), v2_ref[...])
        v = v + (vf_ref[...] - v) * jax.nn.sigmoid(v0_ref[...] + mix)
    else:
        v_ref[...] = v

    bd = bd_ref[...]
    kk = k * kkp_ref[...]
    kk = kk * lax.rsqrt(jnp.maximum(_group_sumsq(kk, bd), 1e-24))
    k2 = k * (1.0 + (a - 1.0) * kap_ref[...])
    bonus_ref[...] = _group_sum(r * k2 * rkp_ref[...], bd) * v

    hi = log_decay.astype(BF16)
    lo = (log_decay - hi.astype(F32)).astype(BF16)
    tri = tri_ref[...]
    cum = _dot(tri, hi) + _dot(tri, lo)
    inv_gamma = jnp.exp(-cum)
    at_ref[...] = (-kk * jnp.exp(cum - log_decay)).astype(BF16)
    rt_ref[...] = (r * jnp.exp(cum)).astype(BF16)
    bt_ref[...] = (kk * a * inv_gamma).astype(BF16)
    kt_ref[...] = (k2 * inv_gamma).astype(BF16)
    vb_ref[...] = v.astype(BF16)
    for c in range(ts // CHUNK):
        ge_ref[c] = jnp.exp(jnp.sum(log_decay[c * CHUNK:(c + 1) * CHUNK, :], axis=0, keepdims=True))


def _rwkv_prep(u3, mu, w0, w2p, a0, a2p, g2, kkp, kap, rkp, ones_bd, vres):
    B, S, _ = u3.shape
    ts = PREP_ROWS
    W = RWKV_WIDTH
    n_c = ts // CHUNK
    full2 = lambda b, i: (0, 0)
    tile = lambda b, i: (b, i, 0)
    prev = lambda b, i: (b, jnp.maximum(i * (ts // SUBLANES) - 1, 0), 0)
    tri = (jnp.tril(jnp.ones((ts, ts), F32))
           * jnp.kron(jnp.eye(n_c, dtype=F32), jnp.ones((CHUNK, CHUNK), F32))).astype(BF16)
    in_specs = [pl.BlockSpec((None, ts, RWKV_IN), tile),
                pl.BlockSpec((None, SUBLANES, RWKV_IN), prev),
                pl.BlockSpec((1, RWKV_IN), full2), pl.BlockSpec((1, W), full2),
                pl.BlockSpec((LANES, W), full2), pl.BlockSpec((1, W), full2),
                pl.BlockSpec((LANES, W), full2), pl.BlockSpec((GATE_LORA, W), full2),
                pl.BlockSpec((1, W), full2), pl.BlockSpec((1, W), full2), pl.BlockSpec((1, W), full2),
                pl.BlockSpec((W, W), full2), pl.BlockSpec((ts, ts), full2)]
    args = [u3, u3, mu, w0, w2p, a0, a2p, g2, kkp, kap, rkp, ones_bd, tri]
    if vres is not None:
        v_first, v0, v1p, v2p = vres
        in_specs += [pl.BlockSpec((None, ts, W), tile), pl.BlockSpec((1, W), full2),
                     pl.BlockSpec((W, LANES), full2), pl.BlockSpec((LANES, W), full2)]
        args += [v_first, v0, v1p, v2p]
    tok = pl.BlockSpec((None, ts, W), tile)
    out_specs = [tok] * 5 + [pl.BlockSpec((None, n_c, 1, W), lambda b, i: (b, i, 0, 0)), tok, tok]
    out_shape = ([jax.ShapeDtypeStruct((B, S, W), BF16)] * 5
                 + [jax.ShapeDtypeStruct((B, S // CHUNK, 1, W), F32)]
                 + [jax.ShapeDtypeStruct((B, S, W), F32)] * 2)
    if vres is None:
        out_specs.append(tok)
        out_shape.append(jax.ShapeDtypeStruct((B, S, W), F32))
    return pl.pallas_call(
        functools.partial(_rwkv_prep_kernel, vres is not None),
        grid=(B, S // ts),
        in_specs=in_specs,
        out_specs=out_specs,
        out_shape=out_shape,
        compiler_params=_cparams(("parallel", "parallel"), 48),
        name="rwkv_prep",
    )(*args)


def _wkv_chunk_kernel(at_ref, rt_ref, bt_ref, kt_ref, vb_ref, ge_ref, bonus_ref, lw_ref, lb_ref,
                      y_ref, n_ref):
    C = CHUNK
    n_pairs = RWKV_HEADS // 2
    n_seqs = at_ref.shape[0]
    n_chunks = at_ref.shape[1] // C

    @pl.when(pl.program_id(1) == 0)
    def _():
        n_ref[...] = jnp.zeros_like(n_ref)

    row = lax.broadcasted_iota(jnp.int32, (2 * C, LANES), 0)
    lane = lax.broadcasted_iota(jnp.int32, (2 * C, LANES), 1)
    top = row < C
    left = lane < HEAD_DIM
    same = top == left
    t_row = jnp.where(top, row, row - C)
    s_col = jnp.where(left, lane, lane - HEAD_DIM)
    strict = s_col < t_row
    incl = s_col <= t_row
    even = lax.broadcasted_iota(jnp.int32, (C, LANES), 1) < HEAD_DIM
    zeros_c = jnp.zeros((C, LANES), BF16)

    unit_groups = [[(c, b, p) for c in range(c0, c0 + WKV_GROUP_CHUNKS) for b in range(n_seqs)
                    for p in range(n_pairs)] for c0 in range(0, n_chunks, WKV_GROUP_CHUNKS)]

    def tile(ref, unit):
        c, b, p = unit
        return ref[b, c * C:(c + 1) * C, p * LANES:(p + 1) * LANES]

    def stack_heads(x):
        z = jnp.zeros_like(x)
        return jnp.concatenate([jnp.where(even, x, z), jnp.where(even, z, x)], axis=0)

    def half_sum(x):
        lo = jnp.sum(jnp.where(even, x, 0.0), axis=-1, keepdims=True)
        hi = jnp.sum(jnp.where(even, 0.0, x), axis=-1, keepdims=True)
        return jnp.where(even, lo, hi)

    for units in unit_groups:
        at2 = [stack_heads(tile(at_ref, un)) for un in units]
        bk = [jnp.concatenate([tile(bt_ref, un), tile(kt_ref, un)], axis=0) for un in units]
        g4 = [_dot_nt(jnp.concatenate([a, stack_heads(tile(rt_ref, un))], axis=0), b)
              for un, a, b in zip(units, at2, bk)]
        ga = [jnp.where(strict, g[:2 * C], 0.0) for g in g4]
        gr = [jnp.where(incl, g[2 * C:], 0.0).astype(BF16) for g in g4]
        zs = [jnp.where(same, _dot(g.astype(BF16), jnp.concatenate([zeros_c, tile(vb_ref, un)], axis=0)), 0.0)
              for un, g in zip(units, ga)]
        pw = [jnp.where(same, jnp.where(top, g, pltpu.roll(g, HEAD_DIM, 1)), 0.0).astype(BF16) for g in ga]
        w = [jnp.concatenate([a.astype(F32), z], axis=1) for a, z in zip(at2, zs)]
        w = [x + _dot(m, x.astype(BF16)) for m, x in zip(pw, w)]
        for _ in range(NEUMANN_SQUARINGS):
            pw = [_dot(m, m).astype(BF16) for m in pw]
            w = [x + _dot(m, x.astype(BF16)) for m, x in zip(pw, w)]
        pm = [(x[:C, :LANES] + x[C:, :LANES]).astype(BF16) for x in w]
        q = [x[:C, LANES:] + x[C:, LANES:] for x in w]

        for i, un in enumerate(units):
            c, b, p = un
            cols = slice(p * LANES, (p + 1) * LANES)
            rows = slice(c * C, (c + 1) * C)
            n0 = n_ref[b, p]
            pr = _dot_nt(jnp.concatenate([pm[i], tile(rt_ref, un)], axis=0), n0.astype(BF16))
            u = pr[:C] + q[i]
            uv = jnp.concatenate([u.astype(BF16), tile(vb_ref, un)], axis=0)
            yp = _dot(gr[i], uv)
            y = pr[C:] + jnp.where(even, yp[:C], yp[C:])
            g_slab = ge_ref[b, c, :, cols]
            bkg = (bk[i].astype(F32) * g_slab).astype(BF16)
            dn = lax.dot_general(uv, bkg, (((0,), (0,)), ((), ())), preferred_element_type=F32)
            n_ref[b, p] = g_slab * n0 + jnp.where(same, dn, 0.0)
            mean = half_sum(y) * (1.0 / HEAD_DIM)
            yc = y - mean
            var = half_sum(yc * yc) * (1.0 / HEAD_DIM)
            yn = yc * lax.rsqrt(var + LNX_EPS) * lw_ref[:, cols] + lb_ref[:, cols]
            y_ref[b, rows, cols] = yn + bonus_ref[b, rows, cols]


def _wkv_chunked(at, rt, bt, kt, vb, ge, bonus, lnw, lnb):
    B, S, W = at.shape
    ts = WKV_ROWS
    nb = WKV_SEQS if B % WKV_SEQS == 0 else 1
    n_c = ts // CHUNK
    tile = lambda b, i: (b, i, 0)
    full = lambda b, i: (0, 0)
    tok = pl.BlockSpec((nb, ts, W), tile)
    return pl.pallas_call(
        _wkv_chunk_kernel,
        grid=(B // nb, S // ts),
        in_specs=[tok] * 5 + [pl.BlockSpec((nb, n_c, 1, W), lambda b, i: (b, i, 0, 0)), tok,
                              pl.BlockSpec((1, W), full), pl.BlockSpec((1, W), full)],
        out_specs=tok,
        out_shape=jax.ShapeDtypeStruct((B, S, W), F32),
        scratch_shapes=[pltpu.VMEM((nb, RWKV_HEADS // 2, 2 * HEAD_DIM, LANES), F32)],
        compiler_params=_cparams(("parallel", "arbitrary"), 32),
        name="wkv_chunked",
    )(at, rt, bt, kt, vb, ge, bonus, lnw, lnb)


def _band_attn_kernel(q_ref, k_ref, v_ref, bias_ref, o_ref, s_ref, e_ref, kz_ref, vz_ref):
    i = pl.program_id(1)
    head_rows = kz_ref.shape[0] - BAND_PAD

    @pl.when(i == 0)
    def _():
        kz_ref[0:BAND_PAD, :] = jnp.zeros((BAND_PAD, kz_ref.shape[1]), kz_ref.dtype)
        vz_ref[0:BAND_PAD, :] = jnp.zeros((BAND_PAD, vz_ref.shape[1]), vz_ref.dtype)
        kz_ref[BAND_PAD:, :] = k_ref[0:head_rows, :]
        vz_ref[BAND_PAD:, :] = v_ref[0:head_rows, :]

    n_chunks = q_ref.shape[0] // CHUNK
    n_pairs = ATT_HEADS // 2
    lane = lax.broadcasted_iota(jnp.int32, (CHUNK, LANES), 1)
    even = lane < HEAD_DIM
    slot = lax.broadcasted_iota(jnp.int32, (1, 1, BAND), 2)

    group = s_ref.shape[0]

    def chunk_body(masked, cc, carry):
        cgs = [i * n_chunks + cc * group + j for j in range(group)]
        if masked:
            kv_k, kv_v = kz_ref, vz_ref
            starts = [pl.multiple_of(cg * CHUNK, CHUNK) for cg in cgs]
        else:
            kv_k, kv_v = k_ref, v_ref
            starts = [pl.multiple_of((cg - PREV_CHUNKS) * CHUNK, CHUNK) for cg in cgs]
        rows = [pl.multiple_of((cc * group + j) * CHUNK, CHUNK) for j in range(group)]
        for j in range(group):
            for p in range(n_pairs):
                cols = slice(p * LANES, (p + 1) * LANES)
                q2 = q_ref[pl.ds(rows[j], CHUNK), cols]
                zero = jnp.zeros_like(q2)
                qs = jnp.concatenate([jnp.where(even, q2, zero), jnp.where(even, zero, q2)], axis=0)
                s_ref[j, p] = _dot_nt(qs, kv_k[pl.ds(starts[j], BAND), cols]) + bias_ref[p]
        inv = []
        for j in range(group):
            s = s_ref[j]
            if masked:
                s = jnp.where(slot >= (BAND_PAD - cgs[j] * CHUNK), s, NEG_INF)
            e = jnp.exp2(s - jnp.max(s, axis=-1, keepdims=True))
            inv.append(1.0 / jnp.sum(e, axis=-1, keepdims=True))
            e_ref[j] = e.astype(BF16)
        for j in range(group):
            for p in range(n_pairs):
                cols = slice(p * LANES, (p + 1) * LANES)
                o = _dot(e_ref[j, p], kv_v[pl.ds(starts[j], BAND), cols]) * inv[j][p]
                o_ref[pl.ds(rows[j], CHUNK), cols] = jnp.where(even, o[:CHUNK], o[CHUNK:])
        return carry

    @pl.when(i * n_chunks < PREV_CHUNKS)
    def _():
        lax.fori_loop(0, n_chunks // group, functools.partial(chunk_body, True), 0)

    @pl.when(i * n_chunks >= PREV_CHUNKS)
    def _():
        lax.fori_loop(0, n_chunks // group, functools.partial(chunk_body, False), 0)


def _band_attn(q3, k3, v3, bias_pairs):
    B, S, W = q3.shape
    SP = S
    tq = ATT_ROWS
    assert PREV_CHUNKS % (tq // CHUNK) == 0
    n_pairs = ATT_HEADS // 2
    return pl.pallas_call(
        _band_attn_kernel,
        grid=(B, S // tq),
        in_specs=[pl.BlockSpec((None, tq, W), lambda b, i: (b, i, 0)),
                  pl.BlockSpec((None, SP, W), lambda b, i: (b, 0, 0)),
                  pl.BlockSpec((None, SP, W), lambda b, i: (b, 0, 0)),
                  pl.BlockSpec((ATT_HEADS // 2, 2 * CHUNK, BAND), lambda b, i: (0, 0, 0))],
        out_specs=pl.BlockSpec((None, tq, W), lambda b, i: (b, i, 0)),
        out_shape=jax.ShapeDtypeStruct((B, S, W), F32),
        scratch_shapes=[pltpu.VMEM((ATT_GROUP, n_pairs, 2 * CHUNK, BAND), F32),
                        pltpu.VMEM((ATT_GROUP, n_pairs, 2 * CHUNK, BAND), BF16),
                        pltpu.VMEM((BAND_PAD + PREV_CHUNKS * CHUNK, W), BF16),
                        pltpu.VMEM((BAND_PAD + PREV_CHUNKS * CHUNK, W), BF16)],
        compiler_params=_cparams(("parallel", "arbitrary"), 48),
        name="band_attn",
    )(q3, k3, v3, bias_pairs)


def _mem_kv_kernel(m_ref, g_ref, w_ref, kn_ref, k_ref, v_ref):
    h = _rms(m_ref[...], g_ref[...]).astype(BF16)
    kv = _dot(h, w_ref[...])
    for hd in range(XA_HEADS):
        cols = slice(hd * XA_HEAD_DIM, (hd + 1) * XA_HEAD_DIM)
        k_ref[:, cols] = _rms(kv[:, cols], kn_ref[...]).astype(BF16)
    v_ref[...] = kv[:, XA_WIDTH:].astype(BF16)


def _mem_kv(mem2d, g, w_kv, k_norm):
    R, D = mem2d.shape
    tm = ROW_TILE
    full = lambda i: (0, 0)
    row = lambda i: (i, 0)
    return pl.pallas_call(
        _mem_kv_kernel,
        grid=(R // tm,),
        in_specs=[pl.BlockSpec((tm, D), row), pl.BlockSpec((1, D), full),
                  pl.BlockSpec((D, 2 * XA_WIDTH), full), pl.BlockSpec((1, XA_HEAD_DIM), full)],
        out_specs=[pl.BlockSpec((tm, XA_WIDTH), row)] * 2,
        out_shape=[jax.ShapeDtypeStruct((R, XA_WIDTH), BF16)] * 2,
        compiler_params=_cparams(("parallel",), 32),
        name="mem_kv",
    )(mem2d, g, w_kv, k_norm)


def _mix_xa_router_kernel(x_ref, yr_ref, g_ref, ya_ref, gao_ref, wout_ref, gxa_ref, wq_ref, qn_ref,
                          km_ref, vm_ref, wo_ref, gff_ref, wrt_ref, brt_ref,
                          tri_ref, xo_ref, h_ref, rt_ref, cnt_ref, run_ref, stage_ref):
    first = (pl.program_id(0) == 0) & (pl.program_id(1) == 0)

    @pl.when(first)
    def _():
        run_ref[...] = jnp.zeros_like(run_ref)

    n_rows = x_ref.shape[0]
    groups = [slice(r, r + n_rows // MIX_GROUPS) for r in range(0, n_rows, n_rows // MIX_GROUPS)]
    head_cols = [slice(hd * XA_HEAD_DIM, (hd + 1) * XA_HEAD_DIM) for hd in range(XA_HEADS)]

    yr = [(yr_ref[rows, :] * g_ref[rows, :]).astype(BF16) for rows in groups]
    ya = [_rms(ya_ref[rows, :], gao_ref[...]).astype(BF16) for rows in groups]
    x1 = [x_ref[rows, :] + _dot(a, wout_ref[0:RWKV_WIDTH, :]) + _dot(b, wout_ref[RWKV_WIDTH:, :])
          for rows, a, b in zip(groups, yr, ya)]

    q = [_dot(_rms(xg, gxa_ref[...]).astype(BF16), wq_ref[...]) for xg in x1]
    qscale = XA_HEAD_DIM ** -0.5 * LOG2E
    qh = [[(_rms(qg[:, cols], qn_ref[...]) * qscale).astype(BF16) for cols in head_cols] for qg in q]
    s = [[_dot_nt(qg[hd], km_ref[:, cols]) for hd, cols in enumerate(head_cols)] for qg in qh]
    e = [[jnp.exp2(sh - jnp.max(sh, axis=-1, keepdims=True)) for sh in sg] for sg in s]
    inv = [[1.0 / jnp.sum(eh, axis=-1, keepdims=True) for eh in eg] for eg in e]
    o = [jnp.concatenate([_dot(eg[hd].astype(BF16), vm_ref[:, cols]) * ig[hd]
                          for hd, cols in enumerate(head_cols)], axis=-1).astype(BF16)
         for eg, ig in zip(e, inv)]
    x = jnp.concatenate([xg + _dot(og, wo_ref[...]) for xg, og in zip(x1, o)], axis=0)
    xo_ref[...] = x

    h = _rms(x, gff_ref[...])
    tm = h.shape[0]
    _store_row_tiles(stage_ref, h)
    h_ref[...] = stage_ref[...].astype(BF16)
    logits =_dot(h.astype(BF16), wrt_ref[...]) + brt_ref[...]
    lane_i = lax.broadcasted_iota(jnp.int32, logits.shape, 1)
    lane = lane_i.astype(F32)
    big = float(ROUTE_LANES)
    gmask = lane < N_GROUPS
    gmax = jnp.max(jnp.where(gmask, logits, -jnp.inf), axis=-1, keepdims=True)
    ge = jnp.where(gmask, jnp.exp(logits - gmax), 0.0)
    gp = ge / jnp.sum(ge, axis=-1, keepdims=True)
    g_gate = jnp.max(gp, axis=-1, keepdims=True)
    g_top = jnp.min(jnp.where(gmask & (gp == g_gate), lane, big), axis=-1, keepdims=True)
    lo = N_GROUPS + g_top * EXPERTS_PER_GROUP
    rmask = (lane >= lo) & (lane < lo + EXPERTS_PER_GROUP)
    rmax = jnp.max(jnp.where(rmask, logits, -jnp.inf), axis=-1, keepdims=True)
    re = jnp.where(rmask, jnp.exp(logits - rmax), 0.0)
    p_in = re / jnp.sum(re, axis=-1, keepdims=True)
    p1 = jnp.max(jnp.where(rmask, p_in, -1.0), axis=-1, keepdims=True)
    j1 = jnp.min(jnp.where(rmask & (p_in == p1), lane, big), axis=-1, keepdims=True)
    rmask2 = rmask & (lane != j1)
    p2 = jnp.max(jnp.where(rmask2, p_in, -1.0), axis=-1, keepdims=True)
    j2 = jnp.min(jnp.where(rmask2 & (p_in == p2), lane, big), axis=-1, keepdims=True)
    denom = p1 + p2
    w1 = g_gate * (p1 / denom)
    w2 = g_gate * (p2 / denom)
    e1 = j1 - N_GROUPS
    e2 = j2 - N_GROUPS
    hit1 = lane == j1
    hit2 = lane == j2
    hits = jnp.where(hit1 | hit2, 1.0, 0.0)
    before = _dot(tri_ref[...], hits.astype(BF16)) + run_ref[...]
    rank1 = jnp.sum(jnp.where(hit1, before, 0.0), axis=-1, keepdims=True)
    rank2 = jnp.sum(jnp.where(hit2, before, 0.0), axis=-1, keepdims=True)
    run = run_ref[...] + jnp.sum(hits, axis=0, keepdims=True)
    run_ref[...] = run
    cnt_ref[...] = jnp.broadcast_to(run, cnt_ref.shape)
    rt_ref[...] = jnp.where(lane_i == 0, e1, jnp.where(lane_i == 1, e2,
                            jnp.where(lane_i == 2, w1, jnp.where(lane_i == 3, w2,
                                      jnp.where(lane_i == 4, rank1, jnp.where(lane_i == 5, rank2, 0.0))))))


def _mix_xa_router(x3, yr3, g3, ya3, gao, wout, gxa, wq, qn, km3, vm3, wo, gff, wrt, brt):
    B, S, D = x3.shape
    M = km3.shape[1]
    tm = MIX_ROWS
    tile = lambda b, i: (b, i, 0)
    full = lambda b, i: (0, 0)
    memb = lambda b, i: (b, 0, 0)
    W = RWKV_WIDTH
    n_i = S // tm
    tri = jnp.tril(jnp.ones((tm, tm), F32), -1).astype(BF16)
    return pl.pallas_call(
        _mix_xa_router_kernel,
        grid=(B, S // tm),
        in_specs=[pl.BlockSpec((None, tm, D), tile), pl.BlockSpec((None, tm, W), tile),
                  pl.BlockSpec((None, tm, W), tile), pl.BlockSpec((None, tm, W), tile),
                  pl.BlockSpec((1, W), full), pl.BlockSpec((2 * W, D), full),
                  pl.BlockSpec((1, D), full), pl.BlockSpec((D, XA_WIDTH), full),
                  pl.BlockSpec((1, XA_HEAD_DIM), full),
                  pl.BlockSpec((None, M, XA_WIDTH), memb), pl.BlockSpec((None, M, XA_WIDTH), memb),
                  pl.BlockSpec((XA_WIDTH, D), full), pl.BlockSpec((1, D), full),
                  pl.BlockSpec((D, ROUTE_LANES), full), pl.BlockSpec((1, ROUTE_LANES), full),
                  pl.BlockSpec((tm, tm), full)],
        out_specs=[pl.BlockSpec((None, tm, D), tile),
                   pl.BlockSpec((tm * SUBLANES, LANES), lambda b, i: (b * n_i + i, 0)),
                   pl.BlockSpec((None, tm, ROUTE_LANES), tile),
                   pl.BlockSpec((SUBLANES, ROUTE_LANES), full)],
        out_shape=[jax.ShapeDtypeStruct((B, S, D), F32),
                   jax.ShapeDtypeStruct((B * S * SUBLANES, LANES), BF16),
                   jax.ShapeDtypeStruct((B, S, ROUTE_LANES), F32),
                   jax.ShapeDtypeStruct((SUBLANES, ROUTE_LANES), F32)],
        scratch_shapes=[pltpu.VMEM((1, ROUTE_LANES), F32), pltpu.VMEM((tm * SUBLANES, LANES), F32)],
        compiler_params=_cparams(("arbitrary", "arbitrary"), 48),
        name="mix_xa_router",
    )(x3, yr3, g3, ya3, gao, wout, gxa, wq, qn, km3, vm3, wo, gff, wrt, brt, tri)


def _row_copy(src_ref, src_off, dst_ref, dst_off, sem):
    src = src_ref.at[pl.ds(pl.multiple_of(src_off, SUBLANES), SUBLANES)]
    dst = dst_ref.at[pl.ds(pl.multiple_of(dst_off, SUBLANES), SUBLANES)]
    return pltpu.make_async_copy(src, dst, sem)


def _load_row_tiles(stage_ref, rows):
    return jnp.concatenate([stage_ref[pl.ds(j, rows, stride=SUBLANES), :] for j in range(SUBLANES)], axis=-1)


def _store_row_tiles(stage_ref, x):
    for j in range(SUBLANES):
        stage_ref[pl.ds(j, x.shape[0], stride=SUBLANES), :] = x[:, j * LANES:(j + 1) * LANES]


def _dispatch_kernel(dest_ref, h_ref, zeros_ref, xs_ref, sem):
    del zeros_ref
    tm = h_ref.shape[0] // SUBLANES
    base = pl.program_id(0) * (tm * TOP_K)

    def issue(r, carry):
        for kk in range(TOP_K):
            _row_copy(h_ref, r * SUBLANES, xs_ref, dest_ref[base + r * TOP_K + kk], sem).start(priority=kk)
        return carry

    lax.fori_loop(0, tm, issue, 0, unroll=DMA_UNROLL)

    def drain(r, carry):
        for kk in range(TOP_K):
            _row_copy(h_ref, r * SUBLANES, xs_ref, dest_ref[base + r * TOP_K + kk], sem).wait()
        return carry

    lax.fori_loop(0, tm, drain, 0, unroll=DMA_UNROLL)


def _dispatch(dest, h_tiles, n_slots):
    tm = MOE_ROWS
    T = h_tiles.shape[0] // SUBLANES
    zeros = jnp.zeros((n_slots * SUBLANES, LANES), BF16)
    return pl.pallas_call(
        _dispatch_kernel,
        grid_spec=pltpu.PrefetchScalarGridSpec(
            num_scalar_prefetch=1,
            grid=(T // tm,),
            in_specs=[pl.BlockSpec((tm * SUBLANES, LANES), lambda i, d: (i, 0)),
                      pl.BlockSpec(memory_space=pl.ANY)],
            out_specs=pl.BlockSpec(memory_space=pl.ANY),
            scratch_shapes=[pltpu.SemaphoreType.DMA(())],
        ),
        out_shape=jax.ShapeDtypeStruct((n_slots * SUBLANES, LANES), BF16),
        input_output_aliases={2: 0},
        compiler_params=_cparams(("arbitrary",), 32),
        name="moe_dispatch",
    )(dest, h_tiles, zeros)


def _expert_kernel(be_ref, nu_ref, xs_ref, wg_ref, wu_ref, wd_ref, y_ref, wgb_ref, wub_ref, wdb_ref,
                   stage_ref):
    i = pl.program_id(0)
    used = i < nu_ref[0]
    new_expert = (i == 0) | (be_ref[i] != be_ref[jnp.maximum(i - 1, 0)])

    @pl.when(used & new_expert)
    def _():
        wgb_ref[...] = wg_ref[...].astype(BF16)
        wub_ref[...] = wu_ref[...].astype(BF16)
        wdb_ref[...] = wd_ref[...].astype(BF16)

    @pl.when(used)
    def _():
        stage_ref[...] = xs_ref[...].astype(F32)
        xb = _load_row_tiles(stage_ref, MOE_BLOCK).astype(BF16)
        gate = _dot(xb, wgb_ref[...])
        up = _dot(xb, wub_ref[...])
        act = (gate * jax.nn.sigmoid(gate) * up).astype(BF16)
        _store_row_tiles(stage_ref, _dot(act, wdb_ref[...]))
        y_ref[...] = stage_ref[...].astype(BF16)

    @pl.when(jnp.logical_not(used))
    def _():
        y_ref[...] = jnp.zeros_like(y_ref)


def _expert_ffn(block_expert, n_used, xs_tiles, layer, wg, wu, wd):
    D, FF = wg.shape[2], wg.shape[3]
    n_blocks = xs_tiles.shape[0] // (MOE_BLOCK * SUBLANES)
    rows = MOE_BLOCK * SUBLANES
    blk = lambda i, be, nu: (jnp.minimum(i, nu[0] - 1), 0)
    wsel = lambda i, be, nu: (layer, be[jnp.minimum(i, nu[0] - 1)], 0, 0)
    return pl.pallas_call(
        _expert_kernel,
        grid_spec=pltpu.PrefetchScalarGridSpec(
            num_scalar_prefetch=2,
            grid=(n_blocks,),
            in_specs=[pl.BlockSpec((rows, LANES), blk),
                      pl.BlockSpec((None, None, D, FF), wsel),
                      pl.BlockSpec((None, None, D, FF), wsel),
                      pl.BlockSpec((None, None, FF, D), wsel)],
            out_specs=pl.BlockSpec((rows, LANES), lambda i, be, nu: (i, 0)),
            scratch_shapes=[pltpu.VMEM((D, FF), BF16), pltpu.VMEM((D, FF), BF16), pltpu.VMEM((FF, D), BF16),
                            pltpu.VMEM((rows, LANES), F32)],
        ),
        out_shape=jax.ShapeDtypeStruct(xs_tiles.shape, BF16),
        compiler_params=_cparams(("arbitrary",), 48),
        name="moe_experts",
    )(block_expert, n_used, xs_tiles, wg, wu, wd)


def _combine_kernel(dest_ref, x_ref, wt_ref, yb_ref, o_ref, buf_ref, sem, stage_ref):
    tm = x_ref.shape[0]
    base = pl.program_id(0) * (tm * TOP_K)

    def issue(r, carry):
        for kk in range(TOP_K):
            _row_copy(yb_ref, dest_ref[base + r * TOP_K + kk], buf_ref.at[kk], r * SUBLANES, sem).start(priority=kk)
        return carry

    lax.fori_loop(0, tm, issue, 0, unroll=DMA_UNROLL)

    def drain(r, carry):
        for kk in range(TOP_K):
            _row_copy(yb_ref, dest_ref[base + r * TOP_K + kk], buf_ref.at[kk], r * SUBLANES, sem).wait()
        return carry

    lax.fori_loop(0, tm, drain, 0, unroll=DMA_UNROLL)
    wt = wt_ref[...]
    w1 = wt[:, 2:3]
    w2 = wt[:, 3:4]
    stage_ref[...] = buf_ref[...].astype(F32)
    for j in range(SUBLANES):
        cols = slice(j * LANES, (j + 1) * LANES)
        y1 = stage_ref[0, pl.ds(j, tm, stride=SUBLANES), :]
        y2 = stage_ref[1, pl.ds(j, tm, stride=SUBLANES), :]
        o_ref[:, cols] = x_ref[:, cols] + (y1 * w1 + y2 * w2)


def _combine(dest, x2d, rt2d, yb_tiles):
    T, D = x2d.shape
    tm = MOE_ROWS
    return pl.pallas_call(
        _combine_kernel,
        grid_spec=pltpu.PrefetchScalarGridSpec(
            num_scalar_prefetch=1,
            grid=(T // tm,),
            in_specs=[pl.BlockSpec((tm, D), lambda i, d: (i, 0)),
                      pl.BlockSpec((tm, ROUTE_LANES), lambda i, d: (i, 0)),
                      pl.BlockSpec(memory_space=pl.ANY)],
            out_specs=pl.BlockSpec((tm, D), lambda i, d: (i, 0)),
            scratch_shapes=[pltpu.VMEM((TOP_K, tm * SUBLANES, LANES), BF16), pltpu.SemaphoreType.DMA(()),
                            pltpu.VMEM((TOP_K, tm * SUBLANES, LANES), F32)],
        ),
        out_shape=jax.ShapeDtypeStruct((T, D), F32),
        compiler_params=_cparams(("arbitrary",), 48),
        name="moe_combine",
    )(dest, x2d, rt2d, yb_tiles)


def _moe_plan(rt2d, counts_lanes):
    T = rt2d.shape[0]
    A = T * TOP_K
    e = rt2d[:, 0:TOP_K].astype(jnp.int32).reshape(A)
    rank = rt2d[:, 4:4 + TOP_K].astype(jnp.int32).reshape(A)
    counts = counts_lanes[0, N_GROUPS:N_GROUPS + N_EXPERTS].astype(jnp.int32)
    padded = ((counts + MOE_BLOCK - 1) // MOE_BLOCK) * MOE_BLOCK
    pends = jnp.cumsum(padded)
    pstarts = pends - padded
    onehot = e[:, None] == jnp.arange(N_EXPERTS, dtype=jnp.int32)[None, :]
    dest = jnp.sum(jnp.where(onehot, pstarts[None, :], 0), axis=1) + rank
    n_blocks = (A + N_EXPERTS * (MOE_BLOCK - 1) + MOE_BLOCK - 1) // MOE_BLOCK
    block_start = jnp.arange(n_blocks, dtype=jnp.int32) * MOE_BLOCK
    block_expert = jnp.minimum(jnp.sum(pends[None, :] <= block_start[:, None], axis=1), N_EXPERTS - 1)
    n_used = (pends[-1] // MOE_BLOCK).astype(jnp.int32).reshape(1)
    dest_off = (dest * SUBLANES).astype(jnp.int32)
    return dest_off, block_expert.astype(jnp.int32), n_used, n_blocks * MOE_BLOCK


def _rel_bias_pairs(table):
    n_ext = BAND + CHUNK - 1
    ext = jnp.concatenate([table, jnp.broadcast_to(table[:, -1:], (table.shape[0], n_ext - table.shape[1]))], axis=1)
    rev = ext[:, ::-1]
    bias = jnp.stack([rev[:, CHUNK - 1 - i:CHUNK - 1 - i + BAND] for i in range(CHUNK)], axis=1)
    return (bias.astype(F32) * LOG2E).reshape(-1, ATT_HEADS // 2, 2 * CHUNK, BAND)


def _pad_rows(w, rows, offset=0):
    out = jnp.zeros((rows, w.shape[1]), w.dtype)
    return out.at[offset:offset + w.shape[0]].set(w)


def kernel(x, mem, norm_mix, w_in, shift_mu, decay_w0, decay_w2, iclr_a0, iclr_a2, gate_g2, rw_k_k, rw_k_a, rw_r_k, lnx_w, lnx_b, vres_v0, vres_v1, vres_v2, att_q_norm, att_k_norm, att_rel_bias, att_out_norm, w_out, norm_xa, norm_mem, xa_w_q, xa_w_kv, xa_w_o, xa_q_norm, xa_k_norm, norm_ffn, moe_w_group, moe_b_group, moe_w_route, moe_b_route, moe_w_gate, moe_w_up, moe_w_down):
    B, S, D = x.shape
    T = B * S
    depth = w_in.shape[0]
    M = mem.shape[1]
    row = lambda p: p.reshape(1, -1)
    ones_bd = jnp.kron(jnp.eye(ATT_HEADS, dtype=F32), jnp.ones((HEAD_DIM, HEAD_DIM), F32)).astype(BF16)

    bias_pairs = _rel_bias_pairs(att_rel_bias.reshape(depth * ATT_HEADS, -1))

    v_first = None
    for l in range(depth):
        u, qh, kh, vh = _inproj(x.reshape(T, D), row(norm_mix[l]), w_in[l].astype(BF16),
                                row(jnp.tile(att_q_norm[l], ATT_HEADS)),
                                row(jnp.tile(att_k_norm[l], ATT_HEADS)))
        w2p = _pad_rows(decay_w2[l], LANES, 0).astype(BF16)
        a2p = _pad_rows(iclr_a2[l], LANES, DECAY_LORA).astype(BF16)
        vres = None
        if l > 0:
            v1p = jnp.zeros((RWKV_WIDTH, LANES), F32).at[:, :MV_LORA].set(vres_v1[l - 1]).astype(BF16)
            v2p = _pad_rows(vres_v2[l - 1], LANES, 0).astype(BF16)
            vres = (v_first, row(vres_v0[l - 1]), v1p, v2p)
        prep = _rwkv_prep(u.reshape(B, S, RWKV_IN), row(shift_mu[l]), row(decay_w0[l]),
                          w2p, row(iclr_a0[l]), a2p, gate_g2[l].astype(BF16),
                          row(rw_k_k[l]), row(rw_k_a[l]), row(rw_r_k[l]), ones_bd, vres)
        at3, rt3, bt3, kt3, vb3, ge4, bonus3, g3 = prep[:8]
        if l == 0:
            v_first = prep[8]
        yr3 = _wkv_chunked(at3, rt3, bt3, kt3, vb3, ge4, bonus3, row(lnx_w[l]), row(lnx_b[l]))
        ya3 = _band_attn(qh.reshape(B, S, ATT_WIDTH), kh.reshape(B, S, ATT_WIDTH),
                         vh.reshape(B, S, ATT_WIDTH), bias_pairs[l])
        km, vm = _mem_kv(mem.reshape(B * M, D), row(norm_mem[l]), xa_w_kv[l].astype(BF16), row(xa_k_norm[l]))
        wrt = jnp.zeros((D, ROUTE_LANES), F32)
        wrt = wrt.at[:, :N_GROUPS].set(moe_w_group[l]).at[:, N_GROUPS:N_GROUPS + N_EXPERTS].set(moe_w_route[l])
        brt = jnp.zeros((ROUTE_LANES,), F32)
        brt = brt.at[:N_GROUPS].set(moe_b_group[l]).at[N_GROUPS:N_GROUPS + N_EXPERTS].set(moe_b_route[l])
        x3, h_tiles, rt3, counts = _mix_xa_router(
            x, yr3, g3, ya3, row(att_out_norm[l]), w_out[l].astype(BF16), row(norm_xa[l]),
            xa_w_q[l].astype(BF16), row(xa_q_norm[l]), km.reshape(B, M, XA_WIDTH), vm.reshape(B, M, XA_WIDTH),
            xa_w_o[l].astype(BF16), row(norm_ffn[l]), wrt.astype(BF16), row(brt))
        rt2d = rt3.reshape(T, ROUTE_LANES)
        dest, block_expert, n_used, n_slots = _moe_plan(rt2d, counts)
        xs = _dispatch(dest, h_tiles, n_slots)
        yb = _expert_ffn(block_expert, n_used, xs, l, moe_w_gate, moe_w_up, moe_w_down)
        x = _combine(dest, x3.reshape(T, D), rt2d, yb).reshape(B, S, D)
    return x
```
